```python
import jax, jax.numpy as jnp
from jax import lax
import numpy as np

D_MODEL = 2048
BATCH = 16
SEQ = 256
DEPTH = 2
DEC_BATCH = 8
DEC_SEQ = 4096
PAST_LEN = 512

GRID_W = 64
N_EVEN = (DEPTH + 1) // 2
N_ODD = DEPTH // 2
EPS = 1e-6
ATTN_WIDTH = D_MODEL // 2
HEAD_DIM = 64
N_HEADS_A = ATTN_WIDTH // HEAD_DIM
N_KV_A = max(1, N_HEADS_A // 8)
GQA_GROUP = N_HEADS_A // N_KV_A
WINDOW = 128
BLOCK = 128
ROPE_BASE = 10000.0
HY_WIDTH = D_MODEL - ATTN_WIDTH
HY_ORDER = 2
HY_SHORT = 3
HY_BANDS = 16
HY_EMB = 2 * HY_BANDS + 1
HY_HIDDEN = 64
LRU_WIDTH = D_MODEL
N_HEADS_C = 8
LRU_BLOCK = LRU_WIDTH // N_HEADS_C
LRU_CONV = 4
LRU_C = 8.0

Q_COLS = N_HEADS_A * HEAD_DIM
KV_COLS = N_KV_A * HEAD_DIM
EVEN_SPLITS = (Q_COLS, KV_COLS, KV_COLS, ATTN_WIDTH, (HY_ORDER + 1) * HY_WIDTH, HY_WIDTH)
EVEN_IN_COLS = Q_COLS + 2 * KV_COLS + ATTN_WIDTH + (HY_ORDER + 2) * HY_WIDTH
EVEN_OUT_ROWS = ATTN_WIDTH + HY_WIDTH

kernel_name = 'hybrid_diffusion_attn_hyena_rglru_step'

F32 = jnp.float32


def _split_cols(x, sizes):
    idx = np.cumsum(sizes)[:-1].tolist()
    return jnp.split(x, idx, axis=-1)


def _rmsnorm(x, g):
    xf = x.astype(F32)
    y = xf * lax.rsqrt(jnp.mean(xf * xf, axis=-1, keepdims=True) + EPS)
    return (y * g.astype(F32)).astype(x.dtype)


def _modulation(cvec, w, b):
    m = jax.nn.silu(cvec) @ w + b
    return jnp.split(m[:, None, :], 3, axis=-1)


def _depthwise_conv(x, w, b, left):
    K = w.shape[0]
    L = x.shape[1]
    xp = jnp.pad(x, ((0, 0), (left, K - 1 - left), (0, 0)))
    out = xp[:, 0:L] * w[0]
    for j in range(1, K):
        out = out + xp[:, j:j + L] * w[j]
    return out + b


def _axial_rope(x):
    L = x.shape[1]
    rows = L // GRID_W
    row = jnp.repeat(jnp.arange(rows), GRID_W).astype(F32)
    col = jnp.tile(jnp.arange(GRID_W), rows).astype(F32)
    half = HEAD_DIM // 2
    nf = half // 2
    inv = ROPE_BASE ** (-jnp.arange(nf, dtype=F32) / nf)
    shape = (1, L) + (1,) * (x.ndim - 3) + (nf,)

    def rot(xa, pos):
        ang = pos[:, None] * inv[None, :]
        cos = jnp.cos(ang).reshape(shape).astype(x.dtype)
        sin = jnp.sin(ang).reshape(shape).astype(x.dtype)
        x1, x2 = xa[..., :nf], xa[..., nf:]
        return jnp.concatenate([x1 * cos - x2 * sin, x2 * cos + x1 * sin], axis=-1)

    return jnp.concatenate([rot(x[..., :half], row), rot(x[..., half:], col)], axis=-1)


def _sink_softmax(s, sink):
    s_sink = jnp.broadcast_to(sink.astype(F32)[None, :, :, None, None], s.shape[:-1] + (1,))
    return jax.nn.softmax(jnp.concatenate([s, s_sink], axis=-1), axis=-1)[..., :-1]


def _context_attention(q, k, v, sink):
    T = q.shape[1]
    scale = HEAD_DIM ** -0.5

    def block(i):
        qb = lax.dynamic_slice_in_dim(q, i * BLOCK, BLOCK, axis=1)
        s = jnp.einsum('bqkgd,bskd->bkgqs', qb, k).astype(F32) * scale
        p = _sink_softmax(s, sink).astype(v.dtype)
        return jnp.einsum('bkgqs,bskd->bqkgd', p, v)

    out = lax.map(block, jnp.arange(T // BLOCK))
    return jnp.moveaxis(out, 0, 1).reshape(q.shape)


def _latent_attention(q, k, v, k_ctx, v_ctx, sink):
    L = q.shape[1]
    Tc = k_ctx.shape[1]
    span = BLOCK + 2 * WINDOW
    scale = HEAD_DIM ** -0.5
    neg = jnp.finfo(F32).min
    kp = jnp.pad(k, ((0, 0), (WINDOW, WINDOW), (0, 0), (0, 0)))
    vp = jnp.pad(v, ((0, 0), (WINDOW, WINDOW), (0, 0), (0, 0)))

    def block(i):
        start = i * BLOCK
        qb = lax.dynamic_slice_in_dim(q, start, BLOCK, axis=1)
        kb = lax.dynamic_slice_in_dim(kp, start, span, axis=1)
        vb = lax.dynamic_slice_in_dim(vp, start, span, axis=1)
        qpos = start + jnp.arange(BLOCK)
        kpos = start - WINDOW + jnp.arange(span)
        valid = (jnp.abs(qpos[:, None] - kpos[None, :]) <= WINDOW) & (kpos >= 0)[None, :] & (kpos < L)[None, :]
        s_loc = jnp.einsum('bqkgd,bskd->bkgqs', qb, kb).astype(F32) * scale
        s_loc = jnp.where(valid, s_loc, neg)
        s_ctx = jnp.einsum('bqkgd,bskd->bkgqs', qb, k_ctx).astype(F32) * scale
        p = _sink_softmax(jnp.concatenate([s_ctx, s_loc], axis=-1), sink).astype(v.dtype)
        return (jnp.einsum('bkgqs,bskd->bqkgd', p[..., :Tc], v_ctx)
                + jnp.einsum('bkgqs,bskd->bqkgd', p[..., Tc:], vb))

    out = lax.map(block, jnp.arange(L // BLOCK))
    return jnp.moveaxis(out, 0, 1).reshape(q.shape)


def _hyena_filters(L, w1, b1, w2, b2, w3, decay):
    pos = jnp.arange(L, dtype=F32)
    t = pos / L
    freqs = jnp.linspace(1e-4, HY_BANDS - 1, HY_BANDS, dtype=F32)
    ang = 2.0 * jnp.pi * t[:, None] * freqs[None, :]
    z = jnp.concatenate([t[:, None], jnp.cos(ang), -jnp.sin(ang)], axis=-1)
    h = jnp.sin(z @ w1.astype(F32) + b1.astype(F32))
    h = jnp.sin(h @ w2.astype(F32) + b2.astype(F32))
    h = (h @ w3.astype(F32)).reshape(L, 2, HY_ORDER, HY_WIDTH)
    h = h * jnp.exp(-t[:, None, None, None] * jnp.abs(decay.astype(F32))[None])
    taps = jnp.concatenate([h[:, 0], jnp.zeros((1, HY_ORDER, HY_WIDTH), F32), h[:0:-1, 1]], axis=0)
    taps = taps * lax.rsqrt(jnp.sum(taps * taps, axis=0, keepdims=True) + EPS)
    return jnp.fft.rfft(taps, axis=0)


def _hyena(u, short_w, short_b, w1, b1, w2, b2, w3, decay, hy_bias):
    L = u.shape[1]
    uc = _depthwise_conv(u, short_w, short_b, left=(HY_SHORT - 1) // 2)
    x1, x2, z = jnp.split(uc, 3, axis=-1)
    hf = _hyena_filters(L, w1, b1, w2, b2, w3, decay)
    zf = z.astype(F32)
    for n, xn in enumerate((x1, x2)):
        conv = jnp.fft.irfft(jnp.fft.rfft(zf, n=2 * L, axis=1) * hf[None, :, n], n=2 * L, axis=1)[:, :L]
        zf = xn.astype(F32) * (conv + hy_bias[n].astype(F32) * zf)
    return zf.astype(u.dtype)


def _even_mixer(h, w_in, w_out, sink, short_w, short_b, w1, b1, w2, b2, w3, decay, hy_bias, ctx_k=None, ctx_v=None):
    B, L, _ = h.shape
    q, k, v, g_attn, hy_in, g_hy = _split_cols(h @ w_in, EVEN_SPLITS)
    q = q.reshape(B, L, N_KV_A, GQA_GROUP, HEAD_DIM)
    k = k.reshape(B, L, N_KV_A, HEAD_DIM)
    v = v.reshape(B, L, N_KV_A, HEAD_DIM)
    sink = sink.reshape(N_KV_A, GQA_GROUP)
    if ctx_k is None:
        att = _context_attention(q, k, v, sink)
        kv_out = (k, v)
    else:
        att = _latent_attention(_axial_rope(q), _axial_rope(k), v, ctx_k, ctx_v, sink)
        kv_out = None
    att = att.reshape(B, L, ATTN_WIDTH)
    hy = _hyena(hy_in, short_w, short_b, w1, b1, w2, b2, w3, decay, hy_bias)
    mixed = jnp.concatenate([att * jax.nn.silu(g_attn), hy * jax.nn.silu(g_hy)], axis=-1)
    return mixed @ w_out, kv_out


def _lru_combine(left, right):
    a1, b1 = left
    a2, b2 = right
    return a1 * a2, a2 * b1 + b2


def _odd_mixer(h, w_in, w_out, conv_w, conv_b, wa, ba, wx, bx, lam, h0=None):
    B, L, _ = h.shape
    xb, gate = jnp.split(h @ w_in, 2, axis=-1)
    xc = _depthwise_conv(xb, conv_w, conv_b, left=LRU_CONV // 2)
    xr = xc.reshape(B, L, N_HEADS_C, LRU_BLOCK)
    r = jax.nn.sigmoid((jnp.einsum('blhi,dhij->dblhj', xr, wa).reshape(2, B, L, LRU_WIDTH) + ba[:, None, None]).astype(F32))
    i_g = jax.nn.sigmoid((jnp.einsum('blhi,dhij->dblhj', xr, wx).reshape(2, B, L, LRU_WIDTH) + bx[:, None, None]).astype(F32))
    log_a = -LRU_C * r * jax.nn.softplus(-lam.astype(F32))[:, None, None]
    a = jnp.exp(log_a)
    b = jnp.sqrt(-jnp.expm1(2.0 * log_a)) * i_g * xc.astype(F32)[None]
    a = jnp.stack([a[0], a[1, :, ::-1]])
    b = jnp.stack([b[0], b[1, :, ::-1]])
    a_cum, hs = lax.associative_scan(_lru_combine, (a, b), axis=2)
    if h0 is not None:
        hs = hs + a_cum * h0.astype(F32)[:, :, None]
    final = hs[:, :, -1]
    y = (hs[0] + hs[1, :, ::-1]).astype(h.dtype)
    return (y * jax.nn.silu(gate)) @ w_out, final


def setup_inputs(seed: int = 0) -> dict:
    key = jax.random.key(seed)
    ks = jax.random.split(key, 32)

    def nrm(k, shape, scale):
        return jax.random.normal(k, shape, F32) * scale

    u = jax.random.uniform(ks[31], (N_ODD, 2, LRU_WIDTH), F32, 0.9, 0.999)
    s = u ** (1.0 / LRU_C)
    return {
        'x_prompt': nrm(ks[0], (BATCH, SEQ, D_MODEL), 1.0),
        'x_sample': nrm(ks[1], (DEC_BATCH, DEC_SEQ, D_MODEL), 1.0),
        'cache_k': nrm(ks[2], (DEC_BATCH, N_EVEN, PAST_LEN, N_KV_A, HEAD_DIM), 1.0),
        'cache_v': nrm(ks[3], (DEC_BATCH, N_EVEN, PAST_LEN, N_KV_A, HEAD_DIM), 1.0),
        'state_lru': nrm(ks[4], (DEC_BATCH, N_ODD, 2, LRU_WIDTH), 0.5),
        'c': nrm(ks[5], (DEC_BATCH, D_MODEL), 1.0),
        'c_ctx': nrm(ks[6], (D_MODEL,), 1.0),
        'mod_w': nrm(ks[7], (DEPTH, D_MODEL, 3 * D_MODEL), 0.5 * D_MODEL ** -0.5),
        'mod_b': nrm(ks[8], (DEPTH, 3 * D_MODEL), 0.02),
        'norm_g': 1.0 + nrm(ks[9], (DEPTH, D_MODEL), 0.05),
        'final_norm_g': 1.0 + nrm(ks[10], (D_MODEL,), 0.05),
        'a_w_in': nrm(ks[11], (N_EVEN, D_MODEL, EVEN_IN_COLS), D_MODEL ** -0.5),
        'a_w_out': nrm(ks[12], (N_EVEN, EVEN_OUT_ROWS, D_MODEL), EVEN_OUT_ROWS ** -0.5),
        'a_sink': nrm(ks[13], (N_EVEN, N_HEADS_A), 0.5),
        'hy_short_w': nrm(ks[14], (N_EVEN, HY_SHORT, 3 * HY_WIDTH), HY_SHORT ** -0.5),
        'hy_short_b': nrm(ks[15], (N_EVEN, 3 * HY_WIDTH), 0.02),
        'hy_w1': nrm(ks[16], (N_EVEN, HY_EMB, HY_HIDDEN), HY_EMB ** -0.5),
        'hy_b1': nrm(ks[17], (N_EVEN, HY_HIDDEN), 0.5),
        'hy_w2': nrm(ks[18], (N_EVEN, HY_HIDDEN, HY_HIDDEN), HY_HIDDEN ** -0.5),
        'hy_b2': nrm(ks[19], (N_EVEN, HY_HIDDEN), 0.5),
        'hy_w3': nrm(ks[20], (N_EVEN, HY_HIDDEN, 2 * HY_ORDER * HY_WIDTH), HY_HIDDEN ** -0.5),
        'hy_decay': jax.random.uniform(ks[21], (N_EVEN, 2, HY_ORDER, HY_WIDTH), F32, 3.07, 15.35),
        'hy_bias': nrm(ks[22], (N_EVEN, HY_ORDER, HY_WIDTH), 0.3),
        'c_w_in': nrm(ks[23], (N_ODD, D_MODEL, 2 * LRU_WIDTH), D_MODEL ** -0.5),
        'c_w_out': nrm(ks[24], (N_ODD, LRU_WIDTH, D_MODEL), LRU_WIDTH ** -0.5),
        'c_conv_w': nrm(ks[25], (N_ODD, LRU_CONV, LRU_WIDTH), LRU_CONV ** -0.5),
        'c_conv_b': nrm(ks[26], (N_ODD, LRU_WIDTH), 0.02),
        'c_wa': nrm(ks[27], (N_ODD, 2, N_HEADS_C, LRU_BLOCK, LRU_BLOCK), LRU_BLOCK ** -0.5),
        'c_ba': nrm(ks[28], (N_ODD, 2, LRU_WIDTH), 0.02),
        'c_wx': nrm(ks[29], (N_ODD, 2, N_HEADS_C, LRU_BLOCK, LRU_BLOCK), LRU_BLOCK ** -0.5),
        'c_bx': nrm(ks[30], (N_ODD, 2, LRU_WIDTH), 0.02),
        'c_lambda': jnp.log(s) - jnp.log1p(-s),
    }


def reference(x_prompt, x_sample, cache_k, cache_v, state_lru, c, c_ctx, mod_w, mod_b, norm_g, final_norm_g,
              a_w_in, a_w_out, a_sink, hy_short_w, hy_short_b, hy_w1, hy_b1, hy_w2, hy_b2, hy_w3, hy_decay, hy_bias,
              c_w_in, c_w_out, c_conv_w, c_conv_b, c_wa, c_ba, c_wx, c_bx, c_lambda):
    xp = x_prompt
    xs = x_sample
    new_k, new_v, new_s = [], [], []
    for l in range(DEPTH):
        j = l // 2
        sh_p, sc_p, g_p = _modulation(c_ctx[None, :], mod_w[l], mod_b[l])
        sh_s, sc_s, g_s = _modulation(c, mod_w[l], mod_b[l])
        hp = _rmsnorm(xp, norm_g[l]) * (1.0 + sc_p) + sh_p
        hs = _rmsnorm(xs, norm_g[l]) * (1.0 + sc_s) + sh_s
        if l % 2 == 0:
            ev = (a_w_in[j], a_w_out[j], a_sink[j], hy_short_w[j], hy_short_b[j], hy_w1[j], hy_b1[j],
                  hy_w2[j], hy_b2[j], hy_w3[j], hy_decay[j], hy_bias[j])
            out_p, (k_ctx, v_ctx) = _even_mixer(hp, *ev)
            out_s, _ = _even_mixer(hs, *ev, ctx_k=cache_k[:, j], ctx_v=cache_v[:, j])
            new_k.append(k_ctx)
            new_v.append(v_ctx)
        else:
            od = (c_w_in[j], c_w_out[j], c_conv_w[j], c_conv_b[j], c_wa[j], c_ba[j], c_wx[j], c_bx[j], c_lambda[j])
            out_p, st = _odd_mixer(hp, *od)
            out_s, _ = _odd_mixer(hs, *od, h0=jnp.moveaxis(state_lru[:, j], 1, 0))
            new_s.append(jnp.moveaxis(st, 0, 1).astype(xp.dtype))
        xp = xp + g_p * out_p
        xs = xs + g_s * out_s
    y_prompt = _rmsnorm(xp, final_norm_g)
    y_sample = _rmsnorm(xs, final_norm_g)
    new_cache_k = jnp.stack(new_k, axis=1)
    new_cache_v = jnp.stack(new_v, axis=1)
    new_state_lru = jnp.stack(new_s, axis=1)
    return (y_prompt, y_sample, new_cache_k, new_cache_v, new_state_lru)
```

```python
import functools

import numpy as np
import jax
import jax.numpy as jnp
from jax import lax
from jax.experimental import pallas as pl
from jax.experimental.pallas import tpu as pltpu

F32 = jnp.float32
BF16 = jnp.bfloat16
HIGHEST = lax.Precision.HIGHEST

EPS = 1e-6
HEAD_DIM = 64
N_HEADS_A = 16
N_KV_A = 2
GQA_GROUP = N_HEADS_A // N_KV_A
ATTN_WIDTH = N_HEADS_A * HEAD_DIM
KV_COLS = N_KV_A * HEAD_DIM
WINDOW = 128
BLOCK = 128
GRID_W = 64
ROPE_BASE = 10000.0
HY_WIDTH = 1024
HY_ORDER = 2
HY_BANDS = 16
HY_EMB = 2 * HY_BANDS + 1
N_HEADS_C = 8
LRU_BLOCK = 256
LRU_C = 8.0
ATTN_SCALE = HEAD_DIM ** -0.5
NEG = float(np.finfo(np.float32).min)

LANES = 128
X_PITCH = LANES + 8
S_PITCH = 2 * LANES + 8

U_Q, U_GA, U_X1, U_X2, U_Z, U_GH, U_K2, U_V2 = 0, 1024, 2048, 3072, 4096, 5120, 6144, 6400
U_COLS = 6656

VMEM_LIMIT = 56 * 1024 * 1024


def _cparams(sem, vmem=VMEM_LIMIT):
    return pltpu.CompilerParams(dimension_semantics=sem, vmem_limit_bytes=vmem)


def _const_spec(shape):
    nd = len(shape)
    return pl.BlockSpec(shape, lambda *_: (0,) * nd, pipeline_mode=pl.Buffered(1))


def _mod_body(c_ref, w_ref, b_ref, o_ref):
    c = c_ref[...]
    s = c * jax.nn.sigmoid(c)
    o_ref[0] = jnp.dot(s.astype(BF16), w_ref[0].astype(BF16), preferred_element_type=F32) + b_ref[0]


def _modulation(cvec, mod_w, mod_b):
    depth, d, n3 = mod_w.shape
    rows = cvec.shape[0]
    tn = 512
    return pl.pallas_call(
        _mod_body,
        grid=(depth, n3 // tn),
        in_specs=[pl.BlockSpec((rows, d), lambda l, j: (0, 0)),
                  pl.BlockSpec((1, d, tn), lambda l, j: (l, 0, j)),
                  pl.BlockSpec((1, 1, tn), lambda l, j: (l, 0, j))],
        out_specs=pl.BlockSpec((1, rows, tn), lambda l, j: (l, 0, j)),
        out_shape=jax.ShapeDtypeStruct((depth, rows, n3), F32),
        compiler_params=_cparams(("arbitrary", "arbitrary")),
        name="modulation",
    )(cvec, mod_w, mod_b.reshape(depth, 1, n3))


def _proj_in_body(x_ref, g_ref, sc_ref, sh_ref, w_ref, o_ref, h_scr):
    @pl.when(pl.program_id(2) == 0)
    def _():
        x = x_ref[0]
        ms = jnp.mean(x * x, axis=-1, keepdims=True)
        y = x * lax.rsqrt(ms + EPS) * g_ref[...]
        h_scr[...] = (y * (1.0 + sc_ref[0]) + sh_ref[0]).astype(BF16)

    o_ref[0] = jnp.dot(h_scr[...], w_ref[...], preferred_element_type=F32).astype(o_ref.dtype)


def _proj_in(x, norm_g, scale, shift, w, tm, tn):
    b, l, d = x.shape
    n = w.shape[1]
    per_batch = scale.shape[0] == b
    mod_map = (lambda bi, i, j: (bi, 0, 0)) if per_batch else (lambda bi, i, j: (0, 0, 0))
    return pl.pallas_call(
        _proj_in_body,
        grid=(b, l // tm, n // tn),
        in_specs=[pl.BlockSpec((1, tm, d), lambda bi, i, j: (bi, i, 0)),
                  pl.BlockSpec((1, d), lambda bi, i, j: (0, 0)),
                  pl.BlockSpec((1, 1, d), mod_map),
                  pl.BlockSpec((1, 1, d), mod_map),
                  pl.BlockSpec((d, tn), lambda bi, i, j: (0, j))],
        out_specs=pl.BlockSpec((1, tm, tn), lambda bi, i, j: (bi, i, j)),
        out_shape=jax.ShapeDtypeStruct((b, l, n), BF16),
        scratch_shapes=[pltpu.VMEM((tm, d), BF16)],
        compiler_params=_cparams(("arbitrary", "arbitrary", "arbitrary")),
        name="proj_in",
    )(x, norm_g.reshape(1, d), scale, shift, w)


def _softmax_pv(s, sink, v):
    m = jnp.maximum(jnp.max(s, axis=-1, keepdims=True), sink)
    p = jnp.exp(s - m)
    denom = jnp.sum(p, axis=-1, keepdims=True) + jnp.exp(sink - m)
    o = jnp.dot(p.astype(BF16), v, preferred_element_type=F32)
    return o / denom


def _attn_heads(q, k2, v2, sink_ref, mask, o_ref):
    t = q.shape[0]
    lane = lax.broadcasted_iota(jnp.int32, (t, LANES), 1)
    left = lane < HEAD_DIM
    for j in range(ATTN_WIDTH // LANES):
        kv = (2 * j) // GQA_GROUP
        qs = q[:, LANES * j:LANES * (j + 1)]
        kk = k2[:, LANES * kv:LANES * (kv + 1)]
        vv = v2[:, LANES * kv:LANES * (kv + 1)]
        outs = []
        for half in range(2):
            keep = left if half == 0 else jnp.logical_not(left)
            qh = jnp.where(keep, qs, jnp.zeros_like(qs))
            s = lax.dot_general(qh, kk, (((1,), (1,)), ((), ())), preferred_element_type=F32) * ATTN_SCALE
            if mask is not None:
                s = jnp.where(mask, s, NEG)
            outs.append(_softmax_pv(s, sink_ref[2 * j + half], vv))
        o_ref[0, :, LANES * j:LANES * (j + 1)] = jnp.where(left, outs[0], outs[1]).astype(o_ref.dtype)


def _attn_ctx_body(sink_ref, q_ref, k_ref, v_ref, o_ref):
    _attn_heads(q_ref[0], k_ref[0], v_ref[0], sink_ref, None, o_ref)


def _attn_ctx(u, sink):
    b, t, _ = u.shape
    return pl.pallas_call(
        _attn_ctx_body,
        grid=(b,),
        in_specs=[pl.BlockSpec(memory_space=pltpu.SMEM),
                  pl.BlockSpec((1, t, ATTN_WIDTH), lambda bi: (bi, 0, U_Q // ATTN_WIDTH)),
                  pl.BlockSpec((1, t, 256), lambda bi: (bi, 0, U_K2 // 256)),
                  pl.BlockSpec((1, t, 256), lambda bi: (bi, 0, U_V2 // 256))],
        out_specs=pl.BlockSpec((1, t, ATTN_WIDTH), lambda bi: (bi, 0, 0)),
        out_shape=jax.ShapeDtypeStruct((b, t, ATTN_WIDTH), BF16),
        compiler_params=_cparams(("arbitrary",)),
        name="attn_ctx",
    )(sink, u, u, u)


def _rope_slab(x, cos, sin):
    lane = lax.broadcasted_iota(jnp.int32, x.shape, 1)
    lo = (lane & 16) == 0
    partner = jnp.where(lo, pltpu.roll(x, LANES - 16, 1), pltpu.roll(x, 16, 1))
    return x * cos + partner * sin


def _rope_k_body(k_ref, cos_ref, sin_ref, o_ref):
    cos, sin = cos_ref[...], sin_ref[...]
    for s in range(2):
        x = k_ref[0, :, LANES * s:LANES * (s + 1)].astype(F32)
        o_ref[0, :, LANES * s:LANES * (s + 1)] = _rope_slab(x, cos, sin).astype(o_ref.dtype)


def _rope_k(u, cos, sin, tl):
    b, l, _ = u.shape
    return pl.pallas_call(
        _rope_k_body,
        grid=(b, l // tl),
        in_specs=[pl.BlockSpec((1, tl, 256), lambda bi, i: (bi, i, U_K2 // 256)),
                  pl.BlockSpec((tl, LANES), lambda bi, i: (i, 0)),
                  pl.BlockSpec((tl, LANES), lambda bi, i: (i, 0))],
        out_specs=pl.BlockSpec((1, tl, 256), lambda bi, i: (bi, i, 0)),
        out_shape=jax.ShapeDtypeStruct((b, l, 256), BF16),
        compiler_params=_cparams(("arbitrary", "arbitrary")),
        name="rope_k",
    )(u, cos, sin)


def _attn_lat_body(sink_ref, q_ref, cos_ref, sin_ref, kp_ref, kc_ref, kn_ref, vp_ref, vc_ref, vn_ref,
                   kx_ref, vx_ref, o_ref, q_scr):
    i = pl.program_id(1)
    nblk = pl.num_programs(1)
    cos, sin = cos_ref[...], sin_ref[...]
    for j in range(ATTN_WIDTH // LANES):
        x = q_ref[0, :, LANES * j:LANES * (j + 1)].astype(F32)
        q_scr[:, LANES * j:LANES * (j + 1)] = _rope_slab(x, cos, sin).astype(BF16)
    k2 = jnp.concatenate([kx_ref[0, 0], kp_ref[0], kc_ref[0], kn_ref[0]], axis=0)
    v2 = jnp.concatenate([vx_ref[0, 0], vp_ref[0], vc_ref[0], vn_ref[0]], axis=0)
    tc = kx_ref.shape[2]
    r = lax.broadcasted_iota(jnp.int32, (BLOCK, tc + 3 * BLOCK), 0)
    seg = lax.broadcasted_iota(jnp.int32, (BLOCK, tc + 3 * BLOCK), 1) - tc
    dlt = r - (seg - BLOCK)
    lo = jnp.where(i > 0, 0, BLOCK)
    hi = jnp.where(i < nblk - 1, 3 * BLOCK, 2 * BLOCK)
    local = (dlt <= WINDOW) & (dlt >= -WINDOW) & (seg >= lo) & (seg < hi)
    mask = (seg < 0) | local
    _attn_heads(q_scr[...], k2, v2, sink_ref, mask, o_ref)


def _attn_lat(u, k2r, kx2, vx2, layer, sink, cos, sin):
    b, l, _ = u.shape
    nblk = l // BLOCK
    tc = kx2.shape[2]
    prev = lambda bi, i: (bi, jnp.maximum(i - 1, 0), 0)
    cur = lambda bi, i: (bi, i, 0)
    nxt = lambda bi, i: (bi, jnp.minimum(i + 1, nblk - 1), 0)
    voff = U_V2 // 256
    vprev = lambda bi, i: (bi, jnp.maximum(i - 1, 0), voff)
    vcur = lambda bi, i: (bi, i, voff)
    vnxt = lambda bi, i: (bi, jnp.minimum(i + 1, nblk - 1), voff)
    return pl.pallas_call(
        _attn_lat_body,
        grid=(b, nblk),
        in_specs=[pl.BlockSpec(memory_space=pltpu.SMEM),
                  pl.BlockSpec((1, BLOCK, ATTN_WIDTH), lambda bi, i: (bi, i, U_Q // ATTN_WIDTH)),
                  pl.BlockSpec((BLOCK, LANES), lambda bi, i: (i, 0)),
                  pl.BlockSpec((BLOCK, LANES), lambda bi, i: (i, 0)),
                  pl.BlockSpec((1, BLOCK, 256), prev),
                  pl.BlockSpec((1, BLOCK, 256), cur),
                  pl.BlockSpec((1, BLOCK, 256), nxt),
                  pl.BlockSpec((1, BLOCK, 256), vprev),
                  pl.BlockSpec((1, BLOCK, 256), vcur),
                  pl.BlockSpec((1, BLOCK, 256), vnxt),
                  pl.BlockSpec((1, 1, tc, 256), lambda bi, i: (bi, layer, 0, 0)),
                  pl.BlockSpec((1, 1, tc, 256), lambda bi, i: (bi, layer, 0, 0))],
        out_specs=pl.BlockSpec((1, BLOCK, ATTN_WIDTH), lambda bi, i: (bi, i, 0)),
        out_shape=jax.ShapeDtypeStruct((b, l, ATTN_WIDTH), BF16),
        scratch_shapes=[pltpu.VMEM((BLOCK, ATTN_WIDTH), BF16)],
        compiler_params=_cparams(("arbitrary", "arbitrary")),
        name="attn_lat",
    )(sink, u, cos, sin, k2r, k2r, k2r, u, u, u, kx2, vx2)


def _rope_tables(l):
    nf = HEAD_DIM // 4
    inv = ROPE_BASE ** (-jnp.arange(nf, dtype=F32) / nf)
    tok = jnp.arange(l)
    row = (tok // GRID_W).astype(F32)
    col = (tok % GRID_W).astype(F32)
    d = np.arange(LANES) % HEAD_DIM
    use_col = (d // (HEAD_DIM // 2)) == 1
    lo = (d % (HEAD_DIM // 2)) < nf
    pos = jnp.where(use_col[None, :], col[:, None], row[:, None])
    ang = pos * inv[d % nf][None, :]
    sign = np.where(lo, -1.0, 1.0).astype(np.float32)
    return jnp.cos(ang), jnp.sin(ang) * sign[None, :]


def _hy_hidden_body(z_ref, w1_ref, b1_ref, w2_ref, b2_ref, o_ref):
    h = jnp.sin(jnp.dot(z_ref[...], w1_ref[...], precision=HIGHEST, preferred_element_type=F32) + b1_ref[...])
    o_ref[...] = jnp.sin(jnp.dot(h, w2_ref[...], precision=HIGHEST, preferred_element_type=F32) + b2_ref[...])


def _hy_hidden(zfeat, w1, b1, w2, b2):
    rows = zfeat.shape[0]
    return pl.pallas_call(
        _hy_hidden_body,
        out_shape=jax.ShapeDtypeStruct((rows, LANES), F32),
        compiler_params=pltpu.CompilerParams(vmem_limit_bytes=VMEM_LIMIT),
        name="hy_hidden",
    )(zfeat, w1, b1, w2, b2)


def _hy_features(l):
    pos = jnp.arange(l, dtype=F32)
    t = pos / l
    freqs = jnp.linspace(1e-4, HY_BANDS - 1, HY_BANDS, dtype=F32)
    ang = 2.0 * jnp.pi * t[:, None] * freqs[None, :]
    z = jnp.concatenate([t[:, None], jnp.cos(ang), -jnp.sin(ang)], axis=-1)
    zrev = jnp.concatenate([z[:1], z[:0:-1]], axis=0)
    zz = jnp.concatenate([z, zrev], axis=0)
    return jnp.pad(zz, ((0, 0), (0, LANES - HY_EMB)))


def _hy_taps(hid_ref, w3f_ref, w3b_ref, decf_ref, decb_ref, l):
    ct = w3f_ref.shape[1]
    hf = jnp.dot(hid_ref[0:l, :], w3f_ref[...], precision=HIGHEST, preferred_element_type=F32)
    hb = jnp.dot(hid_ref[l:2 * l, :], w3b_ref[...], precision=HIGHEST, preferred_element_type=F32)
    row = lax.broadcasted_iota(jnp.int32, (l, ct), 0)
    t = row.astype(F32) / l
    trev = (l - row).astype(F32) / l
    hf = hf * jnp.exp(-t * jnp.abs(decf_ref[0]))
    hb = jnp.where(row == 0, 0.0, hb * jnp.exp(-trev * jnp.abs(decb_ref[0])))
    nsq = jnp.sum(hf * hf, axis=0, keepdims=True) + jnp.sum(hb * hb, axis=0, keepdims=True) + EPS
    sc = lax.rsqrt(nsq)
    return hf * sc, hb * sc


def _hy_filter_small_body(hid_ref, w3f_ref, w3b_ref, decf_ref, decb_ref, f_ref, o_ref, *, l):
    hf, hb = _hy_taps(hid_ref, w3f_ref, w3b_ref, decf_ref, decb_ref, l)
    taps = jnp.concatenate([hf, hb], axis=0)
    o_ref[0] = jnp.dot(f_ref[...], taps, precision=HIGHEST, preferred_element_type=F32) * (1.0 / (2 * l))


def _hy_filter_specs(ct):
    nct = HY_WIDTH // ct
    return [pl.BlockSpec((LANES, ct), lambda n, j: (0, n * nct + j)),
            pl.BlockSpec((LANES, ct), lambda n, j: (0, (HY_ORDER + n) * nct + j)),
            pl.BlockSpec((1, 1, ct), lambda n, j: (n, 0, j)),
            pl.BlockSpec((1, 1, ct), lambda n, j: (HY_ORDER + n, 0, j))]


def _hy_filter_small(hid, w3, decay, fmat, l):
    ct = 256
    return pl.pallas_call(
        functools.partial(_hy_filter_small_body, l=l),
        grid=(HY_ORDER, HY_WIDTH // ct),
        in_specs=[_const_spec((2 * l, LANES))] + _hy_filter_specs(ct) + [_const_spec(fmat.shape)],
        out_specs=pl.BlockSpec((1, 4 * l, ct), lambda n, j: (n, 0, j)),
        out_shape=jax.ShapeDtypeStruct((HY_ORDER, 4 * l, HY_WIDTH), F32),
        compiler_params=_cparams(("arbitrary", "arbitrary")),
        name="hy_filter_small",
    )(hid, w3, w3, decay, decay, fmat)


def _hy_filter_big_body(hid_ref, w3f_ref, w3b_ref, decf_ref, decb_ref, fa_ref, f2_ref, o_ref, xs, ss, *, l):
    n1 = 2 * l // LANES
    hf, hb = _hy_taps(hid_ref, w3f_ref, w3b_ref, decf_ref, decb_ref, l)
    for s in range(n1 // 2):
        xs[pl.ds(s * X_PITCH, LANES), :] = hf[s * LANES:(s + 1) * LANES]
        xs[pl.ds((n1 // 2 + s) * X_PITCH, LANES), :] = hb[s * LANES:(s + 1) * LANES]

    def stage_a(n2, carry):
        rhs = xs[pl.ds(n2, n1, stride=X_PITCH), :]
        out = jnp.dot(fa_ref[n2], rhs, precision=HIGHEST, preferred_element_type=F32)
        ss[pl.ds(n2, n1, stride=S_PITCH), :] = out[:n1]
        ss[pl.ds(LANES + n2, n1, stride=S_PITCH), :] = out[n1:]
        return carry

    lax.fori_loop(0, LANES, stage_a, 0)

    def stage_b(k1, carry):
        slab = ss[pl.ds(pl.multiple_of(k1 * S_PITCH, 8), 2 * LANES), :]
        out = jnp.dot(f2_ref[...], slab, precision=HIGHEST, preferred_element_type=F32)
        o_ref[0, pl.ds(pl.multiple_of(k1 * 2 * LANES, 2 * LANES), 2 * LANES), :] = out * (1.0 / (2 * l))
        return carry

    lax.fori_loop(0, n1, stage_b, 0)


def _hy_filter_big(hid, w3, decay, fa, f2, l):
    n1 = 2 * l // LANES
    return pl.pallas_call(
        functools.partial(_hy_filter_big_body, l=l),
        grid=(HY_ORDER, HY_WIDTH // LANES),
        in_specs=[_const_spec((2 * l, LANES))] + _hy_filter_specs(LANES)
        + [_const_spec(fa.shape), _const_spec(f2.shape)],
        out_specs=pl.BlockSpec((1, 4 * l, LANES), lambda n, j: (n, 0, j)),
        out_shape=jax.ShapeDtypeStruct((HY_ORDER, 4 * l, HY_WIDTH), F32),
        scratch_shapes=[pltpu.VMEM((n1 * X_PITCH, LANES), F32),
                        pltpu.VMEM((n1 * S_PITCH, LANES), F32)],
        compiler_params=_cparams(("arbitrary", "arbitrary")),
        name="hy_filter_big",
    )(hid, w3, w3, decay, decay, fa, f2)


def _short_conv(x, w_ref, b_ref):
    l = x.shape[0]
    row = lax.broadcasted_iota(jnp.int32, x.shape, 0)
    prev = jnp.where(row == 0, 0.0, pltpu.roll(x, 1, 0))
    nxt = jnp.where(row == l - 1, 0.0, pltpu.roll(x, l - 1, 0))
    return prev * w_ref[0:1, :] + x * w_ref[1:2, :] + nxt * w_ref[2:3, :] + b_ref[...]


def _hy_small_body(uz_ref, ux1_ref, ux2_ref, wz_ref, wx1_ref, wx2_ref, bz_ref, bx1_ref, bx2_ref,
                   h_ref, bias_ref, f_ref, g_ref, o_ref):
    l = uz_ref.shape[1]
    n = 2 * l
    sig = [_short_conv(uz_ref[b].astype(F32), wz_ref, bz_ref) for b in range(2)]
    gates = [[_short_conv(ux1_ref[b].astype(F32), wx1_ref, bx1_ref) for b in range(2)],
             [_short_conv(ux2_ref[b].astype(F32), wx2_ref, bx2_ref) for b in range(2)]]
    for o in range(HY_ORDER):
        x = jnp.concatenate(sig, axis=0).astype(BF16)
        a = jnp.dot(f_ref[...], x, preferred_element_type=F32)
        ar, ai = a[:n], a[n:]
        hr, hi = h_ref[o, 0:n, :], h_ref[o, n:2 * n, :]
        y = jnp.concatenate([ar * hr - ai * hi, ar * hi + ai * hr], axis=0).astype(BF16)
        conv = jnp.dot(g_ref[...], y, preferred_element_type=F32)
        bias = bias_ref[o:o + 1, :]
        sig = [gates[o][b] * (conv[b * l:(b + 1) * l] + bias * sig[b]) for b in range(2)]
    for b in range(2):
        o_ref[b] = sig[b].astype(o_ref.dtype)


def _hy_small(u, short_w, short_b, hspec, hy_bias, fmat, gmat):
    b, l, _ = u.shape
    ct = 256
    col = lambda off: (lambda j, p: (p, 0, off // ct + j))
    wcol = lambda off: (lambda j, p: (0, off // ct + j))
    return pl.pallas_call(
        _hy_small_body,
        grid=(HY_WIDTH // ct, b // 2),
        in_specs=[pl.BlockSpec((2, l, ct), col(U_Z)),
                  pl.BlockSpec((2, l, ct), col(U_X1)),
                  pl.BlockSpec((2, l, ct), col(U_X2)),
                  pl.BlockSpec((3, ct), wcol(2 * HY_WIDTH)),
                  pl.BlockSpec((3, ct), wcol(0)),
                  pl.BlockSpec((3, ct), wcol(HY_WIDTH)),
                  pl.BlockSpec((1, ct), wcol(2 * HY_WIDTH)),
                  pl.BlockSpec((1, ct), wcol(0)),
                  pl.BlockSpec((1, ct), wcol(HY_WIDTH)),
                  pl.BlockSpec((HY_ORDER, 4 * l, ct), lambda j, p: (0, 0, j)),
                  pl.BlockSpec((HY_ORDER, ct), lambda j, p: (0, j)),
                  _const_spec(fmat.shape),
                  _const_spec(gmat.shape)],
        out_specs=pl.BlockSpec((2, l, ct), lambda j, p: (p, 0, j)),
        out_shape=jax.ShapeDtypeStruct((b, l, HY_WIDTH), BF16),
        compiler_params=_cparams(("arbitrary", "arbitrary")),
        name="hy_small",
    )(u, u, u, short_w, short_w, short_w, short_b, short_b, short_b, hspec, hy_bias, fmat, gmat)


def _hy_big_body(*refs, conv_sig, l):
    if conv_sig:
        (sig_ref, ws_ref, bs_ref, gate_ref, wg_ref, bg_ref, h_ref, bias_ref,
         fa_ref, ga_ref, f2_ref, f2i_ref, o_ref, xs, ss, ys) = refs
    else:
        (sig_ref, gate_ref, wg_ref, bg_ref, h_ref, bias_ref,
         fa_ref, ga_ref, f2_ref, f2i_ref, o_ref, xs, ss, ys) = refs
    n1 = 2 * l // LANES
    hn = n1 // 2

    for b in range(2):
        x = sig_ref[b].astype(F32)
        if conv_sig:
            x = _short_conv(x, ws_ref, bs_ref)
        for s in range(hn):
            xs[pl.ds((b * hn + s) * X_PITCH, LANES), :] = x[s * LANES:(s + 1) * LANES]

    def stage_a(n2, carry):
        xr = xs[pl.ds(n2, hn, stride=X_PITCH), :]
        xi = xs[pl.ds(hn * X_PITCH + n2, hn, stride=X_PITCH), :]
        rhs = jnp.concatenate([xr, xi], axis=0).astype(BF16)
        out = jnp.dot(fa_ref[n2], rhs, preferred_element_type=F32)
        ss[pl.ds(n2, n1, stride=S_PITCH), :] = out[:n1]
        ss[pl.ds(LANES + n2, n1, stride=S_PITCH), :] = out[n1:]
        return carry

    lax.fori_loop(0, LANES, stage_a, 0)

    def stage_b(k1, carry):
        base = pl.multiple_of(k1 * S_PITCH, 8)
        slab = ss[pl.ds(base, 2 * LANES), :].astype(BF16)
        x = jnp.dot(f2_ref[...], slab, preferred_element_type=F32)
        hb = pl.multiple_of(k1 * 2 * LANES, 2 * LANES)
        hr = h_ref[0, pl.ds(hb, LANES), :]
        hi = h_ref[0, pl.ds(hb + LANES, LANES), :]
        xr, xi = x[:LANES], x[LANES:]
        y = jnp.concatenate([xr * hr - xi * hi, xr * hi + xi * hr], axis=0).astype(BF16)
        ss[pl.ds(base, 2 * LANES), :] = jnp.dot(f2i_ref[...], y, preferred_element_type=F32)
        return carry

    lax.fori_loop(0, n1, stage_b, 0)

    def stage_a_inv(n2, carry):
        br = ss[pl.ds(n2, n1, stride=S_PITCH), :]
        bi = ss[pl.ds(LANES + n2, n1, stride=S_PITCH), :]
        rhs = jnp.concatenate([br, bi], axis=0).astype(BF16)
        out = jnp.dot(ga_ref[n2], rhs, preferred_element_type=F32)
        ys[pl.ds(n2, hn, stride=X_PITCH), :] = out[:hn]
        ys[pl.ds(hn * X_PITCH + n2, hn, stride=X_PITCH), :] = out[hn:]
        return carry

    lax.fori_loop(0, LANES, stage_a_inv, 0)

    bias = bias_ref[...]
    for b in range(2):
        g = _short_conv(gate_ref[b].astype(F32), wg_ref, bg_ref)
        for s in range(hn):
            rows = pl.ds((b * hn + s) * X_PITCH, LANES)
            val = g[s * LANES:(s + 1) * LANES] * (ys[rows, :] + bias * xs[rows, :])
            o_ref[b, s * LANES:(s + 1) * LANES, :] = val.astype(o_ref.dtype)


def _hy_big(sig, sig_off, sig_w, gate, gate_off, gate_woff, short_w, short_b, hspec, order, hy_bias,
            tables, out_dtype):
    b, l, _ = gate.shape
    n1 = 2 * l // LANES
    fa, ga, f2, f2i = tables
    conv_sig = sig_w is not None
    col = lambda off: (lambda j, p: (p, 0, off // LANES + j))
    wcol = lambda off: (lambda j, p: (0, off // LANES + j))
    in_specs = [pl.BlockSpec((2, l, LANES), col(sig_off))]
    args = [sig]
    if conv_sig:
        in_specs += [pl.BlockSpec((3, LANES), wcol(sig_w)), pl.BlockSpec((1, LANES), wcol(sig_w))]
        args += [short_w, short_b]
    in_specs += [pl.BlockSpec((2, l, LANES), col(gate_off)),
                 pl.BlockSpec((3, LANES), wcol(gate_woff)),
                 pl.BlockSpec((1, LANES), wcol(gate_woff)),
                 pl.BlockSpec((1, 4 * l, LANES), lambda j, p: (order, 0, j), pipeline_mode=pl.Buffered(1)),
                 pl.BlockSpec((1, LANES), lambda j, p: (0, j)),
                 _const_spec(fa.shape), _const_spec(ga.shape), _const_spec(f2.shape), _const_spec(f2i.shape)]
    args += [gate, short_w, short_b, hspec, hy_bias[order][None, :], fa, ga, f2, f2i]
    return pl.pallas_call(
        functools.partial(_hy_big_body, conv_sig=conv_sig, l=l),
        grid=(HY_WIDTH // LANES, b // 2),
        in_specs=in_specs,
        out_specs=pl.BlockSpec((2, l, LANES), lambda j, p: (p, 0, j)),
        out_shape=jax.ShapeDtypeStruct((b, l, HY_WIDTH), out_dtype),
        scratch_shapes=[pltpu.VMEM((n1 * X_PITCH, LANES), F32),
                        pltpu.VMEM((n1 * S_PITCH, LANES), F32),
                        pltpu.VMEM((n1 * X_PITCH, LANES), F32)],
        compiler_params=_cparams(("arbitrary", "arbitrary")),
        name="hy_big",
    )(*args)


def _cis(idx, n):
    ang = idx.astype(F32) * (2.0 * np.pi / n)
    return jnp.cos(ang), jnp.sin(ang)


def _stack_complex(cr, ci):
    return jnp.concatenate([jnp.concatenate([cr, -ci], axis=-1), jnp.concatenate([ci, cr], axis=-1)], axis=-2)


def _dense_dft_tables(l):
    n = 2 * l
    k = jnp.arange(n, dtype=jnp.int32)
    t = jnp.arange(l, dtype=jnp.int32)
    cr, ci = _cis((k[:, None] * t[None, :]) % n, n)
    fwd = _stack_complex(cr, -ci)
    inv = _stack_complex(cr.T, ci.T)
    cfr, cfi = _cis((k[:, None] * k[None, :]) % n, n)
    filt = jnp.concatenate([cfr, -cfi], axis=0)
    return fwd.astype(BF16), inv.astype(BF16), filt


def _two_stage_dft_tables(l):
    n = 2 * l
    n1 = n // LANES
    hn = n1 // 2
    n2 = jnp.arange(LANES, dtype=jnp.int32)[:, None, None]
    k1 = jnp.arange(n1, dtype=jnp.int32)[None, :, None]
    m1 = jnp.arange(n1, dtype=jnp.int32)[None, None, :]
    cr, ci = _cis((k1 * (LANES * m1 + n2)) % n, n)
    fa = _stack_complex(cr[:, :, :hn], -ci[:, :, :hn])
    ga = _stack_complex(jnp.swapaxes(cr, 1, 2)[:, :hn, :], jnp.swapaxes(ci, 1, 2)[:, :hn, :])
    fa_filt = jnp.concatenate([cr, -ci], axis=1)
    a = jnp.arange(LANES, dtype=jnp.int32)
    c2r, c2i = _cis((a[:, None] * a[None, :]) % LANES, LANES)
    f2 = _stack_complex(c2r, -c2i)
    f2i = _stack_complex(c2r, c2i)
    return (fa.astype(BF16), ga.astype(BF16), f2.astype(BF16), f2i.astype(BF16)), fa_filt, f2


def _silu(x):
    return x * jax.nn.sigmoid(x)


def _proj_out_even_body(att_ref, ga_ref, hy_ref, gh_ref, w_ref, x_ref, g_ref, o_ref):
    a = att_ref[0].astype(F32) * _silu(ga_ref[0].astype(F32))
    h = hy_ref[0].astype(F32) * _silu(gh_ref[0].astype(F32))
    mixed = jnp.concatenate([a, h], axis=-1).astype(BF16)
    res = jnp.dot(mixed, w_ref[...], preferred_element_type=F32)
    o_ref[0] = x_ref[0] + g_ref[0] * res


def _proj_out_even(att, hy, u, w_out, x, gate, tm):
    b, l, d = x.shape
    per_batch = gate.shape[0] == b
    gmap = (lambda bi, i: (bi, 0, 0)) if per_batch else (lambda bi, i: (0, 0, 0))
    return pl.pallas_call(
        _proj_out_even_body,
        grid=(b, l // tm),
        in_specs=[pl.BlockSpec((1, tm, ATTN_WIDTH), lambda bi, i: (bi, i, 0)),
                  pl.BlockSpec((1, tm, ATTN_WIDTH), lambda bi, i: (bi, i, U_GA // ATTN_WIDTH)),
                  pl.BlockSpec((1, tm, HY_WIDTH), lambda bi, i: (bi, i, 0)),
                  pl.BlockSpec((1, tm, HY_WIDTH), lambda bi, i: (bi, i, U_GH // HY_WIDTH)),
                  _const_spec(w_out.shape),
                  pl.BlockSpec((1, tm, d), lambda bi, i: (bi, i, 0)),
                  pl.BlockSpec((1, 1, d), gmap)],
        out_specs=pl.BlockSpec((1, tm, d), lambda bi, i: (bi, i, 0)),
        out_shape=jax.ShapeDtypeStruct((b, l, d), F32),
        compiler_params=_cparams(("arbitrary", "arbitrary")),
        name="proj_out_even",
    )(att, u, hy, u, w_out, x, gate)


def _proj_out_odd_body(y_ref, gt_ref, w_ref, x_ref, g_ref, fg_ref, o_ref, *, final_norm):
    mixed = (y_ref[0].astype(F32) * _silu(gt_ref[0].astype(F32))).astype(BF16)
    res = jnp.dot(mixed, w_ref[...], preferred_element_type=F32)
    xn = x_ref[0] + g_ref[0] * res
    if final_norm:
        ms = jnp.mean(xn * xn, axis=-1, keepdims=True)
        xn = xn * lax.rsqrt(ms + EPS) * fg_ref[...]
    o_ref[0] = xn


def _proj_out_odd(y, u2, w_out, x, gate, final_g, final_norm, tm):
    b, l, d = x.shape
    per_batch = gate.shape[0] == b
    gmap = (lambda bi, i: (bi, 0, 0)) if per_batch else (lambda bi, i: (0, 0, 0))
    return pl.pallas_call(
        functools.partial(_proj_out_odd_body, final_norm=final_norm),
        grid=(b, l // tm),
        in_specs=[pl.BlockSpec((1, tm, d), lambda bi, i: (bi, i, 0)),
                  pl.BlockSpec((1, tm, d), lambda bi, i: (bi, i, 1)),
                  _const_spec(w_out.shape),
                  pl.BlockSpec((1, tm, d), lambda bi, i: (bi, i, 0)),
                  pl.BlockSpec((1, 1, d), gmap),
                  pl.BlockSpec((1, d), lambda bi, i: (0, 0))],
        out_specs=pl.BlockSpec((1, tm, d), lambda bi, i: (bi, i, 0)),
        out_shape=jax.ShapeDtypeStruct((b, l, d), F32),
        compiler_params=_cparams(("arbitrary", "arbitrary")),
        name="proj_out_odd",
    )(y, u2, w_out, x, gate, final_g.reshape(1, d))


def _rmsnorm_body(x_ref, g_ref, o_ref):
    x = x_ref[0]
    ms = jnp.mean(x * x, axis=-1, keepdims=True)
    o_ref[0] = x * lax.rsqrt(ms + EPS) * g_ref[...]


def _rmsnorm(x, g, tm):
    b, l, d = x.shape
    return pl.pallas_call(
        _rmsnorm_body,
        grid=(b, l // tm),
        in_specs=[pl.BlockSpec((1, tm, d), lambda bi, i: (bi, i, 0)),
                  pl.BlockSpec((1, d), lambda bi, i: (0, 0))],
        out_specs=pl.BlockSpec((1, tm, d), lambda bi, i: (bi, i, 0)),
        out_shape=jax.ShapeDtypeStruct((b, l, d), F32),
        compiler_params=_cparams(("arbitrary", "arbitrary")),
        name="final_rmsnorm",
    )(x, g.reshape(1, d))


def _lru_body(xb_ref, cw_ref, cb_ref, wa_ref, ba_ref, wx_ref, bx_ref, lam_ref, h0_ref,
              y_ref, fin_ref, xc_scr, a_scr, b_scr, *, tchunk):
    l = xb_ref.shape[1]
    x = xb_ref[0].astype(F32)
    row = lax.broadcasted_iota(jnp.int32, x.shape, 0)
    xm2 = jnp.where(row < 2, 0.0, pltpu.roll(x, 2, 0))
    xm1 = jnp.where(row < 1, 0.0, pltpu.roll(x, 1, 0))
    xp1 = jnp.where(row >= l - 1, 0.0, pltpu.roll(x, l - 1, 0))
    xc_scr[...] = (xm2 * cw_ref[0:1, :] + xm1 * cw_ref[1:2, :] + x * cw_ref[2:3, :]
                   + xp1 * cw_ref[3:4, :] + cb_ref[...])

    for d in range(2):
        lam = lam_ref[d:d + 1, :]
        softplus_neg = jnp.maximum(-lam, 0.0) + jnp.log1p(jnp.exp(-jnp.abs(lam)))
        for c in range(l // tchunk):
            rows = pl.ds(c * tchunk, tchunk)
            xc = xc_scr[rows, :]
            xcb = xc.astype(BF16)
            r = jax.nn.sigmoid(jnp.dot(xcb, wa_ref[d, 0], preferred_element_type=F32) + ba_ref[d:d + 1, :])
            ig = jax.nn.sigmoid(jnp.dot(xcb, wx_ref[d, 0], preferred_element_type=F32) + bx_ref[d:d + 1, :])
            log_a = -LRU_C * r * softplus_neg
            a = jnp.exp(log_a)
            one_minus_a2 = -jnp.tanh(log_a) * (a * a + 1.0)
            a_scr[d, rows, :] = a
            b_scr[d, rows, :] = jnp.sqrt(one_minus_a2) * ig * xc

    def step(i, carry):
        hf, hb = carry
        tf = i
        tb = l - 1 - i
        hf = a_scr[0, pl.ds(tf, 1), :] * hf + b_scr[0, pl.ds(tf, 1), :]
        hb = a_scr[1, pl.ds(tb, 1), :] * hb + b_scr[1, pl.ds(tb, 1), :]
        a_scr[0, pl.ds(tf, 1), :] = hf
        a_scr[1, pl.ds(tb, 1), :] = hb
        return hf, hb

    hf, hb = lax.fori_loop(0, l, step, (h0_ref[0, 0:1, :], h0_ref[0, 1:2, :]), unroll=8)
    fin_ref[0, 0:1, :] = hf
    fin_ref[0, 1:2, :] = hb
    y_ref[0] = (a_scr[0] + a_scr[1]).astype(y_ref.dtype)


def _lru(u2, conv_w, conv_b, wa, ba, wx, bx, lam, h0):
    b, l, _ = u2.shape
    w = N_HEADS_C * LRU_BLOCK
    hb = LRU_BLOCK
    tchunk = min(l, 512)
    vec = lambda rows: pl.BlockSpec((rows, hb), lambda bi, h: (0, h))
    return pl.pallas_call(
        functools.partial(_lru_body, tchunk=tchunk),
        grid=(b, N_HEADS_C),
        in_specs=[pl.BlockSpec((1, l, hb), lambda bi, h: (bi, 0, h)),
                  vec(4), vec(1),
                  pl.BlockSpec((2, 1, hb, hb), lambda bi, h: (0, h, 0, 0)), vec(2),
                  pl.BlockSpec((2, 1, hb, hb), lambda bi, h: (0, h, 0, 0)), vec(2),
                  vec(2),
                  pl.BlockSpec((1, 2, hb), lambda bi, h: (bi, 0, h))],
        out_specs=[pl.BlockSpec((1, l, hb), lambda bi, h: (bi, 0, h)),
                   pl.BlockSpec((1, 2, hb), lambda bi, h: (bi, 0, h))],
        out_shape=[jax.ShapeDtypeStruct((b, l, w), BF16),
                   jax.ShapeDtypeStruct((b, 2, w), F32)],
        scratch_shapes=[pltpu.VMEM((l, hb), F32),
                        pltpu.VMEM((2, l, hb), F32),
                        pltpu.VMEM((2, l, hb), F32)],
        compiler_params=_cparams(("arbitrary", "arbitrary")),
        name="rg_lru",
    )(u2, conv_w, conv_b.reshape(1, w), wa, ba, wx, bx, lam, h0)


def _even_w_in(w):
    q = w[:, 0:ATTN_WIDTH]
    k = w[:, ATTN_WIDTH:ATTN_WIDTH + KV_COLS]
    v = w[:, ATTN_WIDTH + KV_COLS:ATTN_WIDTH + 2 * KV_COLS]
    o = ATTN_WIDTH + 2 * KV_COLS
    ga = w[:, o:o + ATTN_WIDTH]
    hy = w[:, o + ATTN_WIDTH:o + ATTN_WIDTH + 3 * HY_WIDTH]
    gh = w[:, o + ATTN_WIDTH + 3 * HY_WIDTH:]
    dup = lambda m: jnp.concatenate([m[:, h * HEAD_DIM:(h + 1) * HEAD_DIM] for h in range(N_KV_A) for _ in range(2)],
                                    axis=1)
    return jnp.concatenate([q, ga, hy, gh, dup(k), dup(v)], axis=1).astype(BF16)


def _dup_heads(x):
    return jnp.concatenate([x[..., h, :] for h in range(N_KV_A) for _ in range(2)], axis=-1).astype(BF16)


def kernel(x_prompt, x_sample, cache_k, cache_v, state_lru, c, c_ctx, mod_w, mod_b, norm_g, final_norm_g, a_w_in, a_w_out, a_sink, hy_short_w, hy_short_b, hy_w1, hy_b1, hy_w2, hy_b2, hy_w3, hy_decay, hy_bias, c_w_in, c_w_out, c_conv_w, c_conv_b, c_wa, c_ba, c_wx, c_bx, c_lambda):
    depth, d, _ = mod_w.shape
    bp, lp, _ = x_prompt.shape
    bs, ls, _ = x_sample.shape

    cvec = jnp.concatenate([c, c_ctx[None, :], jnp.zeros((16 - bs - 1, d), F32)], axis=0)
    mod = _modulation(cvec, mod_w, mod_b)

    cos, sin = _rope_tables(ls)
    kx2 = _dup_heads(cache_k)
    vx2 = _dup_heads(cache_v)
    tabs_big, fa_filt, f2_filt = _two_stage_dft_tables(ls)
    fwd_small, inv_small, filt_small = _dense_dft_tables(lp)
    feat_p, feat_s = _hy_features(lp), _hy_features(ls)
    hid_pad = ((0, LANES - hy_w1.shape[-1]),)

    tm_s = min(ls, 1024)
    xp, xs = x_prompt, x_sample
    new_k, new_v, new_s = [], [], []
    for layer in range(depth):
        j = layer // 2
        sh_s, sc_s, g_s = (mod[layer, :bs, None, i * d:(i + 1) * d] for i in range(3))
        sh_p, sc_p, g_p = (mod[layer, bs:bs + 1, None, i * d:(i + 1) * d] for i in range(3))
        last = layer == depth - 1
        if layer % 2 == 0:
            w_in = _even_w_in(a_w_in[j])
            w_out = a_w_out[j].astype(BF16)
            up = _proj_in(xp, norm_g[layer], sc_p, sh_p, w_in, lp, 512)
            us = _proj_in(xs, norm_g[layer], sc_s, sh_s, w_in, tm_s, 512)
            new_k.append(jnp.stack([up[:, :, U_K2 + 2 * h * HEAD_DIM:U_K2 + (2 * h + 1) * HEAD_DIM]
                                    for h in range(N_KV_A)], axis=2).astype(F32))
            new_v.append(jnp.stack([up[:, :, U_V2 + 2 * h * HEAD_DIM:U_V2 + (2 * h + 1) * HEAD_DIM]
                                    for h in range(N_KV_A)], axis=2).astype(F32))
            att_p = _attn_ctx(up, a_sink[j])
            k2r = _rope_k(us, cos, sin, min(ls, 512))
            att_s = _attn_lat(us, k2r, kx2, vx2, j, a_sink[j], cos, sin)

            w1 = jnp.pad(hy_w1[j], ((0, LANES - HY_EMB),) + hid_pad)
            b1 = jnp.pad(hy_b1[j][None, :], ((0, 0),) + hid_pad)
            w2 = jnp.pad(hy_w2[j], hid_pad + hid_pad)
            b2 = jnp.pad(hy_b2[j][None, :], ((0, 0),) + hid_pad)
            w3 = jnp.pad(hy_w3[j], hid_pad + ((0, 0),))
            decay = hy_decay[j].reshape(2 * HY_ORDER, 1, HY_WIDTH)
            sw, sb = hy_short_w[j], hy_short_b[j][None, :]
            hspec_p = _hy_filter_small(_hy_hidden(feat_p, w1, b1, w2, b2), w3, decay, filt_small, lp)
            hspec_s = _hy_filter_big(_hy_hidden(feat_s, w1, b1, w2, b2), w3, decay, fa_filt, f2_filt, ls)
            hy_p = _hy_small(up, sw, sb, hspec_p, hy_bias[j], fwd_small, inv_small)
            z1 = _hy_big(us, U_Z, 2 * HY_WIDTH, us, U_X1, 0, sw, sb, hspec_s, 0, hy_bias[j], tabs_big, F32)
            hy_s = _hy_big(z1, 0, None, us, U_X2, HY_WIDTH, sw, sb, hspec_s, 1, hy_bias[j], tabs_big, BF16)

            xp = _proj_out_even(att_p, hy_p, up, w_out, xp, g_p, lp)
            xs = _proj_out_even(att_s, hy_s, us, w_out, xs, g_s, min(ls, 512))
            if last:
                xp = _rmsnorm(xp, final_norm_g, lp)
                xs = _rmsnorm(xs, final_norm_g, min(ls, 512))
        else:
            w_in = c_w_in[j].astype(BF16)
            w_out = c_w_out[j].astype(BF16)
            wa = c_wa[j].astype(BF16)
            wx = c_wx[j].astype(BF16)
            up = _proj_in(xp, norm_g[layer], sc_p, sh_p, w_in, lp, 512)
            us = _proj_in(xs, norm_g[layer], sc_s, sh_s, w_in, tm_s, 512)
            lru_args = (c_conv_w[j], c_conv_b[j], wa, c_ba[j], wx, c_bx[j], c_lambda[j])
            y_p, fin_p = _lru(up, *lru_args, jnp.zeros((bp, 2, d), F32))
            y_s, _ = _lru(us, *lru_args, state_lru[:, j])
            new_s.append(fin_p)
            xp = _proj_out_odd(y_p, up, w_out, xp, g_p, final_norm_g, last, lp)
            xs = _proj_out_odd(y_s, us, w_out, xs, g_s, final_norm_g, last, min(ls, 512))

    return (xp, xs, jnp.stack(new_k, axis=1), jnp.stack(new_v, axis=1), jnp.stack(new_s, axis=1))
```

```python
import functools

import numpy as np
import jax
import jax.numpy as jnp
from jax import lax
from jax.experimental import pallas as pl
from jax.experimental.pallas import tpu as pltpu

F32 = jnp.float32
BF16 = jnp.bfloat16
HIGHEST = lax.Precision.HIGHEST

EPS = 1e-6
HEAD_DIM = 64
N_HEADS_A = 16
N_KV_A = 2
GQA_GROUP = N_HEADS_A // N_KV_A
ATTN_WIDTH = N_HEADS_A * HEAD_DIM
KV_COLS = N_KV_A * HEAD_DIM
WINDOW = 128
BLOCK = 128
GRID_W = 64
ROPE_BASE = 10000.0
HY_WIDTH = 1024
HY_ORDER = 2
HY_BANDS = 16
HY_EMB = 2 * HY_BANDS + 1
N_HEADS_C = 8
LRU_BLOCK = 256
LRU_C = 8.0
ATTN_SCALE = HEAD_DIM ** -0.5
NEG = float(np.finfo(np.float32).min)

LANES = 128
X_PITCH = LANES + 8
S_PITCH = 2 * LANES + 8

U_Q, U_GA, U_X1, U_X2, U_Z, U_GH, U_K2, U_V2 = 0, 1024, 2048, 3072, 4096, 5120, 6144, 6400
U_COLS = 6656

VMEM_LIMIT = 56 * 1024 * 1024


def _cparams(sem, vmem=VMEM_LIMIT):
    return pltpu.CompilerParams(dimension_semantics=sem, vmem_limit_bytes=vmem)


def _const_spec(shape):
    nd = len(shape)
    return pl.BlockSpec(shape, lambda *_: (0,) * nd, pipeline_mode=pl.Buffered(1))


def _mod_body(c_ref, w_ref, b_ref, o_ref):
    c = c_ref[...]
    s = c * jax.nn.sigmoid(c)
    o_ref[0] = jnp.dot(s.astype(BF16), w_ref[0].astype(BF16), preferred_element_type=F32) + b_ref[0]


def _modulation(cvec, mod_w, mod_b):
    depth, d, n3 = mod_w.shape
    rows = cvec.shape[0]
    tn = 512
    return pl.pallas_call(
        _mod_body,
        grid=(depth, n3 // tn),
        in_specs=[pl.BlockSpec((rows, d), lambda l, j: (0, 0)),
                  pl.BlockSpec((1, d, tn), lambda l, j: (l, 0, j)),
                  pl.BlockSpec((1, 1, tn), lambda l, j: (l, 0, j))],
        out_specs=pl.BlockSpec((1, rows, tn), lambda l, j: (l, 0, j)),
        out_shape=jax.ShapeDtypeStruct((depth, rows, n3), F32),
        compiler_params=_cparams(("arbitrary", "arbitrary")),
        name="modulation",
    )(cvec, mod_w, mod_b.reshape(depth, 1, n3))


def _proj_in_body(x_ref, g_ref, sc_ref, sh_ref, w_ref, o_ref, h_scr):
    @pl.when(pl.program_id(2) == 0)
    def _():
        x = x_ref[0]
        ms = jnp.mean(x * x, axis=-1, keepdims=True)
        y = x * lax.rsqrt(ms + EPS) * g_ref[...]
        h_scr[...] = (y * (1.0 + sc_ref[0]) + sh_ref[0]).astype(BF16)

    o_ref[0] = jnp.dot(h_scr[...], w_ref[...], preferred_element_type=F32).astype(o_ref.dtype)


def _proj_in(x, norm_g, scale, shift, w, tm, tn):
    b, l, d = x.shape
    n = w.shape[1]
    per_batch = scale.shape[0] == b
    mod_map = (lambda bi, i, j: (bi, 0, 0)) if per_batch else (lambda bi, i, j: (0, 0, 0))
    return pl.pallas_call(
        _proj_in_body,
        grid=(b, l // tm, n // tn),
        in_specs=[pl.BlockSpec((1, tm, d), lambda bi, i, j: (bi, i, 0)),
                  pl.BlockSpec((1, d), lambda bi, i, j: (0, 0)),
                  pl.BlockSpec((1, 1, d), mod_map),
                  pl.BlockSpec((1, 1, d), mod_map),
                  pl.BlockSpec((d, tn), lambda bi, i, j: (0, j))],
        out_specs=pl.BlockSpec((1, tm, tn), lambda bi, i, j: (bi, i, j)),
        out_shape=jax.ShapeDtypeStruct((b, l, n), BF16),
        scratch_shapes=[pltpu.VMEM((tm, d), BF16)],
        compiler_params=_cparams(("arbitrary", "arbitrary", "arbitrary")),
        name="proj_in",
    )(x, norm_g.reshape(1, d), scale, shift, w)


def _attn_heads(q, k2, v2, sink_ref, mask_t, o_ref):
    t = q.shape[0]
    left = lax.broadcasted_iota(jnp.int32, (t, LANES), 1) < HEAD_DIM
    top = lax.broadcasted_iota(jnp.int32, (LANES, t), 0) < HEAD_DIM
    zero = jnp.zeros((t, LANES), q.dtype)
    slabs_per_kv = GQA_GROUP // 2
    for kv in range(N_KV_A):
        kk = k2[:, LANES * kv:LANES * (kv + 1)]
        vv = v2[:, LANES * kv:LANES * (kv + 1)]
        blocks = []
        for jj in range(slabs_per_kv):
            j = slabs_per_kv * kv + jj
            qs = q[:, LANES * j:LANES * (j + 1)]
            blocks += [jnp.where(left, qs, zero), jnp.where(left, zero, qs)]
        qstack = jnp.concatenate(blocks, axis=0)
        s_t = lax.dot_general(kk, qstack, (((1,), (1,)), ((), ())), preferred_element_type=F32)
        ps, dens = [], []
        for bi in range(GQA_GROUP):
            s = s_t[:, bi * t:(bi + 1) * t]
            if mask_t is not None:
                s = jnp.where(mask_t, s, NEG)
            sink = sink_ref[GQA_GROUP * kv + bi]
            m = jnp.maximum(jnp.max(s, axis=0, keepdims=True), sink)
            p = jnp.exp(s - m)
            dens.append(jnp.sum(p, axis=0, keepdims=True) + jnp.exp(sink - m))
            ps.append(p.astype(BF16))
        p_t = jnp.concatenate(ps, axis=1)
        o_t = lax.dot_general(vv, p_t, (((0,), (0,)), ((), ())), preferred_element_type=F32)
        o_t = o_t / jnp.concatenate(dens, axis=1)
        for jj in range(slabs_per_kv):
            j = slabs_per_kv * kv + jj
            comb = jnp.where(top, o_t[:, (2 * jj) * t:(2 * jj + 1) * t], o_t[:, (2 * jj + 1) * t:(2 * jj + 2) * t])
            o_ref[0, :, LANES * j:LANES * (j + 1)] = comb.T.astype(o_ref.dtype)


def _attn_ctx_body(sink_ref, q_ref, k_ref, v_ref, o_ref):
    _attn_heads(q_ref[0] * ATTN_SCALE, k_ref[0], v_ref[0], sink_ref, None, o_ref)


def _attn_ctx(u, sink):
    b, t, _ = u.shape
    return pl.pallas_call(
        _attn_ctx_body,
        grid=(b,),
        in_specs=[pl.BlockSpec(memory_space=pltpu.SMEM),
                  pl.BlockSpec((1, t, ATTN_WIDTH), lambda bi: (bi, 0, U_Q // ATTN_WIDTH)),
                  pl.BlockSpec((1, t, 256), lambda bi: (bi, 0, U_K2 // 256)),
                  pl.BlockSpec((1, t, 256), lambda bi: (bi, 0, U_V2 // 256))],
        out_specs=pl.BlockSpec((1, t, ATTN_WIDTH), lambda bi: (bi, 0, 0)),
        out_shape=jax.ShapeDtypeStruct((b, t, ATTN_WIDTH), BF16),
        compiler_params=_cparams(("arbitrary",)),
        name="attn_ctx",
    )(sink, u, u, u)


def _rope_slab(x, cos, sin):
    lane = lax.broadcasted_iota(jnp.int32, x.shape, 1)
    lo = (lane & 16) == 0
    partner = jnp.where(lo, pltpu.roll(x, LANES - 16, 1), pltpu.roll(x, 16, 1))
    return x * cos + partner * sin


def _rope_k_body(k_ref, cos_ref, sin_ref, o_ref):
    cos, sin = cos_ref[...], sin_ref[...]
    for s in range(2):
        x = k_ref[0, :, LANES * s:LANES * (s + 1)].astype(F32)
        o_ref[0, :, LANES * s:LANES * (s + 1)] = _rope_slab(x, cos, sin).astype(o_ref.dtype)


def _rope_k(u, cos, sin, tl):
    b, l, _ = u.shape
    return pl.pallas_call(
        _rope_k_body,
        grid=(b, l // tl),
        in_specs=[pl.BlockSpec((1, tl, 256), lambda bi, i: (bi, i, U_K2 // 256)),
                  pl.BlockSpec((tl, LANES), lambda bi, i: (i, 0)),
                  pl.BlockSpec((tl, LANES), lambda bi, i: (i, 0))],
        out_specs=pl.BlockSpec((1, tl, 256), lambda bi, i: (bi, i, 0)),
        out_shape=jax.ShapeDtypeStruct((b, l, 256), BF16),
        compiler_params=_cparams(("arbitrary", "arbitrary")),
        name="rope_k",
    )(u, cos, sin)


def _attn_lat_body(sink_ref, q_ref, cos_ref, sin_ref, kp_ref, kc_ref, kn_ref, vp_ref, vc_ref, vn_ref,
                   kx_ref, vx_ref, o_ref, q_scr):
    i = pl.program_id(1)
    nblk = pl.num_programs(1)
    cos, sin = cos_ref[...], sin_ref[...]
    for j in range(ATTN_WIDTH // LANES):
        x = q_ref[0, :, LANES * j:LANES * (j + 1)].astype(F32)
        q_scr[:, LANES * j:LANES * (j + 1)] = (_rope_slab(x, cos, sin) * ATTN_SCALE).astype(BF16)
    k2 = jnp.concatenate([kx_ref[0, 0], kp_ref[0], kc_ref[0], kn_ref[0]], axis=0)
    v2 = jnp.concatenate([vx_ref[0, 0], vp_ref[0], vc_ref[0], vn_ref[0]], axis=0)
    tc = kx_ref.shape[2]
    seg = lax.broadcasted_iota(jnp.int32, (tc + 3 * BLOCK, BLOCK), 0) - tc
    r = lax.broadcasted_iota(jnp.int32, (tc + 3 * BLOCK, BLOCK), 1)
    dlt = r - (seg - BLOCK)
    lo = jnp.where(i > 0, 0, BLOCK)
    hi = jnp.where(i < nblk - 1, 3 * BLOCK, 2 * BLOCK)
    local = (dlt <= WINDOW) & (dlt >= -WINDOW) & (seg >= lo) & (seg < hi)
    mask_t = (seg < 0) | local
    _attn_heads(q_scr[...], k2, v2, sink_ref, mask_t, o_ref)


def _attn_lat(u, k2r, kx2, vx2, layer, sink, cos, sin):
    b, l, _ = u.shape
    nblk = l // BLOCK
    tc = kx2.shape[2]
    prev = lambda bi, i: (bi, jnp.maximum(i - 1, 0), 0)
    cur = lambda bi, i: (bi, i, 0)
    nxt = lambda bi, i: (bi, jnp.minimum(i + 1, nblk - 1), 0)
    voff = U_V2 // 256
    vprev = lambda bi, i: (bi, jnp.maximum(i - 1, 0), voff)
    vcur = lambda bi, i: (bi, i, voff)
    vnxt = lambda bi, i: (bi, jnp.minimum(i + 1, nblk - 1), voff)
    return pl.pallas_call(
        _attn_lat_body,
        grid=(b, nblk),
        in_specs=[pl.BlockSpec(memory_space=pltpu.SMEM),
                  pl.BlockSpec((1, BLOCK, ATTN_WIDTH), lambda bi, i: (bi, i, U_Q // ATTN_WIDTH)),
                  pl.BlockSpec((BLOCK, LANES), lambda bi, i: (i, 0)),
                  pl.BlockSpec((BLOCK, LANES), lambda bi, i: (i, 0)),
                  pl.BlockSpec((1, BLOCK, 256), prev),
                  pl.BlockSpec((1, BLOCK, 256), cur),
                  pl.BlockSpec((1, BLOCK, 256), nxt),
                  pl.BlockSpec((1, BLOCK, 256), vprev),
                  pl.BlockSpec((1, BLOCK, 256), vcur),
                  pl.BlockSpec((1, BLOCK, 256), vnxt),
                  pl.BlockSpec((1, 1, tc, 256), lambda bi, i: (bi, layer, 0, 0)),
                  pl.BlockSpec((1, 1, tc, 256), lambda bi, i: (bi, layer, 0, 0))],
        out_specs=pl.BlockSpec((1, BLOCK, ATTN_WIDTH), lambda bi, i: (bi, i, 0)),
        out_shape=jax.ShapeDtypeStruct((b, l, ATTN_WIDTH), BF16),
        scratch_shapes=[pltpu.VMEM((BLOCK, ATTN_WIDTH), BF16)],
        compiler_params=_cparams(("arbitrary", "arbitrary")),
        name="attn_lat",
    )(sink, u, cos, sin, k2r, k2r, k2r, u, u, u, kx2, vx2)


def _rope_tables(l):
    nf = HEAD_DIM // 4
    inv = ROPE_BASE ** (-jnp.arange(nf, dtype=F32) / nf)
    tok = jnp.arange(l)
    row = (tok // GRID_W).astype(F32)
    col = (tok % GRID_W).astype(F32)
    d = np.arange(LANES) % HEAD_DIM
    use_col = (d // (HEAD_DIM // 2)) == 1
    lo = (d % (HEAD_DIM // 2)) < nf
    pos = jnp.where(use_col[None, :], col[:, None], row[:, None])
    ang = pos * inv[d % nf][None, :]
    sign = np.where(lo, -1.0, 1.0).astype(np.float32)
    return jnp.cos(ang), jnp.sin(ang) * sign[None, :]


def _dot_split(a, b):
    ah = a.astype(BF16)
    bh = b.astype(BF16)
    al = (a - ah.astype(F32)).astype(BF16)
    bl = (b - bh.astype(F32)).astype(BF16)
    dot = functools.partial(jnp.dot, preferred_element_type=F32)
    return dot(ah, bh) + (dot(ah, bl) + dot(al, bh))


def _hy_hidden_body(z_ref, w1_ref, b1_ref, w2_ref, b2_ref, o_ref):
    h = jnp.sin(jnp.dot(z_ref[...], w1_ref[...], precision=HIGHEST, preferred_element_type=F32) + b1_ref[...])
    o_ref[...] = jnp.sin(jnp.dot(h, w2_ref[...], precision=HIGHEST, preferred_element_type=F32) + b2_ref[...])


def _hy_hidden(zfeat, w1, b1, w2, b2):
    rows = zfeat.shape[0]
    return pl.pallas_call(
        _hy_hidden_body,
        out_shape=jax.ShapeDtypeStruct((rows, LANES), F32),
        compiler_params=pltpu.CompilerParams(vmem_limit_bytes=VMEM_LIMIT),
        name="hy_hidden",
    )(zfeat, w1, b1, w2, b2)


def _hy_features(l):
    pos = jnp.arange(l, dtype=F32)
    t = pos / l
    freqs = jnp.linspace(1e-4, HY_BANDS - 1, HY_BANDS, dtype=F32)
    ang = 2.0 * jnp.pi * t[:, None] * freqs[None, :]
    z = jnp.concatenate([t[:, None], jnp.cos(ang), -jnp.sin(ang)], axis=-1)
    zrev = jnp.concatenate([z[:1], z[:0:-1]], axis=0)
    zz = jnp.concatenate([z, zrev], axis=0)
    return jnp.pad(zz, ((0, 0), (0, LANES - HY_EMB)))


def _hy_taps(hid_ref, w3f_ref, w3b_ref, decf_ref, decb_ref, l):
    ct = w3f_ref.shape[1]
    hf = _dot_split(hid_ref[0:l, :], w3f_ref[...])
    hb = _dot_split(hid_ref[l:2 * l, :], w3b_ref[...])
    row = lax.broadcasted_iota(jnp.int32, (l, ct), 0)
    t = row.astype(F32) / l
    trev = (l - row).astype(F32) / l
    hf = hf * jnp.exp(-t * jnp.abs(decf_ref[0]))
    hb = jnp.where(row == 0, 0.0, hb * jnp.exp(-trev * jnp.abs(decb_ref[0])))
    nsq = jnp.sum(hf * hf, axis=0, keepdims=True) + jnp.sum(hb * hb, axis=0, keepdims=True) + EPS
    sc = lax.rsqrt(nsq)
    return hf * sc, hb * sc


def _hy_filter_small_body(hid_ref, w3f_ref, w3b_ref, decf_ref, decb_ref, f_ref, o_ref, *, l):
    hf, hb = _hy_taps(hid_ref, w3f_ref, w3b_ref, decf_ref, decb_ref, l)
    taps = jnp.concatenate([hf, hb], axis=0)
    o_ref[0] = jnp.dot(f_ref[...], taps, precision=HIGHEST, preferred_element_type=F32) * (1.0 / (2 * l))


def _hy_filter_specs(ct):
    nct = HY_WIDTH // ct
    return [pl.BlockSpec((LANES, ct), lambda n, j: (0, n * nct + j)),
            pl.BlockSpec((LANES, ct), lambda n, j: (0, (HY_ORDER + n) * nct + j)),
            pl.BlockSpec((1, 1, ct), lambda n, j: (n, 0, j)),
            pl.BlockSpec((1, 1, ct), lambda n, j: (HY_ORDER + n, 0, j))]


def _hy_filter_small(hid, w3, decay, fmat, l):
    ct = 256
    return pl.pallas_call(
        functools.partial(_hy_filter_small_body, l=l),
        grid=(HY_ORDER, HY_WIDTH // ct),
        in_specs=[_const_spec((2 * l, LANES))] + _hy_filter_specs(ct) + [_const_spec(fmat.shape)],
        out_specs=pl.BlockSpec((1, 4 * l, ct), lambda n, j: (n, 0, j)),
        out_shape=jax.ShapeDtypeStruct((HY_ORDER, 4 * l, HY_WIDTH), F32),
        compiler_params=_cparams(("arbitrary", "arbitrary")),
        name="hy_filter_small",
    )(hid, w3, w3, decay, decay, fmat)


def _hy_filter_big_body(hid_ref, w3f_ref, w3b_ref, decf_ref, decb_ref, fa_ref, f2_ref, o_ref, xs, ss, *, l):
    n1 = 2 * l // LANES
    hf, hb = _hy_taps(hid_ref, w3f_ref, w3b_ref, decf_ref, decb_ref, l)
    for s in range(n1 // 2):
        xs[pl.ds(s * X_PITCH, LANES), :] = hf[s * LANES:(s + 1) * LANES]
        xs[pl.ds((n1 // 2 + s) * X_PITCH, LANES), :] = hb[s * LANES:(s + 1) * LANES]

    def stage_a(n2, carry):
        rhs = xs[pl.ds(n2, n1, stride=X_PITCH), :]
        out = _dot_split(fa_ref[n2], rhs)
        ss[pl.ds(n2, n1, stride=S_PITCH), :] = out[:n1]
        ss[pl.ds(LANES + n2, n1, stride=S_PITCH), :] = out[n1:]
        return carry

    lax.fori_loop(0, LANES, stage_a, 0, unroll=8)

    def stage_b(kp, carry):
        slabs = [ss[pl.ds(pl.multiple_of((2 * kp + e) * S_PITCH, 8), 2 * LANES), :] for e in range(2)]
        out = _dot_split(f2_ref[...], jnp.concatenate(slabs, axis=1)) * (1.0 / (2 * l))
        for e in range(2):
            rows = pl.ds(pl.multiple_of((2 * kp + e) * 2 * LANES, 2 * LANES), 2 * LANES)
            o_ref[0, rows, :] = out[:, e * LANES:(e + 1) * LANES]
        return carry

    lax.fori_loop(0, n1 // 2, stage_b, 0, unroll=2)


def _hy_filter_big(hid, w3, decay, fa, f2, l):
    n1 = 2 * l // LANES
    return pl.pallas_call(
        functools.partial(_hy_filter_big_body, l=l),
        grid=(HY_ORDER, HY_WIDTH // LANES),
        in_specs=[_const_spec((2 * l, LANES))] + _hy_filter_specs(LANES)
        + [_const_spec(fa.shape), _const_spec(f2.shape)],
        out_specs=pl.BlockSpec((1, 4 * l, LANES), lambda n, j: (n, 0, j)),
        out_shape=jax.ShapeDtypeStruct((HY_ORDER, 4 * l, HY_WIDTH), F32),
        scratch_shapes=[pltpu.VMEM((n1 * X_PITCH, LANES), F32),
                        pltpu.VMEM((n1 * S_PITCH, LANES), F32)],
        compiler_params=_cparams(("arbitrary", "arbitrary")),
        name="hy_filter_big",
    )(hid, w3, w3, decay, decay, fa, f2)


def _short_conv(x, w_ref, b_ref):
    l = x.shape[0]
    row = lax.broadcasted_iota(jnp.int32, x.shape, 0)
    prev = jnp.where(row == 0, 0.0, pltpu.roll(x, 1, 0))
    nxt = jnp.where(row == l - 1, 0.0, pltpu.roll(x, l - 1, 0))
    return prev * w_ref[0:1, :] + x * w_ref[1:2, :] + nxt * w_ref[2:3, :] + b_ref[...]


def _hy_small_body(uz_ref, ux1_ref, ux2_ref, wz_ref, wx1_ref, wx2_ref, bz_ref, bx1_ref, bx2_ref,
                   h_ref, bias_ref, f_ref, g_ref, o_ref):
    l = uz_ref.shape[1]
    n = 2 * l
    sig = [_short_conv(uz_ref[b].astype(F32), wz_ref, bz_ref) for b in range(2)]
    gates = [[_short_conv(ux1_ref[b].astype(F32), wx1_ref, bx1_ref) for b in range(2)],
             [_short_conv(ux2_ref[b].astype(F32), wx2_ref, bx2_ref) for b in range(2)]]
    for o in range(HY_ORDER):
        x = jnp.concatenate(sig, axis=0).astype(BF16)
        a = jnp.dot(f_ref[...], x, preferred_element_type=F32)
        ar, ai = a[:n], a[n:]
        hr, hi = h_ref[o, 0:n, :], h_ref[o, n:2 * n, :]
        y = jnp.concatenate([ar * hr - ai * hi, ar * hi + ai * hr], axis=0).astype(BF16)
        conv = jnp.dot(g_ref[...], y, preferred_element_type=F32)
        bias = bias_ref[o:o + 1, :]
        sig = [gates[o][b] * (conv[b * l:(b + 1) * l] + bias * sig[b]) for b in range(2)]
    for b in range(2):
        o_ref[b] = sig[b].astype(o_ref.dtype)


def _hy_small(u, short_w, short_b, hspec, hy_bias, fmat, gmat):
    b, l, _ = u.shape
    ct = 256
    col = lambda off: (lambda j, p: (p, 0, off // ct + j))
    wcol = lambda off: (lambda j, p: (0, off // ct + j))
    return pl.pallas_call(
        _hy_small_body,
        grid=(HY_WIDTH // ct, b // 2),
        in_specs=[pl.BlockSpec((2, l, ct), col(U_Z)),
                  pl.BlockSpec((2, l, ct), col(U_X1)),
                  pl.BlockSpec((2, l, ct), col(U_X2)),
                  pl.BlockSpec((3, ct), wcol(2 * HY_WIDTH)),
                  pl.BlockSpec((3, ct), wcol(0)),
                  pl.BlockSpec((3, ct), wcol(HY_WIDTH)),
                  pl.BlockSpec((1, ct), wcol(2 * HY_WIDTH)),
                  pl.BlockSpec((1, ct), wcol(0)),
                  pl.BlockSpec((1, ct), wcol(HY_WIDTH)),
                  pl.BlockSpec((HY_ORDER, 4 * l, ct), lambda j, p: (0, 0, j)),
                  pl.BlockSpec((HY_ORDER, ct), lambda j, p: (0, j)),
                  _const_spec(fmat.shape),
                  _const_spec(gmat.shape)],
        out_specs=pl.BlockSpec((2, l, ct), lambda j, p: (p, 0, j)),
        out_shape=jax.ShapeDtypeStruct((b, l, HY_WIDTH), BF16),
        compiler_params=_cparams(("arbitrary", "arbitrary")),
        name="hy_small",
    )(u, u, u, short_w, short_w, short_w, short_b, short_b, short_b, hspec, hy_bias, fmat, gmat)


def _hy_big_body(*refs, conv_sig, l):
    if conv_sig:
        (sig_ref, ws_ref, bs_ref, gate_ref, wg_ref, bg_ref, h_ref, bias_ref,
         fa_ref, ga_ref, f2_ref, f2i_ref, o_ref, xs, ss, ys) = refs
    else:
        (sig_ref, gate_ref, wg_ref, bg_ref, h_ref, bias_ref,
         fa_ref, ga_ref, f2_ref, f2i_ref, o_ref, xs, ss, ys) = refs
    n1 = 2 * l // LANES
    hn = n1 // 2

    for b in range(2):
        x = sig_ref[b].astype(F32)
        if conv_sig:
            x = _short_conv(x, ws_ref, bs_ref)
        for s in range(hn):
            xs[pl.ds((b * hn + s) * X_PITCH, LANES), :] = x[s * LANES:(s + 1) * LANES]

    def stage_a(n2, carry):
        xr = xs[pl.ds(n2, hn, stride=X_PITCH), :]
        xi = xs[pl.ds(hn * X_PITCH + n2, hn, stride=X_PITCH), :]
        rhs = jnp.concatenate([xr, xi], axis=0).astype(BF16)
        out = jnp.dot(fa_ref[n2], rhs, preferred_element_type=F32)
        ss[pl.ds(n2, n1, stride=S_PITCH), :] = out[:n1]
        ss[pl.ds(LANES + n2, n1, stride=S_PITCH), :] = out[n1:]
        return carry

    lax.fori_loop(0, LANES, stage_a, 0, unroll=8)

    def stage_b(kp, carry):
        bases = [pl.multiple_of((2 * kp + e) * S_PITCH, 8) for e in range(2)]
        slab = jnp.concatenate([ss[pl.ds(bs, 2 * LANES), :] for bs in bases], axis=1).astype(BF16)
        x = jnp.dot(f2_ref[...], slab, preferred_element_type=F32)
        hbs = [pl.multiple_of((2 * kp + e) * 2 * LANES, 2 * LANES) for e in range(2)]
        hr = jnp.concatenate([h_ref[0, pl.ds(hb, LANES), :] for hb in hbs], axis=1)
        hi = jnp.concatenate([h_ref[0, pl.ds(hb + LANES, LANES), :] for hb in hbs], axis=1)
        xr, xi = x[:LANES], x[LANES:]
        y = jnp.concatenate([xr * hr - xi * hi, xr * hi + xi * hr], axis=0).astype(BF16)
        back = jnp.dot(f2i_ref[...], y, preferred_element_type=F32)
        for e in range(2):
            ss[pl.ds(bases[e], 2 * LANES), :] = back[:, e * LANES:(e + 1) * LANES]
        return carry

    lax.fori_loop(0, n1 // 2, stage_b, 0, unroll=2)

    def stage_a_inv(n2, carry):
        br = ss[pl.ds(n2, n1, stride=S_PITCH), :]
        bi = ss[pl.ds(LANES + n2, n1, stride=S_PITCH), :]
        rhs = jnp.concatenate([br, bi], axis=0).astype(BF16)
        out = jnp.dot(ga_ref[n2], rhs, preferred_element_type=F32)
        ys[pl.ds(n2, hn, stride=X_PITCH), :] = out[:hn]
        ys[pl.ds(hn * X_PITCH + n2, hn, stride=X_PITCH), :] = out[hn:]
        return carry

    lax.fori_loop(0, LANES, stage_a_inv, 0, unroll=8)

    bias = bias_ref[...]
    for b in range(2):
        g = _short_conv(gate_ref[b].astype(F32), wg_ref, bg_ref)
        for s in range(hn):
            rows = pl.ds((b * hn + s) * X_PITCH, LANES)
            val = g[s * LANES:(s + 1) * LANES] * (ys[rows, :] + bias * xs[rows, :])
            o_ref[b, s * LANES:(s + 1) * LANES, :] = val.astype(o_ref.dtype)


def _hy_big(sig, sig_off, sig_w, gate, gate_off, gate_woff, short_w, short_b, hspec, order, hy_bias,
            tables, out_dtype):
    b, l, _ = gate.shape
    n1 = 2 * l // LANES
    fa, ga, f2, f2i = tables
    conv_sig = sig_w is not None
    col = lambda off: (lambda j, p: (p, 0, off // LANES + j))
    wcol = lambda off: (lambda j, p: (0, off // LANES + j))
    in_specs = [pl.BlockSpec((2, l, LANES), col(sig_off))]
    args = [sig]
    if conv_sig:
        in_specs += [pl.BlockSpec((3, LANES), wcol(sig_w)), pl.BlockSpec((1, LANES), wcol(sig_w))]
        args += [short_w, short_b]
    in_specs += [pl.BlockSpec((2, l, LANES), col(gate_off)),
                 pl.BlockSpec((3, LANES), wcol(gate_woff)),
                 pl.BlockSpec((1, LANES), wcol(gate_woff)),
                 pl.BlockSpec((1, 4 * l, LANES), lambda j, p: (order, 0, j), pipeline_mode=pl.Buffered(1)),
                 pl.BlockSpec((1, LANES), lambda j, p: (0, j)),
                 _const_spec(fa.shape), _const_spec(ga.shape), _const_spec(f2.shape), _const_spec(f2i.shape)]
    args += [gate, short_w, short_b, hspec, hy_bias[order][None, :], fa, ga, f2, f2i]
    return pl.pallas_call(
        functools.partial(_hy_big_body, conv_sig=conv_sig, l=l),
        grid=(HY_WIDTH // LANES, b // 2),
        in_specs=in_specs,
        out_specs=pl.BlockSpec((2, l, LANES), lambda j, p: (p, 0, j)),
        out_shape=jax.ShapeDtypeStruct((b, l, HY_WIDTH), out_dtype),
        scratch_shapes=[pltpu.VMEM((n1 * X_PITCH, LANES), F32),
                        pltpu.VMEM((n1 * S_PITCH, LANES), F32),
                        pltpu.VMEM((n1 * X_PITCH, LANES), F32)],
        compiler_params=_cparams(("arbitrary", "arbitrary")),
        name="hy_big",
    )(*args)


def _cis(idx, n):
    ang = idx.astype(F32) * (2.0 * np.pi / n)
    return jnp.cos(ang), jnp.sin(ang)


def _stack_complex(cr, ci):
    return jnp.concatenate([jnp.concatenate([cr, -ci], axis=-1), jnp.concatenate([ci, cr], axis=-1)], axis=-2)


def _dense_dft_tables(l):
    n = 2 * l
    k = jnp.arange(n, dtype=jnp.int32)
    t = jnp.arange(l, dtype=jnp.int32)
    cr, ci = _cis((k[:, None] * t[None, :]) % n, n)
    fwd = _stack_complex(cr, -ci)
    inv = _stack_complex(cr.T, ci.T)
    cfr, cfi = _cis((k[:, None] * k[None, :]) % n, n)
    filt = jnp.concatenate([cfr, -cfi], axis=0)
    return fwd.astype(BF16), inv.astype(BF16), filt


def _two_stage_dft_tables(l):
    n = 2 * l
    n1 = n // LANES
    hn = n1 // 2
    n2 = jnp.arange(LANES, dtype=jnp.int32)[:, None, None]
    k1 = jnp.arange(n1, dtype=jnp.int32)[None, :, None]
    m1 = jnp.arange(n1, dtype=jnp.int32)[None, None, :]
    cr, ci = _cis((k1 * (LANES * m1 + n2)) % n, n)
    fa = _stack_complex(cr[:, :, :hn], -ci[:, :, :hn])
    ga = _stack_complex(jnp.swapaxes(cr, 1, 2)[:, :hn, :], jnp.swapaxes(ci, 1, 2)[:, :hn, :])
    fa_filt = jnp.concatenate([cr, -ci], axis=1)
    a = jnp.arange(LANES, dtype=jnp.int32)
    c2r, c2i = _cis((a[:, None] * a[None, :]) % LANES, LANES)
    f2 = _stack_complex(c2r, -c2i)
    f2i = _stack_complex(c2r, c2i)
    return (fa.astype(BF16), ga.astype(BF16), f2.astype(BF16), f2i.astype(BF16)), fa_filt, f2


def _silu(x):
    return x * jax.nn.sigmoid(x)


def _proj_out_even_body(att_ref, ga_ref, hy_ref, gh_ref, w_ref, x_ref, g_ref, o_ref):
    a = att_ref[0].astype(F32) * _silu(ga_ref[0].astype(F32))
    h = hy_ref[0].astype(F32) * _silu(gh_ref[0].astype(F32))
    mixed = jnp.concatenate([a, h], axis=-1).astype(BF16)
    res = jnp.dot(mixed, w_ref[...], preferred_element_type=F32)
    o_ref[0] = x_ref[0] + g_ref[0] * res


def _proj_out_even(att, hy, u, w_out, x, gate, tm):
    b, l, d = x.shape
    per_batch = gate.shape[0] == b
    gmap = (lambda bi, i: (bi, 0, 0)) if per_batch else (lambda bi, i: (0, 0, 0))
    return pl.pallas_call(
        _proj_out_even_body,
        grid=(b, l // tm),
        in_specs=[pl.BlockSpec((1, tm, ATTN_WIDTH), lambda bi, i: (bi, i, 0)),
                  pl.BlockSpec((1, tm, ATTN_WIDTH), lambda bi, i: (bi, i, U_GA // ATTN_WIDTH)),
                  pl.BlockSpec((1, tm, HY_WIDTH), lambda bi, i: (bi, i, 0)),
                  pl.BlockSpec((1, tm, HY_WIDTH), lambda bi, i: (bi, i, U_GH // HY_WIDTH)),
                  _const_spec(w_out.shape),
                  pl.BlockSpec((1, tm, d), lambda bi, i: (bi, i, 0)),
                  pl.BlockSpec((1, 1, d), gmap)],
        out_specs=pl.BlockSpec((1, tm, d), lambda bi, i: (bi, i, 0)),
        out_shape=jax.ShapeDtypeStruct((b, l, d), F32),
        compiler_params=_cparams(("arbitrary", "arbitrary")),
        name="proj_out_even",
    )(att, u, hy, u, w_out, x, gate)


def _proj_out_odd_body(yf_ref, yb_ref, gt_ref, w_ref, x_ref, g_ref, fg_ref, o_ref, *, final_norm):
    y = yf_ref[0].astype(F32) + yb_ref[0].astype(F32)
    mixed = (y * _silu(gt_ref[0].astype(F32))).astype(BF16)
    res = jnp.dot(mixed, w_ref[...], preferred_element_type=F32)
    xn = x_ref[0] + g_ref[0] * res
    if final_norm:
        ms = jnp.mean(xn * xn, axis=-1, keepdims=True)
        xn = xn * lax.rsqrt(ms + EPS) * fg_ref[...]
    o_ref[0] = xn


def _proj_out_odd(yf, yb, u2, w_out, x, gate, final_g, final_norm, tm):
    b, l, d = x.shape
    per_batch = gate.shape[0] == b
    gmap = (lambda bi, i: (bi, 0, 0)) if per_batch else (lambda bi, i: (0, 0, 0))
    return pl.pallas_call(
        functools.partial(_proj_out_odd_body, final_norm=final_norm),
        grid=(b, l // tm),
        in_specs=[pl.BlockSpec((1, tm, d), lambda bi, i: (bi, i, 0)),
                  pl.BlockSpec((1, tm, d), lambda bi, i: (bi, i, 0)),
                  pl.BlockSpec((1, tm, d), lambda bi, i: (bi, i, 1)),
                  _const_spec(w_out.shape),
                  pl.BlockSpec((1, tm, d), lambda bi, i: (bi, i, 0)),
                  pl.BlockSpec((1, 1, d), gmap),
                  pl.BlockSpec((1, d), lambda bi, i: (0, 0))],
        out_specs=pl.BlockSpec((1, tm, d), lambda bi, i: (bi, i, 0)),
        out_shape=jax.ShapeDtypeStruct((b, l, d), F32),
        compiler_params=_cparams(("arbitrary", "arbitrary")),
        name="proj_out_odd",
    )(yf, yb, u2, w_out, x, gate, final_g.reshape(1, d))


def _rmsnorm_body(x_ref, g_ref, o_ref):
    x = x_ref[0]
    ms = jnp.mean(x * x, axis=-1, keepdims=True)
    o_ref[0] = x * lax.rsqrt(ms + EPS) * g_ref[...]


def _rmsnorm(x, g, tm):
    b, l, d = x.shape
    return pl.pallas_call(
        _rmsnorm_body,
        grid=(b, l // tm),
        in_specs=[pl.BlockSpec((1, tm, d), lambda bi, i: (bi, i, 0)),
                  pl.BlockSpec((1, d), lambda bi, i: (0, 0))],
        out_specs=pl.BlockSpec((1, tm, d), lambda bi, i: (bi, i, 0)),
        out_shape=jax.ShapeDtypeStruct((b, l, d), F32),
        compiler_params=_cparams(("arbitrary", "arbitrary")),
        name="final_rmsnorm",
    )(x, g.reshape(1, d))


LRU_GROUP = 8


LRU_HALO = 2 * LRU_GROUP


def _sigmoid(z):
    return 0.5 * jnp.tanh(0.5 * z) + 0.5


def _lru_gates(d, x_ref, xp_ref, xn_ref, keep_prev, keep_next, cw_ref, cb_ref, wa_ref, ba_ref, wx_ref, bx_ref,
               lam_ref, x_scr, a_scr, b_scr):
    tt = x_ref.shape[1]
    g = LRU_GROUP
    nk = LRU_BLOCK // LANES
    for s in range(g):
        for k in range(nk):
            lanes = slice(k * LANES, (k + 1) * LANES)
            x_scr[d, k, pl.ds(LRU_HALO + s, tt, stride=g), :] = x_ref[s, :, lanes].astype(F32)
            x_scr[d, k, pl.ds(s, 1), :] = xp_ref[s, 6:7, lanes].astype(F32) * keep_prev
            x_scr[d, k, pl.ds(g + s, 1), :] = xp_ref[s, 7:8, lanes].astype(F32) * keep_prev
            x_scr[d, k, pl.ds(LRU_HALO + tt * g + s, 1), :] = xn_ref[s, 0:1, lanes].astype(F32) * keep_next
    xcs = []
    for k in range(nk):
        lanes = slice(k * LANES, (k + 1) * LANES)
        acc = cb_ref[:, lanes]
        for j in range(4):
            acc = acc + x_scr[d, k, pl.ds(j * g, tt * g), :] * cw_ref[j:j + 1, lanes]
        xcs.append(acc)
    xc = jnp.concatenate(xcs, axis=1)
    xcb = xc.astype(BF16)
    lam = lam_ref[d:d + 1, :]
    softplus_neg = jnp.maximum(-lam, 0.0) + jnp.log1p(jnp.exp(-jnp.abs(lam)))
    r = _sigmoid(jnp.dot(xcb, wa_ref[d, 0], preferred_element_type=F32) + ba_ref[d:d + 1, :])
    ig = _sigmoid(jnp.dot(xcb, wx_ref[d, 0], preferred_element_type=F32) + bx_ref[d:d + 1, :])
    log_a = r * (-LRU_C * softplus_neg)
    a = jnp.exp(log_a)
    b = jnp.sqrt(-jnp.tanh(log_a) * (a * a + 1.0)) * ig * xc
    for k in range(nk):
        a_scr[d, k] = a[:, k * LANES:(k + 1) * LANES]
        b_scr[d, k] = b[:, k * LANES:(k + 1) * LANES]


def _lru_body(xf_ref, xfp_ref, xfn_ref, xb_ref, xbp_ref, xbn_ref, cw_ref, cb_ref, wa_ref, ba_ref, wx_ref, bx_ref,
              lam_ref, h0_ref, yf_ref, yb_ref, fin_ref, x_scr, a_scr, b_scr, y_scr, h_scr):
    i = pl.program_id(2)
    nt = pl.num_programs(2)
    tt = xf_ref.shape[1]
    nk = LRU_BLOCK // LANES

    @pl.when(i == 0)
    def _():
        h_scr[...] = h0_ref[...]

    first = jnp.where(i == 0, 0.0, 1.0)
    last = jnp.where(i == nt - 1, 0.0, 1.0)
    params = (cw_ref, cb_ref, wa_ref, ba_ref, wx_ref, bx_ref, lam_ref, x_scr, a_scr, b_scr)
    _lru_gates(0, xf_ref, xfp_ref, xfn_ref, first, last, *params)
    _lru_gates(1, xb_ref, xbp_ref, xbn_ref, last, first, *params)

    def step(t, carry):
        rf = pl.ds(pl.multiple_of(t * LRU_GROUP, LRU_GROUP), LRU_GROUP)
        rb = pl.ds(pl.multiple_of((tt - 1 - t) * LRU_GROUP, LRU_GROUP), LRU_GROUP)
        out = []
        for d, rows in ((0, rf), (1, rb)):
            for k in range(nk):
                h = a_scr[d, k, rows, :] * carry[d * nk + k] + b_scr[d, k, rows, :]
                y_scr[d, k, rows, :] = h
                out.append(h)
        return tuple(out)

    init = tuple(h_scr[d, :, k * LANES:(k + 1) * LANES] for d in range(2) for k in range(nk))
    fin = lax.fori_loop(0, tt, step, init, unroll=8)
    for d in range(2):
        for k in range(nk):
            h_scr[d, :, k * LANES:(k + 1) * LANES] = fin[d * nk + k]

    for s in range(LRU_GROUP):
        for k in range(nk):
            rows = pl.ds(s, tt, stride=LRU_GROUP)
            yf_ref[s, :, k * LANES:(k + 1) * LANES] = y_scr[0, k, rows, :].astype(yf_ref.dtype)
            yb_ref[s, :, k * LANES:(k + 1) * LANES] = y_scr[1, k, rows, :].astype(yb_ref.dtype)

    @pl.when(i == nt - 1)
    def _():
        fin_ref[...] = h_scr[...]


def _lru(u2, conv_w, conv_b, wa, ba, wx, bx, lam, h0):
    b, l, _ = u2.shape
    w = N_HEADS_C * LRU_BLOCK
    hb = LRU_BLOCK
    g = LRU_GROUP
    assert b % g == 0, "the scan packs LRU_GROUP sequences per vreg"
    tt = min(l, 256)
    nt = l // tt
    hr = tt // 8
    nhalo = l // 8
    vec = lambda rows: pl.BlockSpec((rows, hb), lambda h, gi, i: (0, h))
    wspec = pl.BlockSpec((2, 1, hb, hb), lambda h, gi, i: (0, h, 0, 0))
    tile = lambda f: pl.BlockSpec((g, tt, hb), lambda h, gi, i: (gi, f(i), h))
    halo_prev = lambda f: pl.BlockSpec((g, 8, hb), lambda h, gi, i: (gi, jnp.maximum(f(i) * hr - 1, 0), h))
    halo_next = lambda f: pl.BlockSpec((g, 8, hb), lambda h, gi, i: (gi, jnp.minimum((f(i) + 1) * hr, nhalo - 1), h))
    fwd = lambda i: i
    bwd = lambda i: nt - 1 - i
    state = pl.BlockSpec((2, g, hb), lambda h, gi, i: (0, gi, h))
    return pl.pallas_call(
        _lru_body,
        grid=(N_HEADS_C, b // g, nt),
        in_specs=[tile(fwd), halo_prev(fwd), halo_next(fwd), tile(bwd), halo_prev(bwd), halo_next(bwd),
                  vec(4), vec(1), wspec, vec(2), wspec, vec(2), vec(2), state],
        out_specs=[tile(fwd), tile(bwd), state],
        out_shape=[jax.ShapeDtypeStruct((b, l, w), BF16),
                   jax.ShapeDtypeStruct((b, l, w), BF16),
                   jax.ShapeDtypeStruct((2, b, w), F32)],
        scratch_shapes=[pltpu.VMEM((2, hb // LANES, (tt + 3) * g, LANES), F32),
                        pltpu.VMEM((2, hb // LANES, tt * g, LANES), F32),
                        pltpu.VMEM((2, hb // LANES, tt * g, LANES), F32),
                        pltpu.VMEM((2, hb // LANES, tt * g, LANES), F32),
                        pltpu.VMEM((2, g, hb), F32)],
        compiler_params=_cparams(("arbitrary", "arbitrary", "arbitrary")),
        name="rg_lru",
    )(u2, u2, u2, u2, u2, u2, conv_w, conv_b.reshape(1, w), wa, ba, wx, bx, lam, h0)


def _even_w_in(w):
    q = w[:, 0:ATTN_WIDTH]
    k = w[:, ATTN_WIDTH:ATTN_WIDTH + KV_COLS]
    v = w[:, ATTN_WIDTH + KV_COLS:ATTN_WIDTH + 2 * KV_COLS]
    o = ATTN_WIDTH + 2 * KV_COLS
    ga = w[:, o:o + ATTN_WIDTH]
    hy = w[:, o + ATTN_WIDTH:o + ATTN_WIDTH + 3 * HY_WIDTH]
    gh = w[:, o + ATTN_WIDTH + 3 * HY_WIDTH:]
    dup = lambda m: jnp.concatenate([m[:, h * HEAD_DIM:(h + 1) * HEAD_DIM] for h in range(N_KV_A) for _ in range(2)],
                                    axis=1)
    return jnp.concatenate([q, ga, hy, gh, dup(k), dup(v)], axis=1).astype(BF16)


def _dup_heads(x):
    return jnp.concatenate([x[..., h, :] for h in range(N_KV_A) for _ in range(2)], axis=-1).astype(BF16)


def kernel(x_prompt, x_sample, cache_k, cache_v, state_lru, c, c_ctx, mod_w, mod_b, norm_g, final_norm_g, a_w_in, a_w_out, a_sink, hy_short_w, hy_short_b, hy_w1, hy_b1, hy_w2, hy_b2, hy_w3, hy_decay, hy_bias, c_w_in, c_w_out, c_conv_w, c_conv_b, c_wa, c_ba, c_wx, c_bx, c_lambda):
    depth, d, _ = mod_w.shape
    bp, lp, _ = x_prompt.shape
    bs, ls, _ = x_sample.shape

    cvec = jnp.concatenate([c, c_ctx[None, :], jnp.zeros((16 - bs - 1, d), F32)], axis=0)
    mod = _modulation(cvec, mod_w, mod_b)

    cos, sin = _rope_tables(ls)
    kx2 = _dup_heads(cache_k)
    vx2 = _dup_heads(cache_v)
    tabs_big, fa_filt, f2_filt = _two_stage_dft_tables(ls)
    fwd_small, inv_small, filt_small = _dense_dft_tables(lp)
    feat_p, feat_s = _hy_features(lp), _hy_features(ls)
    hid_pad = ((0, LANES - hy_w1.shape[-1]),)

    tm_s = min(ls, 1024)
    xp, xs = x_prompt, x_sample
    new_k, new_v, new_s = [], [], []
    for layer in range(depth):
        j = layer // 2
        sh_s, sc_s, g_s = (mod[layer, :bs, None, i * d:(i + 1) * d] for i in range(3))
        sh_p, sc_p, g_p = (mod[layer, bs:bs + 1, None, i * d:(i + 1) * d] for i in range(3))
        last = layer == depth - 1
        if layer % 2 == 0:
            w_in = _even_w_in(a_w_in[j])
            w_out = a_w_out[j].astype(BF16)
            up = _proj_in(xp, norm_g[layer], sc_p, sh_p, w_in, lp, 512)
            us = _proj_in(xs, norm_g[layer], sc_s, sh_s, w_in, tm_s, 512)
            new_k.append(jnp.stack([up[:, :, U_K2 + 2 * h * HEAD_DIM:U_K2 + (2 * h + 1) * HEAD_DIM]
                                    for h in range(N_KV_A)], axis=2).astype(F32))
            new_v.append(jnp.stack([up[:, :, U_V2 + 2 * h * HEAD_DIM:U_V2 + (2 * h + 1) * HEAD_DIM]
                                    for h in range(N_KV_A)], axis=2).astype(F32))
            att_p = _attn_ctx(up, a_sink[j])
            k2r = _rope_k(us, cos, sin, min(ls, 512))
            att_s = _attn_lat(us, k2r, kx2, vx2, j, a_sink[j], cos, sin)

            w1 = jnp.pad(hy_w1[j], ((0, LANES - HY_EMB),) + hid_pad)
            b1 = jnp.pad(hy_b1[j][None, :], ((0, 0),) + hid_pad)
            w2 = jnp.pad(hy_w2[j], hid_pad + hid_pad)
            b2 = jnp.pad(hy_b2[j][None, :], ((0, 0),) + hid_pad)
            w3 = jnp.pad(hy_w3[j], hid_pad + ((0, 0),))
            decay = hy_decay[j].reshape(2 * HY_ORDER, 1, HY_WIDTH)
            sw, sb = hy_short_w[j], hy_short_b[j][None, :]
            hspec_p = _hy_filter_small(_hy_hidden(feat_p, w1, b1, w2, b2), w3, decay, filt_small, lp)
            hspec_s = _hy_filter_big(_hy_hidden(feat_s, w1, b1, w2, b2), w3, decay, fa_filt, f2_filt, ls)
            hy_p = _hy_small(up, sw, sb, hspec_p, hy_bias[j], fwd_small, inv_small)
            z1 = _hy_big(us, U_Z, 2 * HY_WIDTH, us, U_X1, 0, sw, sb, hspec_s, 0, hy_bias[j], tabs_big, F32)
            hy_s = _hy_big(z1, 0, None, us, U_X2, HY_WIDTH, sw, sb, hspec_s, 1, hy_bias[j], tabs_big, BF16)

            xp = _proj_out_even(att_p, hy_p, up, w_out, xp, g_p, lp)
            xs = _proj_out_even(att_s, hy_s, us, w_out, xs, g_s, min(ls, 512))
            if last:
                xp = _rmsnorm(xp, final_norm_g, lp)
                xs = _rmsnorm(xs, final_norm_g, min(ls, 512))
        else:
            w_in = c_w_in[j].astype(BF16)
            w_out = c_w_out[j].astype(BF16)
            wa = c_wa[j].astype(BF16)
            wx = c_wx[j].astype(BF16)
            up = _proj_in(xp, norm_g[layer], sc_p, sh_p, w_in, lp, 512)
            us = _proj_in(xs, norm_g[layer], sc_s, sh_s, w_in, tm_s, 512)
            lru_args = (c_conv_w[j], c_conv_b[j], wa, c_ba[j], wx, c_bx[j], c_lambda[j])
            yf_p, yb_p, fin_p = _lru(up, *lru_args, jnp.zeros((2, bp, d), F32))
            yf_s, yb_s, _ = _lru(us, *lru_args, jnp.moveaxis(state_lru[:, j], 1, 0))
            new_s.append(jnp.moveaxis(fin_p, 0, 1))
            xp = _proj_out_odd(yf_p, yb_p, up, w_out, xp, g_p, final_norm_g, last, lp)
            xs = _proj_out_odd(yf_s, yb_s, us, w_out, xs, g_s, final_norm_g, last, min(ls, 512))

    return (xp, xs, jnp.stack(new_k, axis=1), jnp.stack(new_v, axis=1), jnp.stack(new_s, axis=1))
```

```python
import functools

import numpy as np
import jax
import jax.numpy as jnp
from jax import lax
from jax.experimental import pallas as pl
from jax.experimental.pallas import tpu as pltpu

F32 = jnp.float32
BF16 = jnp.bfloat16
HIGHEST = lax.Precision.HIGHEST

EPS = 1e-6
HEAD_DIM = 64
N_HEADS_A = 16
N_KV_A = 2
GQA_GROUP = N_HEADS_A // N_KV_A
ATTN_WIDTH = N_HEADS_A * HEAD_DIM
KV_COLS = N_KV_A * HEAD_DIM
WINDOW = 128
BLOCK = 128
GRID_W = 64
ROPE_BASE = 10000.0
HY_WIDTH = 1024
HY_ORDER = 2
HY_BANDS = 16
HY_EMB = 2 * HY_BANDS + 1
N_HEADS_C = 8
LRU_BLOCK = 256
LRU_C = 8.0
ATTN_SCALE = HEAD_DIM ** -0.5
NEG = float(np.finfo(np.float32).min)

LANES = 128
X_PITCH = LANES + 8
S_PITCH = 2 * LANES + 8

U_Q, U_GA, U_X1, U_X2, U_Z, U_GH, U_K2, U_V2 = 0, 1024, 2048, 3072, 4096, 5120, 6144, 6400
U_COLS = 6656

VMEM_LIMIT = 56 * 1024 * 1024


def _cparams(sem, vmem=VMEM_LIMIT):
    return pltpu.CompilerParams(dimension_semantics=sem, vmem_limit_bytes=vmem)


def _const_spec(shape):
    nd = len(shape)
    return pl.BlockSpec(shape, lambda *_: (0,) * nd, pipeline_mode=pl.Buffered(1))


def _mod_body(c_ref, w_ref, b_ref, o_ref):
    c = c_ref[...]
    s = c * jax.nn.sigmoid(c)
    o_ref[0] = jnp.dot(s.astype(BF16), w_ref[0].astype(BF16), preferred_element_type=F32) + b_ref[0]


def _modulation(cvec, mod_w, mod_b):
    depth, d, n3 = mod_w.shape
    rows = cvec.shape[0]
    tn = 512
    return pl.pallas_call(
        _mod_body,
        grid=(depth, n3 // tn),
        in_specs=[pl.BlockSpec((rows, d), lambda l, j: (0, 0)),
                  pl.BlockSpec((1, d, tn), lambda l, j: (l, 0, j)),
                  pl.BlockSpec((1, 1, tn), lambda l, j: (l, 0, j))],
        out_specs=pl.BlockSpec((1, rows, tn), lambda l, j: (l, 0, j)),
        out_shape=jax.ShapeDtypeStruct((depth, rows, n3), F32),
        compiler_params=_cparams(("arbitrary", "arbitrary")),
        name="modulation",
    )(cvec, mod_w, mod_b.reshape(depth, 1, n3))


NORM_ROWS = 16
DOT_COLS = 512


def _col_chunks(tn):
    edges = list(range(0, tn, DOT_COLS)) + [tn]
    if len(edges) > 2 and edges[-1] - edges[-2] < 256:
        del edges[-2]
    return list(zip(edges[:-1], edges[1:]))


def _proj_in_body(x_ref, g_ref, sc_ref, sh_ref, w_ref, o_ref, h_scr):
    @pl.when(pl.program_id(2) == 0)
    def _():
        gain = g_ref[...]
        scale1 = 1.0 + sc_ref[0]
        shift = sh_ref[0]

        def rows(r, carry):
            sl = pl.ds(pl.multiple_of(r * NORM_ROWS, NORM_ROWS), NORM_ROWS)
            x = x_ref[0, sl, :]
            ms = jnp.mean(x * x, axis=-1, keepdims=True)
            y = x * lax.rsqrt(ms + EPS) * gain
            h_scr[sl, :] = (y * scale1 + shift).astype(BF16)
            return carry

        lax.fori_loop(0, x_ref.shape[1] // NORM_ROWS, rows, 0, unroll=8)

    for c0, c1 in _col_chunks(w_ref.shape[1]):
        o_ref[0, :, c0:c1] = jnp.dot(h_scr[...], w_ref[:, c0:c1], preferred_element_type=F32).astype(o_ref.dtype)


def _proj_in(x, norm_g, scale, shift, w, tm, n_col_tiles):
    b, l, d = x.shape
    n = w.shape[1]
    tn = n // n_col_tiles
    assert tn * n_col_tiles == n and tn % LANES == 0
    per_batch = scale.shape[0] == b
    mod_map = (lambda bi, i, j: (bi, 0, 0)) if per_batch else (lambda bi, i, j: (0, 0, 0))
    return pl.pallas_call(
        _proj_in_body,
        grid=(b, l // tm, n // tn),
        in_specs=[pl.BlockSpec((1, tm, d), lambda bi, i, j: (bi, i, 0)),
                  pl.BlockSpec((1, d), lambda bi, i, j: (0, 0)),
                  pl.BlockSpec((1, 1, d), mod_map),
                  pl.BlockSpec((1, 1, d), mod_map),
                  pl.BlockSpec((d, tn), lambda bi, i, j: (0, j))],
        out_specs=pl.BlockSpec((1, tm, tn), lambda bi, i, j: (bi, i, j)),
        out_shape=jax.ShapeDtypeStruct((b, l, n), BF16),
        scratch_shapes=[pltpu.VMEM((tm, d), BF16)],
        compiler_params=_cparams(("arbitrary", "arbitrary", "arbitrary")),
        name="proj_in",
    )(x, norm_g.reshape(1, d), scale, shift, w)


def _proj_in_flat(x, norm_g, scale, shift, w, n_col_tiles):
    b, l, d = x.shape
    assert scale.shape[0] == 1
    tm = min(b * l, 1024)
    out = _proj_in(x.reshape(1, b * l, d), norm_g, scale, shift, w, tm, n_col_tiles)
    return out.reshape(b, l, w.shape[1])


def _attn_heads(q, k2, v2, sink_ref, masks, o_ref):
    t = q.shape[0]
    left = lax.broadcasted_iota(jnp.int32, (t, LANES), 1) < HEAD_DIM
    top = lax.broadcasted_iota(jnp.int32, (LANES, t), 0) < HEAD_DIM
    zero = jnp.zeros((t, LANES), q.dtype)
    slabs_per_kv = GQA_GROUP // 2
    for kv in range(N_KV_A):
        kk = k2[:, LANES * kv:LANES * (kv + 1)]
        vv = v2[:, LANES * kv:LANES * (kv + 1)]
        blocks = []
        for jj in range(slabs_per_kv):
            j = slabs_per_kv * kv + jj
            qs = q[:, LANES * j:LANES * (j + 1)]
            blocks += [jnp.where(left, qs, zero), jnp.where(left, zero, qs)]
        qstack = jnp.concatenate(blocks, axis=0)
        s_t = lax.dot_general(kk, qstack, (((1,), (1,)), ((), ())), preferred_element_type=F32)
        ps, dens = [], []
        for bi in range(GQA_GROUP):
            s = s_t[:, bi * t:(bi + 1) * t]
            pieces, pos = [], 0
            for r0, r1, mk in masks:
                if r0 > pos:
                    pieces.append(s[pos:r0])
                pieces.append(jnp.where(mk, s[r0:r1], NEG))
                pos = r1
            if pieces:
                if pos < s.shape[0]:
                    pieces.append(s[pos:])
                s = jnp.concatenate(pieces, axis=0)
            sink = sink_ref[GQA_GROUP * kv + bi]
            m = jnp.maximum(jnp.max(s, axis=0, keepdims=True), sink)
            p = jnp.exp(s - m)
            dens.append(jnp.sum(p, axis=0, keepdims=True) + jnp.exp(sink - m))
            ps.append(p.astype(BF16))
        p_t = jnp.concatenate(ps, axis=1)
        o_t = lax.dot_general(vv, p_t, (((0,), (0,)), ((), ())), preferred_element_type=F32)
        o_t = o_t / jnp.concatenate(dens, axis=1)
        for jj in range(slabs_per_kv):
            j = slabs_per_kv * kv + jj
            comb = jnp.where(top, o_t[:, (2 * jj) * t:(2 * jj + 1) * t], o_t[:, (2 * jj + 1) * t:(2 * jj + 2) * t])
            o_ref[0, :, LANES * j:LANES * (j + 1)] = comb.T.astype(o_ref.dtype)


def _attn_ctx_body(sink_ref, q_ref, k_ref, v_ref, o_ref):
    _attn_heads(q_ref[0] * ATTN_SCALE, k_ref[0], v_ref[0], sink_ref, (), o_ref)


def _attn_ctx(u, sink):
    b, t, _ = u.shape
    return pl.pallas_call(
        _attn_ctx_body,
        grid=(b,),
        in_specs=[pl.BlockSpec(memory_space=pltpu.SMEM),
                  pl.BlockSpec((1, t, ATTN_WIDTH), lambda bi: (bi, 0, U_Q // ATTN_WIDTH)),
                  pl.BlockSpec((1, t, 256), lambda bi: (bi, 0, U_K2 // 256)),
                  pl.BlockSpec((1, t, 256), lambda bi: (bi, 0, U_V2 // 256))],
        out_specs=pl.BlockSpec((1, t, ATTN_WIDTH), lambda bi: (bi, 0, 0)),
        out_shape=jax.ShapeDtypeStruct((b, t, ATTN_WIDTH), BF16),
        compiler_params=_cparams(("arbitrary",)),
        name="attn_ctx",
    )(sink, u, u, u)


def _rope_slab(x, cos, sin):
    lane = lax.broadcasted_iota(jnp.int32, x.shape, 1)
    lo = (lane & 16) == 0
    partner = jnp.where(lo, pltpu.roll(x, LANES - 16, 1), pltpu.roll(x, 16, 1))
    return x * cos + partner * sin


def _rope_k_body(k_ref, cos_ref, sin_ref, o_ref):
    cos, sin = cos_ref[...], sin_ref[...]
    for s in range(2):
        x = k_ref[0, :, LANES * s:LANES * (s + 1)].astype(F32)
        o_ref[0, :, LANES * s:LANES * (s + 1)] = _rope_slab(x, cos, sin).astype(o_ref.dtype)


def _rope_k(u, cos, sin, tl):
    b, l, _ = u.shape
    return pl.pallas_call(
        _rope_k_body,
        grid=(b, l // tl),
        in_specs=[pl.BlockSpec((1, tl, 256), lambda bi, i: (bi, i, U_K2 // 256)),
                  pl.BlockSpec((tl, LANES), lambda bi, i: (i, 0)),
                  pl.BlockSpec((tl, LANES), lambda bi, i: (i, 0))],
        out_specs=pl.BlockSpec((1, tl, 256), lambda bi, i: (bi, i, 0)),
        out_shape=jax.ShapeDtypeStruct((b, l, 256), BF16),
        compiler_params=_cparams(("arbitrary", "arbitrary")),
        name="rope_k",
    )(u, cos, sin)


def _attn_lat_body(sink_ref, q_ref, cos_ref, sin_ref, kp_ref, kc_ref, kn_ref, vp_ref, vc_ref, vn_ref,
                   kx_ref, vx_ref, o_ref, q_scr):
    i = pl.program_id(1)
    nblk = pl.num_programs(1)
    cos, sin = cos_ref[...], sin_ref[...]
    for j in range(ATTN_WIDTH // LANES):
        x = q_ref[0, :, LANES * j:LANES * (j + 1)].astype(F32)
        q_scr[:, LANES * j:LANES * (j + 1)] = (_rope_slab(x, cos, sin) * ATTN_SCALE).astype(BF16)
    k2 = jnp.concatenate([kx_ref[0, 0], kp_ref[0], kc_ref[0], kn_ref[0]], axis=0)
    v2 = jnp.concatenate([vx_ref[0, 0], vp_ref[0], vc_ref[0], vn_ref[0]], axis=0)
    tc = kx_ref.shape[2]
    diff = (lax.broadcasted_iota(jnp.int32, (BLOCK, BLOCK), 0)
            - lax.broadcasted_iota(jnp.int32, (BLOCK, BLOCK), 1))
    prev_ok = diff >= jnp.where(i > 0, 0, 2 * BLOCK)
    next_ok = diff <= jnp.where(i < nblk - 1, 0, -2 * BLOCK)
    masks = ((tc, tc + BLOCK, prev_ok), (tc + 2 * BLOCK, tc + 3 * BLOCK, next_ok))
    _attn_heads(q_scr[...], k2, v2, sink_ref, masks, o_ref)


def _attn_lat(u, k2r, kx2, vx2, layer, sink, cos, sin):
    b, l, _ = u.shape
    nblk = l // BLOCK
    tc = kx2.shape[2]
    prev = lambda bi, i: (bi, jnp.maximum(i - 1, 0), 0)
    cur = lambda bi, i: (bi, i, 0)
    nxt = lambda bi, i: (bi, jnp.minimum(i + 1, nblk - 1), 0)
    voff = U_V2 // 256
    vprev = lambda bi, i: (bi, jnp.maximum(i - 1, 0), voff)
    vcur = lambda bi, i: (bi, i, voff)
    vnxt = lambda bi, i: (bi, jnp.minimum(i + 1, nblk - 1), voff)
    return pl.pallas_call(
        _attn_lat_body,
        grid=(b, nblk),
        in_specs=[pl.BlockSpec(memory_space=pltpu.SMEM),
                  pl.BlockSpec((1, BLOCK, ATTN_WIDTH), lambda bi, i: (bi, i, U_Q // ATTN_WIDTH)),
                  pl.BlockSpec((BLOCK, LANES), lambda bi, i: (i, 0)),
                  pl.BlockSpec((BLOCK, LANES), lambda bi, i: (i, 0)),
                  pl.BlockSpec((1, BLOCK, 256), prev),
                  pl.BlockSpec((1, BLOCK, 256), cur),
                  pl.BlockSpec((1, BLOCK, 256), nxt),
                  pl.BlockSpec((1, BLOCK, 256), vprev),
                  pl.BlockSpec((1, BLOCK, 256), vcur),
                  pl.BlockSpec((1, BLOCK, 256), vnxt),
                  pl.BlockSpec((1, 1, tc, 256), lambda bi, i: (bi, layer, 0, 0)),
                  pl.BlockSpec((1, 1, tc, 256), lambda bi, i: (bi, layer, 0, 0))],
        out_specs=pl.BlockSpec((1, BLOCK, ATTN_WIDTH), lambda bi, i: (bi, i, 0)),
        out_shape=jax.ShapeDtypeStruct((b, l, ATTN_WIDTH), BF16),
        scratch_shapes=[pltpu.VMEM((BLOCK, ATTN_WIDTH), BF16)],
        compiler_params=_cparams(("arbitrary", "arbitrary")),
        name="attn_lat",
    )(sink, u, cos, sin, k2r, k2r, k2r, u, u, u, kx2, vx2)


def _rope_tables(l):
    nf = HEAD_DIM // 4
    inv = ROPE_BASE ** (-jnp.arange(nf, dtype=F32) / nf)
    tok = jnp.arange(l)
    row = (tok // GRID_W).astype(F32)
    col = (tok % GRID_W).astype(F32)
    d = np.arange(LANES) % HEAD_DIM
    use_col = (d // (HEAD_DIM // 2)) == 1
    lo = (d % (HEAD_DIM // 2)) < nf
    pos = jnp.where(use_col[None, :], col[:, None], row[:, None])
    ang = pos * inv[d % nf][None, :]
    sign = np.where(lo, -1.0, 1.0).astype(np.float32)
    return jnp.cos(ang), jnp.sin(ang) * sign[None, :]


def _dot_split(a, b):
    ah = a.astype(BF16)
    bh = b.astype(BF16)
    al = (a - ah.astype(F32)).astype(BF16)
    bl = (b - bh.astype(F32)).astype(BF16)
    dot = functools.partial(jnp.dot, preferred_element_type=F32)
    return dot(ah, bh) + (dot(ah, bl) + dot(al, bh))


def _hy_hidden_body(z_ref, w1_ref, b1_ref, w2_ref, b2_ref, o_ref):
    h = jnp.sin(jnp.dot(z_ref[...], w1_ref[...], precision=HIGHEST, preferred_element_type=F32) + b1_ref[...])
    o_ref[...] = jnp.sin(jnp.dot(h, w2_ref[...], precision=HIGHEST, preferred_element_type=F32) + b2_ref[...])


def _hy_hidden(zfeat, w1, b1, w2, b2):
    rows = zfeat.shape[0]
    return pl.pallas_call(
        _hy_hidden_body,
        out_shape=jax.ShapeDtypeStruct((rows, LANES), F32),
        compiler_params=pltpu.CompilerParams(vmem_limit_bytes=VMEM_LIMIT),
        name="hy_hidden",
    )(zfeat, w1, b1, w2, b2)


def _hy_features(l):
    pos = jnp.arange(l, dtype=F32)
    t = pos / l
    freqs = jnp.linspace(1e-4, HY_BANDS - 1, HY_BANDS, dtype=F32)
    ang = 2.0 * jnp.pi * t[:, None] * freqs[None, :]
    z = jnp.concatenate([t[:, None], jnp.cos(ang), -jnp.sin(ang)], axis=-1)
    zrev = jnp.concatenate([z[:1], z[:0:-1]], axis=0)
    zz = jnp.concatenate([z, zrev], axis=0)
    return jnp.pad(zz, ((0, 0), (0, LANES - HY_EMB)))


def _hy_taps(hid_ref, w3f_ref, w3b_ref, decf_ref, decb_ref, l):
    ct = w3f_ref.shape[1]
    hf = _dot_split(hid_ref[0:l, :], w3f_ref[...])
    hb = _dot_split(hid_ref[l:2 * l, :], w3b_ref[...])
    row = lax.broadcasted_iota(jnp.int32, (l, ct), 0)
    t = row.astype(F32) / l
    trev = (l - row).astype(F32) / l
    hf = hf * jnp.exp(-t * jnp.abs(decf_ref[0]))
    hb = jnp.where(row == 0, 0.0, hb * jnp.exp(-trev * jnp.abs(decb_ref[0])))
    nsq = jnp.sum(hf * hf, axis=0, keepdims=True) + jnp.sum(hb * hb, axis=0, keepdims=True) + EPS
    sc = lax.rsqrt(nsq)
    return hf * sc, hb * sc


def _hy_filter_small_body(hid_ref, w3f_ref, w3b_ref, decf_ref, decb_ref, f_ref, o_ref, *, l):
    hf, hb = _hy_taps(hid_ref, w3f_ref, w3b_ref, decf_ref, decb_ref, l)
    taps = jnp.concatenate([hf, hb], axis=0)
    o_ref[0] = jnp.dot(f_ref[...], taps, precision=HIGHEST, preferred_element_type=F32) * (1.0 / (2 * l))


def _hy_filter_specs(ct):
    nct = HY_WIDTH // ct
    return [pl.BlockSpec((LANES, ct), lambda n, j: (0, n * nct + j)),
            pl.BlockSpec((LANES, ct), lambda n, j: (0, (HY_ORDER + n) * nct + j)),
            pl.BlockSpec((1, 1, ct), lambda n, j: (n, 0, j)),
            pl.BlockSpec((1, 1, ct), lambda n, j: (HY_ORDER + n, 0, j))]


def _hy_filter_small(hid, w3, decay, fmat, l):
    ct = 256
    return pl.pallas_call(
        functools.partial(_hy_filter_small_body, l=l),
        grid=(HY_ORDER, HY_WIDTH // ct),
        in_specs=[_const_spec((2 * l, LANES))] + _hy_filter_specs(ct) + [_const_spec(fmat.shape)],
        out_specs=pl.BlockSpec((1, 4 * l, ct), lambda n, j: (n, 0, j)),
        out_shape=jax.ShapeDtypeStruct((HY_ORDER, 4 * l, HY_WIDTH), F32),
        compiler_params=_cparams(("arbitrary", "arbitrary")),
        name="hy_filter_small",
    )(hid, w3, w3, decay, decay, fmat)


def _hy_filter_big_body(hid_ref, w3f_ref, w3b_ref, decf_ref, decb_ref, fa_ref, f2_ref, o_ref, xs, ss, *, l):
    n1 = 2 * l // LANES
    hf, hb = _hy_taps(hid_ref, w3f_ref, w3b_ref, decf_ref, decb_ref, l)
    for s in range(n1 // 2):
        xs[pl.ds(s * X_PITCH, LANES), :] = hf[s * LANES:(s + 1) * LANES]
        xs[pl.ds((n1 // 2 + s) * X_PITCH, LANES), :] = hb[s * LANES:(s + 1) * LANES]

    def stage_a(n2, carry):
        rhs = xs[pl.ds(n2, n1, stride=X_PITCH), :]
        out = _dot_split(fa_ref[n2], rhs)
        ss[pl.ds(n2, n1, stride=S_PITCH), :] = out[:n1]
        ss[pl.ds(LANES + n2, n1, stride=S_PITCH), :] = out[n1:]
        return carry

    lax.fori_loop(0, LANES, stage_a, 0, unroll=8)

    def stage_b(kp, carry):
        slabs = [ss[pl.ds(pl.multiple_of((2 * kp + e) * S_PITCH, 8), 2 * LANES), :] for e in range(2)]
        out = _dot_split(f2_ref[...], jnp.concatenate(slabs, axis=1)) * (1.0 / (2 * l))
        for e in range(2):
            rows = pl.ds(pl.multiple_of((2 * kp + e) * 2 * LANES, 2 * LANES), 2 * LANES)
            o_ref[0, rows, :] = out[:, e * LANES:(e + 1) * LANES]
        return carry

    lax.fori_loop(0, n1 // 2, stage_b, 0, unroll=2)


def _hy_filter_big(hid, w3, decay, fa, f2, l):
    n1 = 2 * l // LANES
    return pl.pallas_call(
        functools.partial(_hy_filter_big_body, l=l),
        grid=(HY_ORDER, HY_WIDTH // LANES),
        in_specs=[_const_spec((2 * l, LANES))] + _hy_filter_specs(LANES)
        + [_const_spec(fa.shape), _const_spec(f2.shape)],
        out_specs=pl.BlockSpec((1, 4 * l, LANES), lambda n, j: (n, 0, j)),
        out_shape=jax.ShapeDtypeStruct((HY_ORDER, 4 * l, HY_WIDTH), F32),
        scratch_shapes=[pltpu.VMEM((n1 * X_PITCH, LANES), F32),
                        pltpu.VMEM((n1 * S_PITCH, LANES), F32)],
        compiler_params=_cparams(("arbitrary", "arbitrary")),
        name="hy_filter_big",
    )(hid, w3, w3, decay, decay, fa, f2)


def _shift_rows(x, k):
    l = x.shape[0]
    y = pltpu.roll(x, k % l, 0)
    row = lax.broadcasted_iota(jnp.int32, (8, x.shape[1]), 0)
    if k > 0:
        return jnp.concatenate([jnp.where(row < k, 0.0, y[:8]), y[8:]], axis=0)
    return jnp.concatenate([y[:l - 8], jnp.where(row >= 8 + k, 0.0, y[l - 8:])], axis=0)


def _short_conv(x, w_ref, b_ref):
    prev = _shift_rows(x, 1)
    nxt = _shift_rows(x, -1)
    return prev * w_ref[0:1, :] + x * w_ref[1:2, :] + nxt * w_ref[2:3, :] + b_ref[...]


def _hy_small_body(uz_ref, ux1_ref, ux2_ref, wz_ref, wx1_ref, wx2_ref, bz_ref, bx1_ref, bx2_ref,
                   h_ref, bias_ref, f_ref, g_ref, o_ref):
    l = uz_ref.shape[1]
    n = 2 * l
    sig = [_short_conv(uz_ref[b].astype(F32), wz_ref, bz_ref) for b in range(2)]
    gates = [[_short_conv(ux1_ref[b].astype(F32), wx1_ref, bx1_ref) for b in range(2)],
             [_short_conv(ux2_ref[b].astype(F32), wx2_ref, bx2_ref) for b in range(2)]]
    for o in range(HY_ORDER):
        x = jnp.concatenate(sig, axis=0).astype(BF16)
        a = jnp.dot(f_ref[...], x, preferred_element_type=F32)
        ar, ai = a[:n], a[n:]
        hr, hi = h_ref[o, 0:n, :], h_ref[o, n:2 * n, :]
        y = jnp.concatenate([ar * hr - ai * hi, ar * hi + ai * hr], axis=0).astype(BF16)
        conv = jnp.dot(g_ref[...], y, preferred_element_type=F32)
        bias = bias_ref[o:o + 1, :]
        sig = [gates[o][b] * (conv[b * l:(b + 1) * l] + bias * sig[b]) for b in range(2)]
    for b in range(2):
        o_ref[b] = sig[b].astype(o_ref.dtype)


def _hy_small(u, short_w, short_b, hspec, hy_bias, fmat, gmat):
    b, l, _ = u.shape
    ct = 256
    col = lambda off: (lambda j, p: (p, 0, off // ct + j))
    wcol = lambda off: (lambda j, p: (0, off // ct + j))
    return pl.pallas_call(
        _hy_small_body,
        grid=(HY_WIDTH // ct, b // 2),
        in_specs=[pl.BlockSpec((2, l, ct), col(U_Z)),
                  pl.BlockSpec((2, l, ct), col(U_X1)),
                  pl.BlockSpec((2, l, ct), col(U_X2)),
                  pl.BlockSpec((3, ct), wcol(2 * HY_WIDTH)),
                  pl.BlockSpec((3, ct), wcol(0)),
                  pl.BlockSpec((3, ct), wcol(HY_WIDTH)),
                  pl.BlockSpec((1, ct), wcol(2 * HY_WIDTH)),
                  pl.BlockSpec((1, ct), wcol(0)),
                  pl.BlockSpec((1, ct), wcol(HY_WIDTH)),
                  pl.BlockSpec((HY_ORDER, 4 * l, ct), lambda j, p: (0, 0, j)),
                  pl.BlockSpec((HY_ORDER, ct), lambda j, p: (0, j)),
                  _const_spec(fmat.shape),
                  _const_spec(gmat.shape)],
        out_specs=pl.BlockSpec((2, l, ct), lambda j, p: (p, 0, j)),
        out_shape=jax.ShapeDtypeStruct((b, l, HY_WIDTH), BF16),
        compiler_params=_cparams(("arbitrary", "arbitrary")),
        name="hy_small",
    )(u, u, u, short_w, short_w, short_w, short_b, short_b, short_b, hspec, hy_bias, fmat, gmat)


def _hy_big_body(*refs, conv_sig, l):
    if conv_sig:
        (sig_ref, ws_ref, bs_ref, gate_ref, wg_ref, bg_ref, h_ref, bias_ref,
         fa_ref, ga_ref, f2_ref, f2i_ref, o_ref, xs, ss, ys) = refs
    else:
        (sig_ref, gate_ref, wg_ref, bg_ref, h_ref, bias_ref,
         fa_ref, ga_ref, f2_ref, f2i_ref, o_ref, xs, ss, ys) = refs
    n1 = 2 * l // LANES
    hn = n1 // 2

    for b in range(2):
        x = sig_ref[b].astype(F32)
        if conv_sig:
            x = _short_conv(x, ws_ref, bs_ref)
        for s in range(hn):
            xs[pl.ds((b * hn + s) * X_PITCH, LANES), :] = x[s * LANES:(s + 1) * LANES]

    def stage_a(n2, carry):
        xr = xs[pl.ds(n2, hn, stride=X_PITCH), :]
        xi = xs[pl.ds(hn * X_PITCH + n2, hn, stride=X_PITCH), :]
        rhs = jnp.concatenate([xr, xi], axis=0).astype(BF16)
        out = jnp.dot(fa_ref[n2], rhs, preferred_element_type=F32)
        ss[pl.ds(n2, n1, stride=S_PITCH), :] = out[:n1]
        ss[pl.ds(LANES + n2, n1, stride=S_PITCH), :] = out[n1:]
        return carry

    lax.fori_loop(0, LANES, stage_a, 0, unroll=8)

    pairs = min(4, n1 // 2)

    def stage_b(kg, carry):
        k1s = [[(kg * pairs + p) * 2 + e for e in range(2)] for p in range(pairs)]
        slabs = [jnp.concatenate([ss[pl.ds(pl.multiple_of(k1 * S_PITCH, 8), 2 * LANES), :] for k1 in pr],
                                 axis=1).astype(BF16) for pr in k1s]
        backs = []
        for p in range(pairs):
            x = jnp.dot(f2_ref[...], slabs[p], preferred_element_type=F32)
            hbs = [pl.multiple_of(k1 * 2 * LANES, 2 * LANES) for k1 in k1s[p]]
            hr = jnp.concatenate([h_ref[0, pl.ds(hb, LANES), :] for hb in hbs], axis=1)
            hi = jnp.concatenate([h_ref[0, pl.ds(hb + LANES, LANES), :] for hb in hbs], axis=1)
            xr, xi = x[:LANES], x[LANES:]
            y = jnp.concatenate([xr * hr - xi * hi, xr * hi + xi * hr], axis=0).astype(BF16)
            backs.append(jnp.dot(f2i_ref[...], y, preferred_element_type=F32))
        for p in range(pairs):
            for e in range(2):
                ss[pl.ds(pl.multiple_of(k1s[p][e] * S_PITCH, 8), 2 * LANES), :] = backs[p][:, e * LANES:(e + 1) * LANES]
        return carry

    lax.fori_loop(0, n1 // (2 * pairs), stage_b, 0)

    def stage_a_inv(n2, carry):
        br = ss[pl.ds(n2, n1, stride=S_PITCH), :]
        bi = ss[pl.ds(LANES + n2, n1, stride=S_PITCH), :]
        rhs = jnp.concatenate([br, bi], axis=0).astype(BF16)
        out = jnp.dot(ga_ref[n2], rhs, preferred_element_type=F32)
        ys[pl.ds(n2, hn, stride=X_PITCH), :] = out[:hn]
        ys[pl.ds(hn * X_PITCH + n2, hn, stride=X_PITCH), :] = out[hn:]
        return carry

    lax.fori_loop(0, LANES, stage_a_inv, 0, unroll=8)

    bias = bias_ref[...]
    for b in range(2):
        g = _short_conv(gate_ref[b].astype(F32), wg_ref, bg_ref)
        for s in range(hn):
            rows = pl.ds((b * hn + s) * X_PITCH, LANES)
            val = g[s * LANES:(s + 1) * LANES] * (ys[rows, :] + bias * xs[rows, :])
            o_ref[b, s * LANES:(s + 1) * LANES, :] = val.astype(o_ref.dtype)


def _hy_big(sig, sig_off, sig_w, gate, gate_off, gate_woff, short_w, short_b, hspec, order, hy_bias,
            tables, out_dtype):
    b, l, _ = gate.shape
    n1 = 2 * l // LANES
    fa, ga, f2, f2i = tables
    conv_sig = sig_w is not None
    col = lambda off: (lambda j, p: (p, 0, off // LANES + j))
    wcol = lambda off: (lambda j, p: (0, off // LANES + j))
    in_specs = [pl.BlockSpec((2, l, LANES), col(sig_off))]
    args = [sig]
    if conv_sig:
        in_specs += [pl.BlockSpec((3, LANES), wcol(sig_w)), pl.BlockSpec((1, LANES), wcol(sig_w))]
        args += [short_w, short_b]
    in_specs += [pl.BlockSpec((2, l, LANES), col(gate_off)),
                 pl.BlockSpec((3, LANES), wcol(gate_woff)),
                 pl.BlockSpec((1, LANES), wcol(gate_woff)),
                 pl.BlockSpec((1, 4 * l, LANES), lambda j, p: (order, 0, j), pipeline_mode=pl.Buffered(1)),
                 pl.BlockSpec((1, LANES), lambda j, p: (0, j)),
                 _const_spec(fa.shape), _const_spec(ga.shape), _const_spec(f2.shape), _const_spec(f2i.shape)]
    args += [gate, short_w, short_b, hspec, hy_bias[order][None, :], fa, ga, f2, f2i]
    return pl.pallas_call(
        functools.partial(_hy_big_body, conv_sig=conv_sig, l=l),
        grid=(HY_WIDTH // LANES, b // 2),
        in_specs=in_specs,
        out_specs=pl.BlockSpec((2, l, LANES), lambda j, p: (p, 0, j)),
        out_shape=jax.ShapeDtypeStruct((b, l, HY_WIDTH), out_dtype),
        scratch_shapes=[pltpu.VMEM((n1 * X_PITCH, LANES), F32),
                        pltpu.VMEM((n1 * S_PITCH, LANES), F32),
                        pltpu.VMEM((n1 * X_PITCH, LANES), F32)],
        compiler_params=_cparams(("arbitrary", "arbitrary")),
        name="hy_big",
    )(*args)


def _cis(idx, n):
    ang = idx.astype(F32) * (2.0 * np.pi / n)
    return jnp.cos(ang), jnp.sin(ang)


def _stack_complex(cr, ci):
    return jnp.concatenate([jnp.concatenate([cr, -ci], axis=-1), jnp.concatenate([ci, cr], axis=-1)], axis=-2)


def _dense_dft_tables(l):
    n = 2 * l
    k = jnp.arange(n, dtype=jnp.int32)
    t = jnp.arange(l, dtype=jnp.int32)
    cr, ci = _cis((k[:, None] * t[None, :]) % n, n)
    fwd = _stack_complex(cr, -ci)
    inv = _stack_complex(cr.T, ci.T)
    cfr, cfi = _cis((k[:, None] * k[None, :]) % n, n)
    filt = jnp.concatenate([cfr, -cfi], axis=0)
    return fwd.astype(BF16), inv.astype(BF16), filt


def _two_stage_dft_tables(l):
    n = 2 * l
    n1 = n // LANES
    hn = n1 // 2
    n2 = jnp.arange(LANES, dtype=jnp.int32)[:, None, None]
    k1 = jnp.arange(n1, dtype=jnp.int32)[None, :, None]
    m1 = jnp.arange(n1, dtype=jnp.int32)[None, None, :]
    cr, ci = _cis((k1 * (LANES * m1 + n2)) % n, n)
    fa = _stack_complex(cr[:, :, :hn], -ci[:, :, :hn])
    ga = _stack_complex(jnp.swapaxes(cr, 1, 2)[:, :hn, :], jnp.swapaxes(ci, 1, 2)[:, :hn, :])
    fa_filt = jnp.concatenate([cr, -ci], axis=1)
    a = jnp.arange(LANES, dtype=jnp.int32)
    c2r, c2i = _cis((a[:, None] * a[None, :]) % LANES, LANES)
    f2 = _stack_complex(c2r, -c2i)
    f2i = _stack_complex(c2r, c2i)
    return (fa.astype(BF16), ga.astype(BF16), f2.astype(BF16), f2i.astype(BF16)), fa_filt, f2


def _silu(x):
    return x * jax.nn.sigmoid(x)


def _proj_out_even_body(att_ref, ga_ref, hy_ref, gh_ref, w_ref, x_ref, g_ref, o_ref):
    a = att_ref[0].astype(F32) * _silu(ga_ref[0].astype(F32))
    h = hy_ref[0].astype(F32) * _silu(gh_ref[0].astype(F32))
    mixed = jnp.concatenate([a, h], axis=-1).astype(BF16)
    res = jnp.dot(mixed, w_ref[...], preferred_element_type=F32)
    o_ref[0] = x_ref[0] + g_ref[0] * res


def _proj_out_even(att, hy, u, w_out, x, gate, tm):
    b, l, d = x.shape
    per_batch = gate.shape[0] == b
    gmap = (lambda bi, i: (bi, 0, 0)) if per_batch else (lambda bi, i: (0, 0, 0))
    return pl.pallas_call(
        _proj_out_even_body,
        grid=(b, l // tm),
        in_specs=[pl.BlockSpec((1, tm, ATTN_WIDTH), lambda bi, i: (bi, i, 0)),
                  pl.BlockSpec((1, tm, ATTN_WIDTH), lambda bi, i: (bi, i, U_GA // ATTN_WIDTH)),
                  pl.BlockSpec((1, tm, HY_WIDTH), lambda bi, i: (bi, i, 0)),
                  pl.BlockSpec((1, tm, HY_WIDTH), lambda bi, i: (bi, i, U_GH // HY_WIDTH)),
                  _const_spec(w_out.shape),
                  pl.BlockSpec((1, tm, d), lambda bi, i: (bi, i, 0)),
                  pl.BlockSpec((1, 1, d), gmap)],
        out_specs=pl.BlockSpec((1, tm, d), lambda bi, i: (bi, i, 0)),
        out_shape=jax.ShapeDtypeStruct((b, l, d), F32),
        compiler_params=_cparams(("arbitrary", "arbitrary")),
        name="proj_out_even",
    )(att, u, hy, u, w_out, x, gate)


def _proj_out_odd_body(yf_ref, yb_ref, gt_ref, w_ref, x_ref, g_ref, fg_ref, o_ref, *, final_norm):
    y = yf_ref[0].astype(F32) + yb_ref[0].astype(F32)
    mixed = (y * _silu(gt_ref[0].astype(F32))).astype(BF16)
    res = jnp.dot(mixed, w_ref[...], preferred_element_type=F32)
    xn = x_ref[0] + g_ref[0] * res
    if final_norm:
        ms = jnp.mean(xn * xn, axis=-1, keepdims=True)
        xn = xn * lax.rsqrt(ms + EPS) * fg_ref[...]
    o_ref[0] = xn


def _proj_out_odd(yf, yb, u2, w_out, x, gate, final_g, final_norm, tm):
    b, l, d = x.shape
    per_batch = gate.shape[0] == b
    gmap = (lambda bi, i: (bi, 0, 0)) if per_batch else (lambda bi, i: (0, 0, 0))
    return pl.pallas_call(
        functools.partial(_proj_out_odd_body, final_norm=final_norm),
        grid=(b, l // tm),
        in_specs=[pl.BlockSpec((1, tm, d), lambda bi, i: (bi, i, 0)),
                  pl.BlockSpec((1, tm, d), lambda bi, i: (bi, i, 0)),
                  pl.BlockSpec((1, tm, d), lambda bi, i: (bi, i, 1)),
                  _const_spec(w_out.shape),
                  pl.BlockSpec((1, tm, d), lambda bi, i: (bi, i, 0)),
                  pl.BlockSpec((1, 1, d), gmap),
                  pl.BlockSpec((1, d), lambda bi, i: (0, 0))],
        out_specs=pl.BlockSpec((1, tm, d), lambda bi, i: (bi, i, 0)),
        out_shape=jax.ShapeDtypeStruct((b, l, d), F32),
        compiler_params=_cparams(("arbitrary", "arbitrary")),
        name="proj_out_odd",
    )(yf, yb, u2, w_out, x, gate, final_g.reshape(1, d))


def _rmsnorm_body(x_ref, g_ref, o_ref):
    x = x_ref[0]
    ms = jnp.mean(x * x, axis=-1, keepdims=True)
    o_ref[0] = x * lax.rsqrt(ms + EPS) * g_ref[...]


def _rmsnorm(x, g, tm):
    b, l, d = x.shape
    return pl.pallas_call(
        _rmsnorm_body,
        grid=(b, l // tm),
        in_specs=[pl.BlockSpec((1, tm, d), lambda bi, i: (bi, i, 0)),
                  pl.BlockSpec((1, d), lambda bi, i: (0, 0))],
        out_specs=pl.BlockSpec((1, tm, d), lambda bi, i: (bi, i, 0)),
        out_shape=jax.ShapeDtypeStruct((b, l, d), F32),
        compiler_params=_cparams(("arbitrary", "arbitrary")),
        name="final_rmsnorm",
    )(x, g.reshape(1, d))


LRU_GROUP = 8


LRU_HALO = 2 * LRU_GROUP


def _lru_gates(d, x_ref, xp_ref, xn_ref, keep_prev, keep_next, cw_ref, cb_ref, wa_ref, ba_ref, wx_ref, bx_ref,
               lam_ref, x_scr, a_scr, b_scr):
    tt = x_ref.shape[1]
    g = LRU_GROUP
    nk = LRU_BLOCK // LANES
    for s in range(g):
        for k in range(nk):
            lanes = slice(k * LANES, (k + 1) * LANES)
            x_scr[d, k, pl.ds(LRU_HALO + s, tt, stride=g), :] = x_ref[s, :, lanes].astype(F32)
            x_scr[d, k, pl.ds(s, 1), :] = xp_ref[s, 6:7, lanes].astype(F32) * keep_prev
            x_scr[d, k, pl.ds(g + s, 1), :] = xp_ref[s, 7:8, lanes].astype(F32) * keep_prev
            x_scr[d, k, pl.ds(LRU_HALO + tt * g + s, 1), :] = xn_ref[s, 0:1, lanes].astype(F32) * keep_next
    xcs = []
    for k in range(nk):
        lanes = slice(k * LANES, (k + 1) * LANES)
        acc = cb_ref[:, lanes]
        for j in range(4):
            acc = acc + x_scr[d, k, pl.ds(j * g, tt * g), :] * cw_ref[j:j + 1, lanes]
        xcs.append(acc)
    xc = jnp.concatenate(xcs, axis=1)
    xcb = xc.astype(BF16)
    lam = lam_ref[d:d + 1, :]
    softplus_neg = jnp.maximum(-lam, 0.0) + jnp.log1p(jnp.exp(-jnp.abs(lam)))
    tr = jnp.tanh(jnp.dot(xcb, wa_ref[d, 0], preferred_element_type=F32) + ba_ref[d:d + 1, :])
    ti = jnp.tanh(jnp.dot(xcb, wx_ref[d, 0], preferred_element_type=F32) + bx_ref[d:d + 1, :])
    half_c = (-0.5 * LRU_C) * softplus_neg
    log_a = half_c * tr + half_c
    a = jnp.exp(log_a)
    w4 = jnp.tanh(log_a) * (-0.25 * (a * a) - 0.25)
    root = jnp.where(w4 > 0.0, w4 * lax.rsqrt(w4), 0.0)
    b = root * (ti + 1.0) * xc
    for k in range(nk):
        a_scr[d, k] = a[:, k * LANES:(k + 1) * LANES]
        b_scr[d, k] = b[:, k * LANES:(k + 1) * LANES]


def _lru_body(xf_ref, xfp_ref, xfn_ref, xb_ref, xbp_ref, xbn_ref, cw_ref, cb_ref, wa_ref, ba_ref, wx_ref, bx_ref,
              lam_ref, h0_ref, yf_ref, yb_ref, fin_ref, x_scr, a_scr, b_scr, y_scr, h_scr):
    i = pl.program_id(2)
    nt = pl.num_programs(2)
    tt = xf_ref.shape[1]
    nk = LRU_BLOCK // LANES

    @pl.when(i == 0)
    def _():
        h_scr[...] = h0_ref[...]

    first = jnp.where(i == 0, 0.0, 1.0)
    last = jnp.where(i == nt - 1, 0.0, 1.0)
    params = (cw_ref, cb_ref, wa_ref, ba_ref, wx_ref, bx_ref, lam_ref, x_scr, a_scr, b_scr)
    _lru_gates(0, xf_ref, xfp_ref, xfn_ref, first, last, *params)
    _lru_gates(1, xb_ref, xbp_ref, xbn_ref, last, first, *params)

    def step(t, carry):
        rf = pl.ds(pl.multiple_of(t * LRU_GROUP, LRU_GROUP), LRU_GROUP)
        rb = pl.ds(pl.multiple_of((tt - 1 - t) * LRU_GROUP, LRU_GROUP), LRU_GROUP)
        out = []
        for d, rows in ((0, rf), (1, rb)):
            for k in range(nk):
                h = a_scr[d, k, rows, :] * carry[d * nk + k] + b_scr[d, k, rows, :]
                y_scr[d, k, rows, :] = h
                out.append(h)
        return tuple(out)

    init = tuple(h_scr[d, :, k * LANES:(k + 1) * LANES] for d in range(2) for k in range(nk))
    fin = lax.fori_loop(0, tt, step, init, unroll=8)
    for d in range(2):
        for k in range(nk):
            h_scr[d, :, k * LANES:(k + 1) * LANES] = fin[d * nk + k]

    for s in range(LRU_GROUP):
        for k in range(nk):
            rows = pl.ds(s, tt, stride=LRU_GROUP)
            yf_ref[s, :, k * LANES:(k + 1) * LANES] = y_scr[0, k, rows, :].astype(yf_ref.dtype)
            yb_ref[s, :, k * LANES:(k + 1) * LANES] = y_scr[1, k, rows, :].astype(yb_ref.dtype)

    @pl.when(i == nt - 1)
    def _():
        fin_ref[...] = h_scr[...]


def _lru(u2, conv_w, conv_b, wa, ba, wx, bx, lam, h0):
    b, l, _ = u2.shape
    w = N_HEADS_C * LRU_BLOCK
    hb = LRU_BLOCK
    g = LRU_GROUP
    assert b % g == 0, "the scan packs LRU_GROUP sequences per vreg"
    tt = min(l, 256)
    nt = l // tt
    hr = tt // 8
    nhalo = l // 8
    vec = lambda rows: pl.BlockSpec((rows, hb), lambda h, gi, i: (0, h))
    wspec = pl.BlockSpec((2, 1, hb, hb), lambda h, gi, i: (0, h, 0, 0))
    tile = lambda f: pl.BlockSpec((g, tt, hb), lambda h, gi, i: (gi, f(i), h))
    halo_prev = lambda f: pl.BlockSpec((g, 8, hb), lambda h, gi, i: (gi, jnp.maximum(f(i) * hr - 1, 0), h))
    halo_next = lambda f: pl.BlockSpec((g, 8, hb), lambda h, gi, i: (gi, jnp.minimum((f(i) + 1) * hr, nhalo - 1), h))
    fwd = lambda i: i
    bwd = lambda i: nt - 1 - i
    state = pl.BlockSpec((2, g, hb), lambda h, gi, i: (0, gi, h))
    return pl.pallas_call(
        _lru_body,
        grid=(N_HEADS_C, b // g, nt),
        in_specs=[tile(fwd), halo_prev(fwd), halo_next(fwd), tile(bwd), halo_prev(bwd), halo_next(bwd),
                  vec(4), vec(1), wspec, vec(2), wspec, vec(2), vec(2), state],
        out_specs=[tile(fwd), tile(bwd), state],
        out_shape=[jax.ShapeDtypeStruct((b, l, w), BF16),
                   jax.ShapeDtypeStruct((b, l, w), BF16),
                   jax.ShapeDtypeStruct((2, b, w), F32)],
        scratch_shapes=[pltpu.VMEM((2, hb // LANES, (tt + 3) * g, LANES), F32),
                        pltpu.VMEM((2, hb // LANES, tt * g, LANES), F32),
                        pltpu.VMEM((2, hb // LANES, tt * g, LANES), F32),
                        pltpu.VMEM((2, hb // LANES, tt * g, LANES), F32),
                        pltpu.VMEM((2, g, hb), F32)],
        compiler_params=_cparams(("arbitrary", "arbitrary", "arbitrary")),
        name="rg_lru",
    )(u2, u2, u2, u2, u2, u2, conv_w, conv_b.reshape(1, w), wa, ba, wx, bx, lam, h0)


def _even_w_in(w):
    q = w[:, 0:ATTN_WIDTH]
    k = w[:, ATTN_WIDTH:ATTN_WIDTH + KV_COLS]
    v = w[:, ATTN_WIDTH + KV_COLS:ATTN_WIDTH + 2 * KV_COLS]
    o = ATTN_WIDTH + 2 * KV_COLS
    ga = w[:, o:o + ATTN_WIDTH]
    hy = w[:, o + ATTN_WIDTH:o + ATTN_WIDTH + 3 * HY_WIDTH]
    gh = w[:, o + ATTN_WIDTH + 3 * HY_WIDTH:]
    dup = lambda m: jnp.concatenate([m[:, h * HEAD_DIM:(h + 1) * HEAD_DIM] for h in range(N_KV_A) for _ in range(2)],
                                    axis=1)
    return jnp.concatenate([q, ga, hy, gh, dup(k), dup(v)], axis=1).astype(BF16)


def _dup_heads(x):
    return jnp.concatenate([x[..., h, :] for h in range(N_KV_A) for _ in range(2)], axis=-1).astype(BF16)


def kernel(x_prompt, x_sample, cache_k, cache_v, state_lru, c, c_ctx, mod_w, mod_b, norm_g, final_norm_g, a_w_in, a_w_out, a_sink, hy_short_w, hy_short_b, hy_w1, hy_b1, hy_w2, hy_b2, hy_w3, hy_decay, hy_bias, c_w_in, c_w_out, c_conv_w, c_conv_b, c_wa, c_ba, c_wx, c_bx, c_lambda):
    depth, d, _ = mod_w.shape
    bp, lp, _ = x_prompt.shape
    bs, ls, _ = x_sample.shape

    cvec = jnp.concatenate([c, c_ctx[None, :], jnp.zeros((16 - bs - 1, d), F32)], axis=0)
    mod = _modulation(cvec, mod_w, mod_b)

    cos, sin = _rope_tables(ls)
    kx2 = _dup_heads(cache_k)
    vx2 = _dup_heads(cache_v)
    tabs_big, fa_filt, f2_filt = _two_stage_dft_tables(ls)
    fwd_small, inv_small, filt_small = _dense_dft_tables(lp)
    feat_p, feat_s = _hy_features(lp), _hy_features(ls)
    hid_pad = ((0, LANES - hy_w1.shape[-1]),)

    tm_s = min(ls, 1024)
    xp, xs = x_prompt, x_sample
    new_k, new_v, new_s = [], [], []
    for layer in range(depth):
        j = layer // 2
        sh_s, sc_s, g_s = (mod[layer, :bs, None, i * d:(i + 1) * d] for i in range(3))
        sh_p, sc_p, g_p = (mod[layer, bs:bs + 1, None, i * d:(i + 1) * d] for i in range(3))
        last = layer == depth - 1
        if layer % 2 == 0:
            w_in = _even_w_in(a_w_in[j])
            w_out = a_w_out[j].astype(BF16)
            up = _proj_in_flat(xp, norm_g[layer], sc_p, sh_p, w_in, 13)
            us = _proj_in(xs, norm_g[layer], sc_s, sh_s, w_in, tm_s, 13)
            new_k.append(jnp.stack([up[:, :, U_K2 + 2 * h * HEAD_DIM:U_K2 + (2 * h + 1) * HEAD_DIM]
                                    for h in range(N_KV_A)], axis=2).astype(F32))
            new_v.append(jnp.stack([up[:, :, U_V2 + 2 * h * HEAD_DIM:U_V2 + (2 * h + 1) * HEAD_DIM]
                                    for h in range(N_KV_A)], axis=2).astype(F32))
            att_p = _attn_ctx(up, a_sink[j])
            k2r = _rope_k(us, cos, sin, min(ls, 512))
            att_s = _attn_lat(us, k2r, kx2, vx2, j, a_sink[j], cos, sin)

            w1 = jnp.pad(hy_w1[j], ((0, LANES - HY_EMB),) + hid_pad)
            b1 = jnp.pad(hy_b1[j][None, :], ((0, 0),) + hid_pad)
            w2 = jnp.pad(hy_w2[j], hid_pad + hid_pad)
            b2 = jnp.pad(hy_b2[j][None, :], ((0, 0),) + hid_pad)
            w3 = jnp.pad(hy_w3[j], hid_pad + ((0, 0),))
            decay = hy_decay[j].reshape(2 * HY_ORDER, 1, HY_WIDTH)
            sw, sb = hy_short_w[j], hy_short_b[j][None, :]
            hspec_p = _hy_filter_small(_hy_hidden(feat_p, w1, b1, w2, b2), w3, decay, filt_small, lp)
            hspec_s = _hy_filter_big(_hy_hidden(feat_s, w1, b1, w2, b2), w3, decay, fa_filt, f2_filt, ls)
            hy_p = _hy_small(up, sw, sb, hspec_p, hy_bias[j], fwd_small, inv_small)
            z1 = _hy_big(us, U_Z, 2 * HY_WIDTH, us, U_X1, 0, sw, sb, hspec_s, 0, hy_bias[j], tabs_big, F32)
            hy_s = _hy_big(z1, 0, None, us, U_X2, HY_WIDTH, sw, sb, hspec_s, 1, hy_bias[j], tabs_big, BF16)

            xp = _proj_out_even(att_p, hy_p, up, w_out, xp, g_p, lp)
            xs = _proj_out_even(att_s, hy_s, us, w_out, xs, g_s, min(ls, 512))
            if last:
                xp = _rmsnorm(xp, final_norm_g, lp)
                xs = _rmsnorm(xs, final_norm_g, min(ls, 512))
        else:
            w_in = c_w_in[j].astype(BF16)
            w_out = c_w_out[j].astype(BF16)
            wa = (0.5 * c_wa[j]).astype(BF16)
            wx = (0.5 * c_wx[j]).astype(BF16)
            up = _proj_in_flat(xp, norm_g[layer], sc_p, sh_p, w_in, 2)
            us = _proj_in(xs, norm_g[layer], sc_s, sh_s, w_in, tm_s, 2)
            lru_args = (c_conv_w[j], c_conv_b[j], wa, 0.5 * c_ba[j], wx, 0.5 * c_bx[j], c_lambda[j])
            yf_p, yb_p, fin_p = _lru(up, *lru_args, jnp.zeros((2, bp, d), F32))
            yf_s, yb_s, _ = _lru(us, *lru_args, jnp.moveaxis(state_lru[:, j], 1, 0))
            new_s.append(jnp.moveaxis(fin_p, 0, 1))
            xp = _proj_out_odd(yf_p, yb_p, up, w_out, xp, g_p, final_norm_g, last, lp)
            xs = _proj_out_odd(yf_s, yb_s, us, w_out, xs, g_s, final_norm_g, last, min(ls, 512))

    return (xp, xs, jnp.stack(new_k, axis=1), jnp.stack(new_v, axis=1), jnp.stack(new_s, axis=1))
```

```python
import functools

import numpy as np
import jax
import jax.numpy as jnp
from jax import lax
from jax.experimental import pallas as pl
from jax.experimental.pallas import tpu as pltpu

F32 = jnp.float32
BF16 = jnp.bfloat16
HIGHEST = lax.Precision.HIGHEST

EPS = 1e-6
HEAD_DIM = 64
N_HEADS_A = 16
N_KV_A = 2
GQA_GROUP = N_HEADS_A // N_KV_A
ATTN_WIDTH = N_HEADS_A * HEAD_DIM
KV_COLS = N_KV_A * HEAD_DIM
WINDOW = 128
BLOCK = 128
GRID_W = 64
ROPE_BASE = 10000.0
HY_WIDTH = 1024
HY_ORDER = 2
HY_BANDS = 16
HY_EMB = 2 * HY_BANDS + 1
N_HEADS_C = 8
LRU_BLOCK = 256
LRU_C = 8.0
ATTN_SCALE = HEAD_DIM ** -0.5
NEG = float(np.finfo(np.float32).min)

LANES = 128
X_PITCH = LANES + 8
S_PITCH = 2 * LANES + 8
FFT_SIDE = 4
FFT_UNROLL = 64

U_Q, U_GA, U_X1, U_X2, U_Z, U_GH, U_K2, U_V2 = 0, 1024, 2048, 3072, 4096, 5120, 6144, 6400
U_COLS = 6656

VMEM_LIMIT = 56 * 1024 * 1024


def _cparams(sem, vmem=VMEM_LIMIT):
    return pltpu.CompilerParams(dimension_semantics=sem, vmem_limit_bytes=vmem)


def _const_spec(shape):
    nd = len(shape)
    return pl.BlockSpec(shape, lambda *_: (0,) * nd, pipeline_mode=pl.Buffered(1))


def _mod_body(c_ref, w_ref, b_ref, o_ref):
    c = c_ref[...]
    s = c * jax.nn.sigmoid(c)
    o_ref[0] = jnp.dot(s.astype(BF16), w_ref[0].astype(BF16), preferred_element_type=F32) + b_ref[0]


def _modulation(cvec, mod_w, mod_b):
    depth, d, n3 = mod_w.shape
    rows = cvec.shape[0]
    tn = 512
    return pl.pallas_call(
        _mod_body,
        grid=(depth, n3 // tn),
        in_specs=[pl.BlockSpec((rows, d), lambda l, j: (0, 0)),
                  pl.BlockSpec((1, d, tn), lambda l, j: (l, 0, j)),
                  pl.BlockSpec((1, 1, tn), lambda l, j: (l, 0, j))],
        out_specs=pl.BlockSpec((1, rows, tn), lambda l, j: (l, 0, j)),
        out_shape=jax.ShapeDtypeStruct((depth, rows, n3), F32),
        compiler_params=_cparams(("arbitrary", "arbitrary")),
        name="modulation",
    )(cvec, mod_w, mod_b.reshape(depth, 1, n3))


NORM_ROWS = 16
DOT_COLS = 512


def _col_chunks(tn):
    edges = list(range(0, tn, DOT_COLS)) + [tn]
    if len(edges) > 2 and edges[-1] - edges[-2] < 256:
        del edges[-2]
    return list(zip(edges[:-1], edges[1:]))


def _proj_in_body(x_ref, g_ref, sc_ref, sh_ref, w_ref, o_ref, h_scr):
    @pl.when(pl.program_id(2) == 0)
    def _():
        gain = g_ref[...]
        scale1 = 1.0 + sc_ref[0]
        shift = sh_ref[0]

        def rows(r, carry):
            sl = pl.ds(pl.multiple_of(r * NORM_ROWS, NORM_ROWS), NORM_ROWS)
            x = x_ref[0, sl, :]
            ms = jnp.mean(x * x, axis=-1, keepdims=True)
            y = x * lax.rsqrt(ms + EPS) * gain
            h_scr[sl, :] = (y * scale1 + shift).astype(BF16)
            return carry

        lax.fori_loop(0, x_ref.shape[1] // NORM_ROWS, rows, 0, unroll=8)

    for c0, c1 in _col_chunks(w_ref.shape[1]):
        o_ref[0, :, c0:c1] = jnp.dot(h_scr[...], w_ref[:, c0:c1], preferred_element_type=F32).astype(o_ref.dtype)


def _proj_in(x, norm_g, scale, shift, w, tm, n_col_tiles):
    b, l, d = x.shape
    n = w.shape[1]
    tn = n // n_col_tiles
    assert tn * n_col_tiles == n and tn % LANES == 0
    per_batch = scale.shape[0] == b
    mod_map = (lambda bi, i, j: (bi, 0, 0)) if per_batch else (lambda bi, i, j: (0, 0, 0))
    return pl.pallas_call(
        _proj_in_body,
        grid=(b, l // tm, n // tn),
        in_specs=[pl.BlockSpec((1, tm, d), lambda bi, i, j: (bi, i, 0)),
                  pl.BlockSpec((1, d), lambda bi, i, j: (0, 0)),
                  pl.BlockSpec((1, 1, d), mod_map),
                  pl.BlockSpec((1, 1, d), mod_map),
                  pl.BlockSpec((d, tn), lambda bi, i, j: (0, j))],
        out_specs=pl.BlockSpec((1, tm, tn), lambda bi, i, j: (bi, i, j)),
        out_shape=jax.ShapeDtypeStruct((b, l, n), BF16),
        scratch_shapes=[pltpu.VMEM((tm, d), BF16)],
        compiler_params=_cparams(("arbitrary", "arbitrary", "arbitrary")),
        name="proj_in",
    )(x, norm_g.reshape(1, d), scale, shift, w)


def _proj_in_flat(x, norm_g, scale, shift, w, n_col_tiles):
    b, l, d = x.shape
    assert scale.shape[0] == 1
    tm = min(b * l, 1024)
    out = _proj_in(x.reshape(1, b * l, d), norm_g, scale, shift, w, tm, n_col_tiles)
    return out.reshape(b, l, w.shape[1])


ATTN_SLABS = 4


def _attn_heads(q, k2, v2, sink_ref, masks, o_ref):
    t = q.shape[0]
    left = lax.broadcasted_iota(jnp.int32, (t, LANES), 1) < HEAD_DIM
    top = lax.broadcasted_iota(jnp.int32, (LANES, t), 0) < HEAD_DIM
    zero = jnp.zeros((t, LANES), q.dtype)
    for j0 in range(0, ATTN_WIDTH // LANES, ATTN_SLABS):
        kv = (2 * j0) // GQA_GROUP
        kk = k2[:, LANES * kv:LANES * (kv + 1)]
        vv = v2[:, LANES * kv:LANES * (kv + 1)]
        blocks = []
        for j in range(j0, j0 + ATTN_SLABS):
            qs = q[:, LANES * j:LANES * (j + 1)]
            blocks += [jnp.where(left, qs, zero), jnp.where(left, zero, qs)]
        qstack = jnp.concatenate(blocks, axis=0)
        s_t = lax.dot_general(kk, qstack, (((1,), (1,)), ((), ())), preferred_element_type=F32)
        ps, dens = [], []
        for bi in range(2 * ATTN_SLABS):
            s = s_t[:, bi * t:(bi + 1) * t]
            pieces, pos = [], 0
            for r0, r1, mk in masks:
                if r0 > pos:
                    pieces.append(s[pos:r0])
                pieces.append(jnp.where(mk, s[r0:r1], NEG))
                pos = r1
            if pieces:
                if pos < s.shape[0]:
                    pieces.append(s[pos:])
                s = jnp.concatenate(pieces, axis=0)
            sink = sink_ref[2 * j0 + bi]
            m = jnp.maximum(jnp.max(s, axis=0, keepdims=True), sink)
            p = jnp.exp(s - m)
            dens.append(jnp.sum(p, axis=0, keepdims=True) + jnp.exp(sink - m))
            ps.append(p.astype(BF16))
        p_t = jnp.concatenate(ps, axis=1)
        o_t = lax.dot_general(vv, p_t, (((0,), (0,)), ((), ())), preferred_element_type=F32)
        o_t = o_t / jnp.concatenate(dens, axis=1)
        for jj in range(ATTN_SLABS):
            comb = jnp.where(top, o_t[:, (2 * jj) * t:(2 * jj + 1) * t], o_t[:, (2 * jj + 1) * t:(2 * jj + 2) * t])
            o_ref[0, :, LANES * (j0 + jj):LANES * (j0 + jj + 1)] = comb.T.astype(o_ref.dtype)


def _attn_ctx_body(sink_ref, q_ref, k_ref, v_ref, o_ref):
    _attn_heads(q_ref[0] * ATTN_SCALE, k_ref[0], v_ref[0], sink_ref, (), o_ref)


def _attn_ctx(u, sink):
    b, t, _ = u.shape
    return pl.pallas_call(
        _attn_ctx_body,
        grid=(b,),
        in_specs=[pl.BlockSpec(memory_space=pltpu.SMEM),
                  pl.BlockSpec((1, t, ATTN_WIDTH), lambda bi: (bi, 0, U_Q // ATTN_WIDTH)),
                  pl.BlockSpec((1, t, 256), lambda bi: (bi, 0, U_K2 // 256)),
                  pl.BlockSpec((1, t, 256), lambda bi: (bi, 0, U_V2 // 256))],
        out_specs=pl.BlockSpec((1, t, ATTN_WIDTH), lambda bi: (bi, 0, 0)),
        out_shape=jax.ShapeDtypeStruct((b, t, ATTN_WIDTH), BF16),
        compiler_params=_cparams(("arbitrary",)),
        name="attn_ctx",
    )(sink, u, u, u)


def _rope_slab(x, cos, sin):
    lane = lax.broadcasted_iota(jnp.int32, x.shape, 1)
    lo = (lane & 16) == 0
    partner = jnp.where(lo, pltpu.roll(x, LANES - 16, 1), pltpu.roll(x, 16, 1))
    return x * cos + partner * sin


def _rope_k_body(k_ref, cos_ref, sin_ref, o_ref):
    cos, sin = cos_ref[...], sin_ref[...]
    for s in range(2):
        x = k_ref[0, :, LANES * s:LANES * (s + 1)].astype(F32)
        o_ref[0, :, LANES * s:LANES * (s + 1)] = _rope_slab(x, cos, sin).astype(o_ref.dtype)


def _rope_k(u, cos, sin, tl):
    b, l, _ = u.shape
    return pl.pallas_call(
        _rope_k_body,
        grid=(b, l // tl),
        in_specs=[pl.BlockSpec((1, tl, 256), lambda bi, i: (bi, i, U_K2 // 256)),
                  pl.BlockSpec((tl, LANES), lambda bi, i: (i, 0)),
                  pl.BlockSpec((tl, LANES), lambda bi, i: (i, 0))],
        out_specs=pl.BlockSpec((1, tl, 256), lambda bi, i: (bi, i, 0)),
        out_shape=jax.ShapeDtypeStruct((b, l, 256), BF16),
        compiler_params=_cparams(("arbitrary", "arbitrary")),
        name="rope_k",
    )(u, cos, sin)


def _attn_lat_body(sink_ref, q_ref, cos_ref, sin_ref, kp_ref, kc_ref, kn_ref, vp_ref, vc_ref, vn_ref,
                   kx_ref, vx_ref, o_ref, q_scr):
    i = pl.program_id(1)
    nblk = pl.num_programs(1)
    cos, sin = cos_ref[...], sin_ref[...]
    for j in range(ATTN_WIDTH // LANES):
        x = q_ref[0, :, LANES * j:LANES * (j + 1)].astype(F32)
        q_scr[:, LANES * j:LANES * (j + 1)] = (_rope_slab(x, cos, sin) * ATTN_SCALE).astype(BF16)
    k2 = jnp.concatenate([kx_ref[0, 0], kp_ref[0], kc_ref[0], kn_ref[0]], axis=0)
    v2 = jnp.concatenate([vx_ref[0, 0], vp_ref[0], vc_ref[0], vn_ref[0]], axis=0)
    tc = kx_ref.shape[2]
    diff = (lax.broadcasted_iota(jnp.int32, (BLOCK, BLOCK), 0)
            - lax.broadcasted_iota(jnp.int32, (BLOCK, BLOCK), 1))
    prev_ok = diff >= jnp.where(i > 0, 0, 2 * BLOCK)
    next_ok = diff <= jnp.where(i < nblk - 1, 0, -2 * BLOCK)
    masks = ((tc, tc + BLOCK, prev_ok), (tc + 2 * BLOCK, tc + 3 * BLOCK, next_ok))
    _attn_heads(q_scr[...], k2, v2, sink_ref, masks, o_ref)


def _attn_lat(u, k2r, kx2, vx2, layer, sink, cos, sin):
    b, l, _ = u.shape
    nblk = l // BLOCK
    tc = kx2.shape[2]
    prev = lambda bi, i: (bi, jnp.maximum(i - 1, 0), 0)
    cur = lambda bi, i: (bi, i, 0)
    nxt = lambda bi, i: (bi, jnp.minimum(i + 1, nblk - 1), 0)
    voff = U_V2 // 256
    vprev = lambda bi, i: (bi, jnp.maximum(i - 1, 0), voff)
    vcur = lambda bi, i: (bi, i, voff)
    vnxt = lambda bi, i: (bi, jnp.minimum(i + 1, nblk - 1), voff)
    return pl.pallas_call(
        _attn_lat_body,
        grid=(b, nblk),
        in_specs=[pl.BlockSpec(memory_space=pltpu.SMEM),
                  pl.BlockSpec((1, BLOCK, ATTN_WIDTH), lambda bi, i: (bi, i, U_Q // ATTN_WIDTH)),
                  pl.BlockSpec((BLOCK, LANES), lambda bi, i: (i, 0)),
                  pl.BlockSpec((BLOCK, LANES), lambda bi, i: (i, 0)),
                  pl.BlockSpec((1, BLOCK, 256), prev),
                  pl.BlockSpec((1, BLOCK, 256), cur),
                  pl.BlockSpec((1, BLOCK, 256), nxt),
                  pl.BlockSpec((1, BLOCK, 256), vprev),
                  pl.BlockSpec((1, BLOCK, 256), vcur),
                  pl.BlockSpec((1, BLOCK, 256), vnxt),
                  pl.BlockSpec((1, 1, tc, 256), lambda bi, i: (bi, layer, 0, 0)),
                  pl.BlockSpec((1, 1, tc, 256), lambda bi, i: (bi, layer, 0, 0))],
        out_specs=pl.BlockSpec((1, BLOCK, ATTN_WIDTH), lambda bi, i: (bi, i, 0)),
        out_shape=jax.ShapeDtypeStruct((b, l, ATTN_WIDTH), BF16),
        scratch_shapes=[pltpu.VMEM((BLOCK, ATTN_WIDTH), BF16)],
        compiler_params=_cparams(("arbitrary", "arbitrary")),
        name="attn_lat",
    )(sink, u, cos, sin, k2r, k2r, k2r, u, u, u, kx2, vx2)


def _rope_tables(l):
    nf = HEAD_DIM // 4
    inv = ROPE_BASE ** (-jnp.arange(nf, dtype=F32) / nf)
    tok = jnp.arange(l)
    row = (tok // GRID_W).astype(F32)
    col = (tok % GRID_W).astype(F32)
    d = np.arange(LANES) % HEAD_DIM
    use_col = (d // (HEAD_DIM // 2)) == 1
    lo = (d % (HEAD_DIM // 2)) < nf
    pos = jnp.where(use_col[None, :], col[:, None], row[:, None])
    ang = pos * inv[d % nf][None, :]
    sign = np.where(lo, -1.0, 1.0).astype(np.float32)
    return jnp.cos(ang), jnp.sin(ang) * sign[None, :]


def _dot_split(a, b):
    ah = a.astype(BF16)
    bh = b.astype(BF16)
    al = (a - ah.astype(F32)).astype(BF16)
    bl = (b - bh.astype(F32)).astype(BF16)
    dot = functools.partial(jnp.dot, preferred_element_type=F32)
    return dot(ah, bh) + (dot(ah, bl) + dot(al, bh))


def _hy_hidden_body(z_ref, w1_ref, b1_ref, w2_ref, b2_ref, o_ref):
    h = jnp.sin(jnp.dot(z_ref[...], w1_ref[...], precision=HIGHEST, preferred_element_type=F32) + b1_ref[...])
    o_ref[...] = jnp.sin(jnp.dot(h, w2_ref[...], precision=HIGHEST, preferred_element_type=F32) + b2_ref[...])


def _hy_hidden(zfeat, w1, b1, w2, b2):
    rows = zfeat.shape[0]
    return pl.pallas_call(
        _hy_hidden_body,
        out_shape=jax.ShapeDtypeStruct((rows, LANES), F32),
        compiler_params=pltpu.CompilerParams(vmem_limit_bytes=VMEM_LIMIT),
        name="hy_hidden",
    )(zfeat, w1, b1, w2, b2)


def _hy_features(l):
    pos = jnp.arange(l, dtype=F32)
    t = pos / l
    freqs = jnp.linspace(1e-4, HY_BANDS - 1, HY_BANDS, dtype=F32)
    ang = 2.0 * jnp.pi * t[:, None] * freqs[None, :]
    z = jnp.concatenate([t[:, None], jnp.cos(ang), -jnp.sin(ang)], axis=-1)
    zrev = jnp.concatenate([z[:1], z[:0:-1]], axis=0)
    zz = jnp.concatenate([z, zrev], axis=0)
    return jnp.pad(zz, ((0, 0), (0, LANES - HY_EMB)))


def _hy_taps(hid_ref, w3f_ref, w3b_ref, decf_ref, decb_ref, l):
    ct = w3f_ref.shape[1]
    hf = _dot_split(hid_ref[0:l, :], w3f_ref[...])
    hb = _dot_split(hid_ref[l:2 * l, :], w3b_ref[...])
    row = lax.broadcasted_iota(jnp.int32, (l, ct), 0)
    t = row.astype(F32) / l
    trev = (l - row).astype(F32) / l
    hf = hf * jnp.exp(-t * jnp.abs(decf_ref[0]))
    hb = jnp.where(row == 0, 0.0, hb * jnp.exp(-trev * jnp.abs(decb_ref[0])))
    nsq = jnp.sum(hf * hf, axis=0, keepdims=True) + jnp.sum(hb * hb, axis=0, keepdims=True) + EPS
    sc = lax.rsqrt(nsq)
    return hf * sc, hb * sc


def _hy_filter_small_body(hid_ref, w3f_ref, w3b_ref, decf_ref, decb_ref, f_ref, o_ref, *, l):
    hf, hb = _hy_taps(hid_ref, w3f_ref, w3b_ref, decf_ref, decb_ref, l)
    taps = jnp.concatenate([hf, hb], axis=0)
    o_ref[0] = jnp.dot(f_ref[...], taps, precision=HIGHEST, preferred_element_type=F32) * (1.0 / (2 * l))


def _hy_filter_specs(ct):
    nct = HY_WIDTH // ct
    return [pl.BlockSpec((LANES, ct), lambda n, j: (0, n * nct + j)),
            pl.BlockSpec((LANES, ct), lambda n, j: (0, (HY_ORDER + n) * nct + j)),
            pl.BlockSpec((1, 1, ct), lambda n, j: (n, 0, j)),
            pl.BlockSpec((1, 1, ct), lambda n, j: (HY_ORDER + n, 0, j))]


def _hy_filter_small(hid, w3, decay, fmat, l):
    ct = 256
    return pl.pallas_call(
        functools.partial(_hy_filter_small_body, l=l),
        grid=(HY_ORDER, HY_WIDTH // ct),
        in_specs=[_const_spec((2 * l, LANES))] + _hy_filter_specs(ct) + [_const_spec(fmat.shape)],
        out_specs=pl.BlockSpec((1, 4 * l, ct), lambda n, j: (n, 0, j)),
        out_shape=jax.ShapeDtypeStruct((HY_ORDER, 4 * l, HY_WIDTH), F32),
        compiler_params=_cparams(("arbitrary", "arbitrary")),
        name="hy_filter_small",
    )(hid, w3, w3, decay, decay, fmat)


def _hy_filter_big_body(hid_ref, w3f_ref, w3b_ref, decf_ref, decb_ref, fa_ref, f2_ref, o_ref, xs, ss, *, l):
    n1 = 2 * l // LANES
    hf, hb = _hy_taps(hid_ref, w3f_ref, w3b_ref, decf_ref, decb_ref, l)
    for s in range(n1 // 2):
        xs[pl.ds(s * X_PITCH, LANES), :] = hf[s * LANES:(s + 1) * LANES]
        xs[pl.ds((n1 // 2 + s) * X_PITCH, LANES), :] = hb[s * LANES:(s + 1) * LANES]

    def stage_a(n2, carry):
        rhs = xs[pl.ds(n2, n1, stride=X_PITCH), :]
        out = _dot_split(fa_ref[n2], rhs)
        ss[pl.ds(n2, n1, stride=S_PITCH), :] = out[:n1]
        ss[pl.ds(LANES + n2, n1, stride=S_PITCH), :] = out[n1:]
        return carry

    lax.fori_loop(0, LANES, stage_a, 0, unroll=FFT_UNROLL // 2)

    side = min(FFT_SIDE, n1)

    def stage_b(kg, carry):
        slabs = [ss[pl.ds(pl.multiple_of((side * kg + e) * S_PITCH, 8), 2 * LANES), :] for e in range(side)]
        out = _dot_split(f2_ref[...], jnp.concatenate(slabs, axis=1)) * (1.0 / (2 * l))
        for e in range(side):
            rows = pl.ds(pl.multiple_of((side * kg + e) * 2 * LANES, 2 * LANES), 2 * LANES)
            o_ref[0, rows, :] = out[:, e * LANES:(e + 1) * LANES]
        return carry

    lax.fori_loop(0, n1 // side, stage_b, 0, unroll=2)


def _hy_filter_big(hid, w3, decay, fa, f2, l):
    n1 = 2 * l // LANES
    return pl.pallas_call(
        functools.partial(_hy_filter_big_body, l=l),
        grid=(HY_ORDER, HY_WIDTH // LANES),
        in_specs=[_const_spec((2 * l, LANES))] + _hy_filter_specs(LANES)
        + [_const_spec(fa.shape), _const_spec(f2.shape)],
        out_specs=pl.BlockSpec((1, 4 * l, LANES), lambda n, j: (n, 0, j)),
        out_shape=jax.ShapeDtypeStruct((HY_ORDER, 4 * l, HY_WIDTH), F32),
        scratch_shapes=[pltpu.VMEM((n1 * X_PITCH, LANES), F32),
                        pltpu.VMEM((n1 * S_PITCH, LANES), F32)],
        compiler_params=_cparams(("arbitrary", "arbitrary")),
        name="hy_filter_big",
    )(hid, w3, w3, decay, decay, fa, f2)


def _shift_rows(x, k):
    l = x.shape[0]
    y = pltpu.roll(x, k % l, 0)
    row = lax.broadcasted_iota(jnp.int32, (8, x.shape[1]), 0)
    if k > 0:
        return jnp.concatenate([jnp.where(row < k, 0.0, y[:8]), y[8:]], axis=0)
    return jnp.concatenate([y[:l - 8], jnp.where(row >= 8 + k, 0.0, y[l - 8:])], axis=0)


def _short_conv(x, w_ref, b_ref):
    prev = _shift_rows(x, 1)
    nxt = _shift_rows(x, -1)
    return prev * w_ref[0:1, :] + x * w_ref[1:2, :] + nxt * w_ref[2:3, :] + b_ref[...]


def _hy_small_body(uz_ref, ux1_ref, ux2_ref, wz_ref, wx1_ref, wx2_ref, bz_ref, bx1_ref, bx2_ref,
                   h_ref, bias_ref, f_ref, g_ref, o_ref):
    l = uz_ref.shape[1]
    n = 2 * l
    sig = [_short_conv(uz_ref[b].astype(F32), wz_ref, bz_ref) for b in range(2)]
    gates = [[_short_conv(ux1_ref[b].astype(F32), wx1_ref, bx1_ref) for b in range(2)],
             [_short_conv(ux2_ref[b].astype(F32), wx2_ref, bx2_ref) for b in range(2)]]
    for o in range(HY_ORDER):
        x = jnp.concatenate(sig, axis=0).astype(BF16)
        a = jnp.dot(f_ref[...], x, preferred_element_type=F32)
        ar, ai = a[:n], a[n:]
        hr, hi = h_ref[o, 0:n, :], h_ref[o, n:2 * n, :]
        y = jnp.concatenate([ar * hr - ai * hi, ar * hi + ai * hr], axis=0).astype(BF16)
        conv = jnp.dot(g_ref[...], y, preferred_element_type=F32)
        bias = bias_ref[o:o + 1, :]
        sig = [gates[o][b] * (conv[b * l:(b + 1) * l] + bias * sig[b]) for b in range(2)]
    for b in range(2):
        o_ref[b] = sig[b].astype(o_ref.dtype)


def _hy_small(u, short_w, short_b, hspec, hy_bias, fmat, gmat):
    b, l, _ = u.shape
    ct = 256
    col = lambda off: (lambda j, p: (p, 0, off // ct + j))
    wcol = lambda off: (lambda j, p: (0, off // ct + j))
    return pl.pallas_call(
        _hy_small_body,
        grid=(HY_WIDTH // ct, b // 2),
        in_specs=[pl.BlockSpec((2, l, ct), col(U_Z)),
                  pl.BlockSpec((2, l, ct), col(U_X1)),
                  pl.BlockSpec((2, l, ct), col(U_X2)),
                  pl.BlockSpec((3, ct), wcol(2 * HY_WIDTH)),
                  pl.BlockSpec((3, ct), wcol(0)),
                  pl.BlockSpec((3, ct), wcol(HY_WIDTH)),
                  pl.BlockSpec((1, ct), wcol(2 * HY_WIDTH)),
                  pl.BlockSpec((1, ct), wcol(0)),
                  pl.BlockSpec((1, ct), wcol(HY_WIDTH)),
                  pl.BlockSpec((HY_ORDER, 4 * l, ct), lambda j, p: (0, 0, j)),
                  pl.BlockSpec((HY_ORDER, ct), lambda j, p: (0, j)),
                  _const_spec(fmat.shape),
                  _const_spec(gmat.shape)],
        out_specs=pl.BlockSpec((2, l, ct), lambda j, p: (p, 0, j)),
        out_shape=jax.ShapeDtypeStruct((b, l, HY_WIDTH), BF16),
        compiler_params=_cparams(("arbitrary", "arbitrary")),
        name="hy_small",
    )(u, u, u, short_w, short_w, short_w, short_b, short_b, short_b, hspec, hy_bias, fmat, gmat)


def _hy_big_body(*refs, conv_sig, l):
    if conv_sig:
        (sig_ref, ws_ref, bs_ref, gate_ref, wg_ref, bg_ref, h_ref, bias_ref,
         fa_ref, ga_ref, f2_ref, f2i_ref, o_ref, xs, ss, ys) = refs
    else:
        (sig_ref, gate_ref, wg_ref, bg_ref, h_ref, bias_ref,
         fa_ref, ga_ref, f2_ref, f2i_ref, o_ref, xs, ss, ys) = refs
    n1 = 2 * l // LANES
    hn = n1 // 2

    for b in range(2):
        x = sig_ref[b].astype(F32)
        if conv_sig:
            x = _short_conv(x, ws_ref, bs_ref)
        for s in range(hn):
            xs[pl.ds((b * hn + s) * X_PITCH, LANES), :] = x[s * LANES:(s + 1) * LANES]

    def stage_a(n2, carry):
        xr = xs[pl.ds(n2, hn, stride=X_PITCH), :]
        xi = xs[pl.ds(hn * X_PITCH + n2, hn, stride=X_PITCH), :]
        rhs = jnp.concatenate([xr, xi], axis=0).astype(BF16)
        out = jnp.dot(fa_ref[n2], rhs, preferred_element_type=F32)
        ss[pl.ds(n2, n1, stride=S_PITCH), :] = out[:n1]
        ss[pl.ds(LANES + n2, n1, stride=S_PITCH), :] = out[n1:]
        return carry

    lax.fori_loop(0, LANES, stage_a, 0, unroll=FFT_UNROLL)

    side = 2
    groups = min(8, n1 // side)

    def stage_b(kg, carry):
        k1s = [[(kg * groups + p) * side + e for e in range(side)] for p in range(groups)]
        slabs = [jnp.concatenate([ss[pl.ds(pl.multiple_of(k1 * S_PITCH, 8), 2 * LANES), :] for k1 in grp],
                                 axis=1).astype(BF16) for grp in k1s]
        backs = []
        for p in range(groups):
            x = jnp.dot(f2_ref[...], slabs[p], preferred_element_type=F32)
            hbs = [pl.multiple_of(k1 * 2 * LANES, 2 * LANES) for k1 in k1s[p]]
            hr = jnp.concatenate([h_ref[0, pl.ds(hb, LANES), :] for hb in hbs], axis=1)
            hi = jnp.concatenate([h_ref[0, pl.ds(hb + LANES, LANES), :] for hb in hbs], axis=1)
            xr, xi = x[:LANES], x[LANES:]
            y = jnp.concatenate([xr * hr - xi * hi, xr * hi + xi * hr], axis=0).astype(BF16)
            backs.append(jnp.dot(f2i_ref[...], y, preferred_element_type=F32))
        for p in range(groups):
            for e in range(side):
                ss[pl.ds(pl.multiple_of(k1s[p][e] * S_PITCH, 8), 2 * LANES), :] = backs[p][:, e * LANES:(e + 1) * LANES]
        return carry

    lax.fori_loop(0, n1 // (side * groups), stage_b, 0)

    def stage_a_inv(n2, carry):
        br = ss[pl.ds(n2, n1, stride=S_PITCH), :]
        bi = ss[pl.ds(LANES + n2, n1, stride=S_PITCH), :]
        rhs = jnp.concatenate([br, bi], axis=0).astype(BF16)
        out = jnp.dot(ga_ref[n2], rhs, preferred_element_type=F32)
        ys[pl.ds(n2, hn, stride=X_PITCH), :] = out[:hn]
        ys[pl.ds(hn * X_PITCH + n2, hn, stride=X_PITCH), :] = out[hn:]
        return carry

    lax.fori_loop(0, LANES, stage_a_inv, 0, unroll=FFT_UNROLL)

    bias = bias_ref[...]
    for b in range(2):
        g = _short_conv(gate_ref[b].astype(F32), wg_ref, bg_ref)
        for s in range(hn):
            rows = pl.ds((b * hn + s) * X_PITCH, LANES)
            val = g[s * LANES:(s + 1) * LANES] * (ys[rows, :] + bias * xs[rows, :])
            o_ref[b, s * LANES:(s + 1) * LANES, :] = val.astype(o_ref.dtype)


def _hy_big(sig, sig_off, sig_w, gate, gate_off, gate_woff, short_w, short_b, hspec, order, hy_bias,
            tables, out_dtype):
    b, l, _ = gate.shape
    n1 = 2 * l // LANES
    fa, ga, f2, f2i = tables
    conv_sig = sig_w is not None
    col = lambda off: (lambda j, p: (p, 0, off // LANES + j))
    wcol = lambda off: (lambda j, p: (0, off // LANES + j))
    in_specs = [pl.BlockSpec((2, l, LANES), col(sig_off))]
    args = [sig]
    if conv_sig:
        in_specs += [pl.BlockSpec((3, LANES), wcol(sig_w)), pl.BlockSpec((1, LANES), wcol(sig_w))]
        args += [short_w, short_b]
    in_specs += [pl.BlockSpec((2, l, LANES), col(gate_off)),
                 pl.BlockSpec((3, LANES), wcol(gate_woff)),
                 pl.BlockSpec((1, LANES), wcol(gate_woff)),
                 pl.BlockSpec((1, 4 * l, LANES), lambda j, p: (order, 0, j), pipeline_mode=pl.Buffered(1)),
                 pl.BlockSpec((1, LANES), lambda j, p: (0, j)),
                 _const_spec(fa.shape), _const_spec(ga.shape), _const_spec(f2.shape), _const_spec(f2i.shape)]
    args += [gate, short_w, short_b, hspec, hy_bias[order][None, :], fa, ga, f2, f2i]
    return pl.pallas_call(
        functools.partial(_hy_big_body, conv_sig=conv_sig, l=l),
        grid=(HY_WIDTH // LANES, b // 2),
        in_specs=in_specs,
        out_specs=pl.BlockSpec((2, l, LANES), lambda j, p: (p, 0, j)),
        out_shape=jax.ShapeDtypeStruct((b, l, HY_WIDTH), out_dtype),
        scratch_shapes=[pltpu.VMEM((n1 * X_PITCH, LANES), F32),
                        pltpu.VMEM((n1 * S_PITCH, LANES), F32),
                        pltpu.VMEM((n1 * X_PITCH, LANES), F32)],
        compiler_params=_cparams(("arbitrary", "arbitrary")),
        name="hy_big",
    )(*args)


def _cis(idx, n):
    ang = idx.astype(F32) * (2.0 * np.pi / n)
    return jnp.cos(ang), jnp.sin(ang)


def _stack_complex(cr, ci):
    return jnp.concatenate([jnp.concatenate([cr, -ci], axis=-1), jnp.concatenate([ci, cr], axis=-1)], axis=-2)


def _dense_dft_tables(l):
    n = 2 * l
    k = jnp.arange(n, dtype=jnp.int32)
    t = jnp.arange(l, dtype=jnp.int32)
    cr, ci = _cis((k[:, None] * t[None, :]) % n, n)
    fwd = _stack_complex(cr, -ci)
    inv = _stack_complex(cr.T, ci.T)
    cfr, cfi = _cis((k[:, None] * k[None, :]) % n, n)
    filt = jnp.concatenate([cfr, -cfi], axis=0)
    return fwd.astype(BF16), inv.astype(BF16), filt


def _two_stage_dft_tables(l):
    n = 2 * l
    n1 = n // LANES
    hn = n1 // 2
    n2 = jnp.arange(LANES, dtype=jnp.int32)[:, None, None]
    k1 = jnp.arange(n1, dtype=jnp.int32)[None, :, None]
    m1 = jnp.arange(n1, dtype=jnp.int32)[None, None, :]
    cr, ci = _cis((k1 * (LANES * m1 + n2)) % n, n)
    fa = _stack_complex(cr[:, :, :hn], -ci[:, :, :hn])
    ga = _stack_complex(jnp.swapaxes(cr, 1, 2)[:, :hn, :], jnp.swapaxes(ci, 1, 2)[:, :hn, :])
    fa_filt = jnp.concatenate([cr, -ci], axis=1)
    a = jnp.arange(LANES, dtype=jnp.int32)
    c2r, c2i = _cis((a[:, None] * a[None, :]) % LANES, LANES)
    f2 = _stack_complex(c2r, -c2i)
    f2i = _stack_complex(c2r, c2i)
    return (fa.astype(BF16), ga.astype(BF16), f2.astype(BF16), f2i.astype(BF16)), fa_filt, f2


def _silu(x):
    return x * jax.nn.sigmoid(x)


def _proj_out_even_body(att_ref, ga_ref, hy_ref, gh_ref, w_ref, x_ref, g_ref, o_ref):
    a = att_ref[0].astype(F32) * _silu(ga_ref[0].astype(F32))
    h = hy_ref[0].astype(F32) * _silu(gh_ref[0].astype(F32))
    mixed = jnp.concatenate([a, h], axis=-1).astype(BF16)
    res = jnp.dot(mixed, w_ref[...], preferred_element_type=F32)
    o_ref[0] = x_ref[0] + g_ref[0] * res


def _proj_out_even(att, hy, u, w_out, x, gate, tm):
    b, l, d = x.shape
    per_batch = gate.shape[0] == b
    gmap = (lambda bi, i: (bi, 0, 0)) if per_batch else (lambda bi, i: (0, 0, 0))
    return pl.pallas_call(
        _proj_out_even_body,
        grid=(b, l // tm),
        in_specs=[pl.BlockSpec((1, tm, ATTN_WIDTH), lambda bi, i: (bi, i, 0)),
                  pl.BlockSpec((1, tm, ATTN_WIDTH), lambda bi, i: (bi, i, U_GA // ATTN_WIDTH)),
                  pl.BlockSpec((1, tm, HY_WIDTH), lambda bi, i: (bi, i, 0)),
                  pl.BlockSpec((1, tm, HY_WIDTH), lambda bi, i: (bi, i, U_GH // HY_WIDTH)),
                  _const_spec(w_out.shape),
                  pl.BlockSpec((1, tm, d), lambda bi, i: (bi, i, 0)),
                  pl.BlockSpec((1, 1, d), gmap)],
        out_specs=pl.BlockSpec((1, tm, d), lambda bi, i: (bi, i, 0)),
        out_shape=jax.ShapeDtypeStruct((b, l, d), F32),
        compiler_params=_cparams(("arbitrary", "arbitrary")),
        name="proj_out_even",
    )(att, u, hy, u, w_out, x, gate)


def _proj_out_odd_body(yf_ref, yb_ref, gt_ref, w_ref, x_ref, g_ref, fg_ref, o_ref, *, final_norm):
    y = yf_ref[0].astype(F32) + yb_ref[0].astype(F32)
    mixed = (y * _silu(gt_ref[0].astype(F32))).astype(BF16)
    res = jnp.dot(mixed, w_ref[...], preferred_element_type=F32)
    xn = x_ref[0] + g_ref[0] * res
    if final_norm:
        ms = jnp.mean(xn * xn, axis=-1, keepdims=True)
        xn = xn * lax.rsqrt(ms + EPS) * fg_ref[...]
    o_ref[0] = xn


def _proj_out_odd(yf, yb, u2, w_out, x, gate, final_g, final_norm, tm):
    b, l, d = x.shape
    per_batch = gate.shape[0] == b
    gmap = (lambda bi, i: (bi, 0, 0)) if per_batch else (lambda bi, i: (0, 0, 0))
    return pl.pallas_call(
        functools.partial(_proj_out_odd_body, final_norm=final_norm),
        grid=(b, l // tm),
        in_specs=[pl.BlockSpec((1, tm, d), lambda bi, i: (bi, i, 0)),
                  pl.BlockSpec((1, tm, d), lambda bi, i: (bi, i, 0)),
                  pl.BlockSpec((1, tm, d), lambda bi, i: (bi, i, 1)),
                  _const_spec(w_out.shape),
                  pl.BlockSpec((1, tm, d), lambda bi, i: (bi, i, 0)),
                  pl.BlockSpec((1, 1, d), gmap),
                  pl.BlockSpec((1, d), lambda bi, i: (0, 0))],
        out_specs=pl.BlockSpec((1, tm, d), lambda bi, i: (bi, i, 0)),
        out_shape=jax.ShapeDtypeStruct((b, l, d), F32),
        compiler_params=_cparams(("arbitrary", "arbitrary")),
        name="proj_out_odd",
    )(yf, yb, u2, w_out, x, gate, final_g.reshape(1, d))


def _rmsnorm_body(x_ref, g_ref, o_ref):
    x = x_ref[0]
    ms = jnp.mean(x * x, axis=-1, keepdims=True)
    o_ref[0] = x * lax.rsqrt(ms + EPS) * g_ref[...]


def _rmsnorm(x, g, tm):
    b, l, d = x.shape
    return pl.pallas_call(
        _rmsnorm_body,
        grid=(b, l // tm),
        in_specs=[pl.BlockSpec((1, tm, d), lambda bi, i: (bi, i, 0)),
                  pl.BlockSpec((1, d), lambda bi, i: (0, 0))],
        out_specs=pl.BlockSpec((1, tm, d), lambda bi, i: (bi, i, 0)),
        out_shape=jax.ShapeDtypeStruct((b, l, d), F32),
        compiler_params=_cparams(("arbitrary", "arbitrary")),
        name="final_rmsnorm",
    )(x, g.reshape(1, d))


LRU_GROUP = 8


LRU_HALO = 2 * LRU_GROUP


def _lru_gates(d, x_ref, xp_ref, xn_ref, keep_prev, keep_next, cw_ref, cb_ref, wa_ref, ba_ref, wx_ref, bx_ref,
               lam_ref, x_scr, a_scr, b_scr):
    tt = x_ref.shape[1]
    g = LRU_GROUP
    nk = LRU_BLOCK // LANES
    for s in range(g):
        for k in range(nk):
            lanes = slice(k * LANES, (k + 1) * LANES)
            x_scr[d, k, pl.ds(LRU_HALO + s, tt, stride=g), :] = x_ref[s, :, lanes].astype(F32)
            x_scr[d, k, pl.ds(s, 1), :] = xp_ref[s, 6:7, lanes].astype(F32) * keep_prev
            x_scr[d, k, pl.ds(g + s, 1), :] = xp_ref[s, 7:8, lanes].astype(F32) * keep_prev
            x_scr[d, k, pl.ds(LRU_HALO + tt * g + s, 1), :] = xn_ref[s, 0:1, lanes].astype(F32) * keep_next
    xcs = []
    for k in range(nk):
        lanes = slice(k * LANES, (k + 1) * LANES)
        acc = cb_ref[:, lanes]
        for j in range(4):
            acc = acc + x_scr[d, k, pl.ds(j * g, tt * g), :] * cw_ref[j:j + 1, lanes]
        xcs.append(acc)
    xc = jnp.concatenate(xcs, axis=1)
    xcb = xc.astype(BF16)
    lam = lam_ref[d:d + 1, :]
    softplus_neg = jnp.maximum(-lam, 0.0) + jnp.log1p(jnp.exp(-jnp.abs(lam)))
    tr = jnp.tanh(jnp.dot(xcb, wa_ref[d, 0], preferred_element_type=F32) + ba_ref[d:d + 1, :])
    ti = jnp.tanh(jnp.dot(xcb, wx_ref[d, 0], preferred_element_type=F32) + bx_ref[d:d + 1, :])
    half_c = (-0.5 * LRU_C) * softplus_neg
    log_a = half_c * tr + half_c
    a = jnp.exp(log_a)
    w4 = jnp.tanh(log_a) * (-0.25 * (a * a) - 0.25)
    root = jnp.where(w4 > 0.0, w4 * lax.rsqrt(w4), 0.0)
    b = root * (ti + 1.0) * xc
    for k in range(nk):
        a_scr[d, k] = a[:, k * LANES:(k + 1) * LANES]
        b_scr[d, k] = b[:, k * LANES:(k + 1) * LANES]


def _lru_body(xf_ref, xfp_ref, xfn_ref, xb_ref, xbp_ref, xbn_ref, cw_ref, cb_ref, wa_ref, ba_ref, wx_ref, bx_ref,
              lam_ref, h0_ref, yf_ref, yb_ref, fin_ref, x_scr, a_scr, b_scr, y_scr, h_scr):
    i = pl.program_id(2)
    nt = pl.num_programs(2)
    tt = xf_ref.shape[1]
    nk = LRU_BLOCK // LANES

    @pl.when(i == 0)
    def _():
        h_scr[...] = h0_ref[...]

    first = jnp.where(i == 0, 0.0, 1.0)
    last = jnp.where(i == nt - 1, 0.0, 1.0)
    params = (cw_ref, cb_ref, wa_ref, ba_ref, wx_ref, bx_ref, lam_ref, x_scr, a_scr, b_scr)
    _lru_gates(0, xf_ref, xfp_ref, xfn_ref, first, last, *params)
    _lru_gates(1, xb_ref, xbp_ref, xbn_ref, last, first, *params)

    def step(t, carry):
        rf = pl.ds(pl.multiple_of(t * LRU_GROUP, LRU_GROUP), LRU_GROUP)
        rb = pl.ds(pl.multiple_of((tt - 1 - t) * LRU_GROUP, LRU_GROUP), LRU_GROUP)
        out = []
        for d, rows in ((0, rf), (1, rb)):
            for k in range(nk):
                h = a_scr[d, k, rows, :] * carry[d * nk + k] + b_scr[d, k, rows, :]
                y_scr[d, k, rows, :] = h
                out.append(h)
        return tuple(out)

    init = tuple(h_scr[d, :, k * LANES:(k + 1) * LANES] for d in range(2) for k in range(nk))
    fin = lax.fori_loop(0, tt, step, init, unroll=16)
    for d in range(2):
        for k in range(nk):
            h_scr[d, :, k * LANES:(k + 1) * LANES] = fin[d * nk + k]

    for s in range(LRU_GROUP):
        for k in range(nk):
            rows = pl.ds(s, tt, stride=LRU_GROUP)
            yf_ref[s, :, k * LANES:(k + 1) * LANES] = y_scr[0, k, rows, :].astype(yf_ref.dtype)
            yb_ref[s, :, k * LANES:(k + 1) * LANES] = y_scr[1, k, rows, :].astype(yb_ref.dtype)

    @pl.when(i == nt - 1)
    def _():
        fin_ref[...] = h_scr[...]


def _lru(u2, conv_w, conv_b, wa, ba, wx, bx, lam, h0):
    b, l, _ = u2.shape
    w = N_HEADS_C * LRU_BLOCK
    hb = LRU_BLOCK
    g = LRU_GROUP
    assert b % g == 0, "the scan packs LRU_GROUP sequences per vreg"
    tt = min(l, 256)
    nt = l // tt
    hr = tt // 8
    nhalo = l // 8
    vec = lambda rows: pl.BlockSpec((rows, hb), lambda h, gi, i: (0, h))
    wspec = pl.BlockSpec((2, 1, hb, hb), lambda h, gi, i: (0, h, 0, 0))
    tile = lambda f: pl.BlockSpec((g, tt, hb), lambda h, gi, i: (gi, f(i), h))
    halo_prev = lambda f: pl.BlockSpec((g, 8, hb), lambda h, gi, i: (gi, jnp.maximum(f(i) * hr - 1, 0), h))
    halo_next = lambda f: pl.BlockSpec((g, 8, hb), lambda h, gi, i: (gi, jnp.minimum((f(i) + 1) * hr, nhalo - 1), h))
    fwd = lambda i: i
    bwd = lambda i: nt - 1 - i
    state = pl.BlockSpec((2, g, hb), lambda h, gi, i: (0, gi, h))
    return pl.pallas_call(
        _lru_body,
        grid=(N_HEADS_C, b // g, nt),
        in_specs=[tile(fwd), halo_prev(fwd), halo_next(fwd), tile(bwd), halo_prev(bwd), halo_next(bwd),
                  vec(4), vec(1), wspec, vec(2), wspec, vec(2), vec(2), state],
        out_specs=[tile(fwd), tile(bwd), state],
        out_shape=[jax.ShapeDtypeStruct((b, l, w), BF16),
                   jax.ShapeDtypeStruct((b, l, w), BF16),
                   jax.ShapeDtypeStruct((2, b, w), F32)],
        scratch_shapes=[pltpu.VMEM((2, hb // LANES, (tt + 3) * g, LANES), F32),
                        pltpu.VMEM((2, hb // LANES, tt * g, LANES), F32),
                        pltpu.VMEM((2, hb // LANES, tt * g, LANES), F32),
                        pltpu.VMEM((2, hb // LANES, tt * g, LANES), F32),
                        pltpu.VMEM((2, g, hb), F32)],
        compiler_params=_cparams(("arbitrary", "arbitrary", "arbitrary")),
        name="rg_lru",
    )(u2, u2, u2, u2, u2, u2, conv_w, conv_b.reshape(1, w), wa, ba, wx, bx, lam, h0)


def _even_w_in(w):
    q = w[:, 0:ATTN_WIDTH]
    k = w[:, ATTN_WIDTH:ATTN_WIDTH + KV_COLS]
    v = w[:, ATTN_WIDTH + KV_COLS:ATTN_WIDTH + 2 * KV_COLS]
    o = ATTN_WIDTH + 2 * KV_COLS
    ga = w[:, o:o + ATTN_WIDTH]
    hy = w[:, o + ATTN_WIDTH:o + ATTN_WIDTH + 3 * HY_WIDTH]
    gh = w[:, o + ATTN_WIDTH + 3 * HY_WIDTH:]
    dup = lambda m: jnp.concatenate([m[:, h * HEAD_DIM:(h + 1) * HEAD_DIM] for h in range(N_KV_A) for _ in range(2)],
                                    axis=1)
    return jnp.concatenate([q, ga, hy, gh, dup(k), dup(v)], axis=1).astype(BF16)


def _dup_heads(x):
    return jnp.concatenate([x[..., h, :] for h in range(N_KV_A) for _ in range(2)], axis=-1).astype(BF16)


def kernel(x_prompt, x_sample, cache_k, cache_v, state_lru, c, c_ctx, mod_w, mod_b, norm_g, final_norm_g, a_w_in, a_w_out, a_sink, hy_short_w, hy_short_b, hy_w1, hy_b1, hy_w2, hy_b2, hy_w3, hy_decay, hy_bias, c_w_in, c_w_out, c_conv_w, c_conv_b, c_wa, c_ba, c_wx, c_bx, c_lambda):
    depth, d, _ = mod_w.shape
    bp, lp, _ = x_prompt.shape
    bs, ls, _ = x_sample.shape

    cvec = jnp.concatenate([c, c_ctx[None, :], jnp.zeros((16 - bs - 1, d), F32)], axis=0)
    mod = _modulation(cvec, mod_w, mod_b)

    cos, sin = _rope_tables(ls)
    kx2 = _dup_heads(cache_k)
    vx2 = _dup_heads(cache_v)
    tabs_big, fa_filt, f2_filt = _two_stage_dft_tables(ls)
    fwd_small, inv_small, filt_small = _dense_dft_tables(lp)
    feat_p, feat_s = _hy_features(lp), _hy_features(ls)
    hid_pad = ((0, LANES - hy_w1.shape[-1]),)

    tm_s = min(ls, 1024)
    xp, xs = x_prompt, x_sample
    new_k, new_v, new_s = [], [], []
    for layer in range(depth):
        j = layer // 2
        sh_s, sc_s, g_s = (mod[layer, :bs, None, i * d:(i + 1) * d] for i in range(3))
        sh_p, sc_p, g_p = (mod[layer, bs:bs + 1, None, i * d:(i + 1) * d] for i in range(3))
        last = layer == depth - 1
        if layer % 2 == 0:
            w_in = _even_w_in(a_w_in[j])
            w_out = a_w_out[j].astype(BF16)
            up = _proj_in_flat(xp, norm_g[layer], sc_p, sh_p, w_in, 13)
            us = _proj_in(xs, norm_g[layer], sc_s, sh_s, w_in, min(ls, 512), 2)
            new_k.append(jnp.stack([up[:, :, U_K2 + 2 * h * HEAD_DIM:U_K2 + (2 * h + 1) * HEAD_DIM]
                                    for h in range(N_KV_A)], axis=2).astype(F32))
            new_v.append(jnp.stack([up[:, :, U_V2 + 2 * h * HEAD_DIM:U_V2 + (2 * h + 1) * HEAD_DIM]
                                    for h in range(N_KV_A)], axis=2).astype(F32))
            att_p = _attn_ctx(up, a_sink[j])
            k2r = _rope_k(us, cos, sin, min(ls, 512))
            att_s = _attn_lat(us, k2r, kx2, vx2, j, a_sink[j], cos, sin)

            w1 = jnp.pad(hy_w1[j], ((0, LANES - HY_EMB),) + hid_pad)
            b1 = jnp.pad(hy_b1[j][None, :], ((0, 0),) + hid_pad)
            w2 = jnp.pad(hy_w2[j], hid_pad + hid_pad)
            b2 = jnp.pad(hy_b2[j][None, :], ((0, 0),) + hid_pad)
            w3 = jnp.pad(hy_w3[j], hid_pad + ((0, 0),))
            decay = hy_decay[j].reshape(2 * HY_ORDER, 1, HY_WIDTH)
            sw, sb = hy_short_w[j], hy_short_b[j][None, :]
            hspec_p = _hy_filter_small(_hy_hidden(feat_p, w1, b1, w2, b2), w3, decay, filt_small, lp)
            hspec_s = _hy_filter_big(_hy_hidden(feat_s, w1, b1, w2, b2), w3, decay, fa_filt, f2_filt, ls)
            hy_p = _hy_small(up, sw, sb, hspec_p, hy_bias[j], fwd_small, inv_small)
            z1 = _hy_big(us, U_Z, 2 * HY_WIDTH, us, U_X1, 0, sw, sb, hspec_s, 0, hy_bias[j], tabs_big, F32)
            hy_s = _hy_big(z1, 0, None, us, U_X2, HY_WIDTH, sw, sb, hspec_s, 1, hy_bias[j], tabs_big, BF16)

            xp = _proj_out_even(att_p, hy_p, up, w_out, xp, g_p, lp)
            xs = _proj_out_even(att_s, hy_s, us, w_out, xs, g_s, min(ls, 512))
            if last:
                xp = _rmsnorm(xp, final_norm_g, lp)
                xs = _rmsnorm(xs, final_norm_g, min(ls, 512))
        else:
            w_in = c_w_in[j].astype(BF16)
            w_out = c_w_out[j].astype(BF16)
            wa = (0.5 * c_wa[j]).astype(BF16)
            wx = (0.5 * c_wx[j]).astype(BF16)
            up = _proj_in_flat(xp, norm_g[layer], sc_p, sh_p, w_in, 2)
            us = _proj_in(xs, norm_g[layer], sc_s, sh_s, w_in, tm_s, 2)
            lru_args = (c_conv_w[j], c_conv_b[j], wa, 0.5 * c_ba[j], wx, 0.5 * c_bx[j], c_lambda[j])
            yf_p, yb_p, fin_p = _lru(up, *lru_args, jnp.zeros((2, bp, d), F32))
            yf_s, yb_s, _ = _lru(us, *lru_args, jnp.moveaxis(state_lru[:, j], 1, 0))
            new_s.append(jnp.moveaxis(fin_p, 0, 1))
            xp = _proj_out_odd(yf_p, yb_p, up, w_out, xp, g_p, final_norm_g, last, lp)
            xs = _proj_out_odd(yf_s, yb_s, us, w_out, xs, g_s, final_norm_g, last, min(ls, 512))

    return (xp, xs, jnp.stack(new_k, axis=1), jnp.stack(new_v, axis=1), jnp.stack(new_s, axis=1))
```

```python
import functools

import numpy as np
import jax
import jax.numpy as jnp
from jax import lax
from jax.experimental import pallas as pl
from jax.experimental.pallas import tpu as pltpu

F32 = jnp.float32
BF16 = jnp.bfloat16
HIGHEST = lax.Precision.HIGHEST

EPS = 1e-6
HEAD_DIM = 64
N_HEADS_A = 16
N_KV_A = 2
GQA_GROUP = N_HEADS_A // N_KV_A
ATTN_WIDTH = N_HEADS_A * HEAD_DIM
KV_COLS = N_KV_A * HEAD_DIM
WINDOW = 128
BLOCK = 128
GRID_W = 64
ROPE_BASE = 10000.0
HY_WIDTH = 1024
HY_ORDER = 2
HY_BANDS = 16
HY_EMB = 2 * HY_BANDS + 1
N_HEADS_C = 8
LRU_BLOCK = 256
LRU_C = 8.0
ATTN_SCALE = HEAD_DIM ** -0.5
NEG = float(np.finfo(np.float32).min)

LANES = 128
X_PITCH = LANES + 8
S_PITCH = 2 * LANES + 8
FFT_SIDE = 4
FFT_UNROLL = 128
FFT_GROUPS = 16

U_Q, U_GA, U_X1, U_X2, U_Z, U_GH, U_K2, U_V2 = 0, 1024, 2048, 3072, 4096, 5120, 6144, 6400
U_COLS = 6656

VMEM_LIMIT = 56 * 1024 * 1024

ROWS_WIDE_COLS = 512
ROWS_NARROW_COLS = 1024
ROWS_OUT = 512
ROWS_ROPE = 2048
LRU_TIME = 256


def _cparams(sem, vmem=VMEM_LIMIT):
    return pltpu.CompilerParams(dimension_semantics=sem, vmem_limit_bytes=vmem)


def _const_spec(shape):
    nd = len(shape)
    return pl.BlockSpec(shape, lambda *_: (0,) * nd, pipeline_mode=pl.Buffered(1))


def _mod_body(c_ref, w_ref, b_ref, o_ref):
    c = c_ref[...]
    s = c * jax.nn.sigmoid(c)
    o_ref[0] = jnp.dot(s.astype(BF16), w_ref[0].astype(BF16), preferred_element_type=F32) + b_ref[0]


def _modulation(cvec, mod_w, mod_b):
    depth, d, n3 = mod_w.shape
    rows = cvec.shape[0]
    tn = 512
    return pl.pallas_call(
        _mod_body,
        grid=(depth, n3 // tn),
        in_specs=[pl.BlockSpec((rows, d), lambda l, j: (0, 0)),
                  pl.BlockSpec((1, d, tn), lambda l, j: (l, 0, j)),
                  pl.BlockSpec((1, 1, tn), lambda l, j: (l, 0, j))],
        out_specs=pl.BlockSpec((1, rows, tn), lambda l, j: (l, 0, j)),
        out_shape=jax.ShapeDtypeStruct((depth, rows, n3), F32),
        compiler_params=_cparams(("arbitrary", "arbitrary")),
        name="modulation",
    )(cvec, mod_w, mod_b.reshape(depth, 1, n3))


NORM_ROWS = 16
DOT_COLS = 512


def _col_chunks(tn):
    edges = list(range(0, tn, DOT_COLS)) + [tn]
    if len(edges) > 2 and edges[-1] - edges[-2] < 256:
        del edges[-2]
    return list(zip(edges[:-1], edges[1:]))


def _proj_in_body(x_ref, g_ref, sc_ref, sh_ref, w_ref, o_ref, h_scr):
    @pl.when(pl.program_id(2) == 0)
    def _():
        gain = g_ref[...]
        scale1 = 1.0 + sc_ref[0]
        shift = sh_ref[0]

        def rows(r, carry):
            sl = pl.ds(pl.multiple_of(r * NORM_ROWS, NORM_ROWS), NORM_ROWS)
            x = x_ref[0, sl, :]
            ms = jnp.mean(x * x, axis=-1, keepdims=True)
            y = x * lax.rsqrt(ms + EPS) * gain
            h_scr[sl, :] = (y * scale1 + shift).astype(BF16)
            return carry

        lax.fori_loop(0, x_ref.shape[1] // NORM_ROWS, rows, 0, unroll=8)

    for c0, c1 in _col_chunks(w_ref.shape[1]):
        o_ref[0, :, c0:c1] = jnp.dot(h_scr[...], w_ref[:, c0:c1], preferred_element_type=F32).astype(o_ref.dtype)


def _proj_in(x, norm_g, scale, shift, w, tm, n_col_tiles):
    b, l, d = x.shape
    n = w.shape[1]
    tn = n // n_col_tiles
    assert tn * n_col_tiles == n and tn % LANES == 0
    per_batch = scale.shape[0] == b
    mod_map = (lambda bi, i, j: (bi, 0, 0)) if per_batch else (lambda bi, i, j: (0, 0, 0))
    return pl.pallas_call(
        _proj_in_body,
        grid=(b, l // tm, n // tn),
        in_specs=[pl.BlockSpec((1, tm, d), lambda bi, i, j: (bi, i, 0)),
                  pl.BlockSpec((1, d), lambda bi, i, j: (0, 0)),
                  pl.BlockSpec((1, 1, d), mod_map),
                  pl.BlockSpec((1, 1, d), mod_map),
                  pl.BlockSpec((d, tn), lambda bi, i, j: (0, j))],
        out_specs=pl.BlockSpec((1, tm, tn), lambda bi, i, j: (bi, i, j)),
        out_shape=jax.ShapeDtypeStruct((b, l, n), BF16),
        scratch_shapes=[pltpu.VMEM((tm, d), BF16)],
        compiler_params=_cparams(("arbitrary", "arbitrary", "arbitrary")),
        name="proj_in",
    )(x, norm_g.reshape(1, d), scale, shift, w)


def _proj_in_flat(x, norm_g, scale, shift, w, rows, n_col_tiles):
    b, l, d = x.shape
    assert scale.shape[0] == 1
    tm = min(b * l, rows)
    out = _proj_in(x.reshape(1, b * l, d), norm_g, scale, shift, w, tm, n_col_tiles)
    return out.reshape(b, l, w.shape[1])


ATTN_SLABS = 4


def _attn_heads(q, k2, v2, sink_ref, masks, o_ref):
    t = q.shape[0]
    left = lax.broadcasted_iota(jnp.int32, (t, LANES), 1) < HEAD_DIM
    top = lax.broadcasted_iota(jnp.int32, (LANES, t), 0) < HEAD_DIM
    zero = jnp.zeros((t, LANES), q.dtype)
    for j0 in range(0, ATTN_WIDTH // LANES, ATTN_SLABS):
        kv = (2 * j0) // GQA_GROUP
        kk = k2[:, LANES * kv:LANES * (kv + 1)]
        vv = v2[:, LANES * kv:LANES * (kv + 1)]
        blocks = []
        for j in range(j0, j0 + ATTN_SLABS):
            qs = q[:, LANES * j:LANES * (j + 1)]
            blocks += [jnp.where(left, qs, zero), jnp.where(left, zero, qs)]
        qstack = jnp.concatenate(blocks, axis=0)
        s_t = lax.dot_general(kk, qstack, (((1,), (1,)), ((), ())), preferred_element_type=F32)
        ps, dens = [], []
        for bi in range(2 * ATTN_SLABS):
            s = s_t[:, bi * t:(bi + 1) * t]
            pieces, pos = [], 0
            for r0, r1, mk in masks:
                if r0 > pos:
                    pieces.append(s[pos:r0])
                pieces.append(jnp.where(mk, s[r0:r1], NEG))
                pos = r1
            if pieces:
                if pos < s.shape[0]:
                    pieces.append(s[pos:])
                s = jnp.concatenate(pieces, axis=0)
            sink = sink_ref[2 * j0 + bi]
            m = jnp.maximum(jnp.max(s, axis=0, keepdims=True), sink)
            p = jnp.exp(s - m)
            dens.append(jnp.sum(p, axis=0, keepdims=True) + jnp.exp(sink - m))
            ps.append(p.astype(BF16))
        p_t = jnp.concatenate(ps, axis=1)
        o_t = lax.dot_general(vv, p_t, (((0,), (0,)), ((), ())), preferred_element_type=F32)
        o_t = o_t / jnp.concatenate(dens, axis=1)
        for jj in range(ATTN_SLABS):
            comb = jnp.where(top, o_t[:, (2 * jj) * t:(2 * jj + 1) * t], o_t[:, (2 * jj + 1) * t:(2 * jj + 2) * t])
            o_ref[0, :, LANES * (j0 + jj):LANES * (j0 + jj + 1)] = comb.T.astype(o_ref.dtype)


def _attn_ctx_body(sink_ref, q_ref, k_ref, v_ref, o_ref):
    _attn_heads(q_ref[0] * ATTN_SCALE, k_ref[0], v_ref[0], sink_ref, (), o_ref)


def _attn_ctx(u, sink):
    b, t, _ = u.shape
    return pl.pallas_call(
        _attn_ctx_body,
        grid=(b,),
        in_specs=[pl.BlockSpec(memory_space=pltpu.SMEM),
                  pl.BlockSpec((1, t, ATTN_WIDTH), lambda bi: (bi, 0, U_Q // ATTN_WIDTH)),
                  pl.BlockSpec((1, t, 256), lambda bi: (bi, 0, U_K2 // 256)),
                  pl.BlockSpec((1, t, 256), lambda bi: (bi, 0, U_V2 // 256))],
        out_specs=pl.BlockSpec((1, t, ATTN_WIDTH), lambda bi: (bi, 0, 0)),
        out_shape=jax.ShapeDtypeStruct((b, t, ATTN_WIDTH), BF16),
        compiler_params=_cparams(("arbitrary",)),
        name="attn_ctx",
    )(sink, u, u, u)


def _rope_slab(x, cos, sin):
    lane = lax.broadcasted_iota(jnp.int32, x.shape, 1)
    lo = (lane & 16) == 0
    partner = jnp.where(lo, pltpu.roll(x, LANES - 16, 1), pltpu.roll(x, 16, 1))
    return x * cos + partner * sin


def _rope_k_body(k_ref, cos_ref, sin_ref, o_ref):
    cos, sin = cos_ref[...], sin_ref[...]
    for s in range(2):
        x = k_ref[0, :, LANES * s:LANES * (s + 1)].astype(F32)
        o_ref[0, :, LANES * s:LANES * (s + 1)] = _rope_slab(x, cos, sin).astype(o_ref.dtype)


def _rope_k(u, cos, sin, tl):
    b, l, _ = u.shape
    return pl.pallas_call(
        _rope_k_body,
        grid=(b, l // tl),
        in_specs=[pl.BlockSpec((1, tl, 256), lambda bi, i: (bi, i, U_K2 // 256)),
                  pl.BlockSpec((tl, LANES), lambda bi, i: (i, 0)),
                  pl.BlockSpec((tl, LANES), lambda bi, i: (i, 0))],
        out_specs=pl.BlockSpec((1, tl, 256), lambda bi, i: (bi, i, 0)),
        out_shape=jax.ShapeDtypeStruct((b, l, 256), BF16),
        compiler_params=_cparams(("arbitrary", "arbitrary")),
        name="rope_k",
    )(u, cos, sin)


def _attn_lat_body(sink_ref, q_ref, cos_ref, sin_ref, kp_ref, kc_ref, kn_ref, vp_ref, vc_ref, vn_ref,
                   kx_ref, vx_ref, o_ref, q_scr):
    i = pl.program_id(1)
    nblk = pl.num_programs(1)
    cos, sin = cos_ref[...], sin_ref[...]
    for j in range(ATTN_WIDTH // LANES):
        x = q_ref[0, :, LANES * j:LANES * (j + 1)].astype(F32)
        q_scr[:, LANES * j:LANES * (j + 1)] = (_rope_slab(x, cos, sin) * ATTN_SCALE).astype(BF16)
    k2 = jnp.concatenate([kx_ref[0, 0], kp_ref[0], kc_ref[0], kn_ref[0]], axis=0)
    v2 = jnp.concatenate([vx_ref[0, 0], vp_ref[0], vc_ref[0], vn_ref[0]], axis=0)
    tc = kx_ref.shape[2]
    diff = (lax.broadcasted_iota(jnp.int32, (BLOCK, BLOCK), 0)
            - lax.broadcasted_iota(jnp.int32, (BLOCK, BLOCK), 1))
    prev_ok = diff >= jnp.where(i > 0, 0, 2 * BLOCK)
    next_ok = diff <= jnp.where(i < nblk - 1, 0, -2 * BLOCK)
    masks = ((tc, tc + BLOCK, prev_ok), (tc + 2 * BLOCK, tc + 3 * BLOCK, next_ok))
    _attn_heads(q_scr[...], k2, v2, sink_ref, masks, o_ref)


def _attn_lat(u, k2r, kx2, vx2, layer, sink, cos, sin):
    b, l, _ = u.shape
    nblk = l // BLOCK
    tc = kx2.shape[2]
    prev = lambda bi, i: (bi, jnp.maximum(i - 1, 0), 0)
    cur = lambda bi, i: (bi, i, 0)
    nxt = lambda bi, i: (bi, jnp.minimum(i + 1, nblk - 1), 0)
    voff = U_V2 // 256
    vprev = lambda bi, i: (bi, jnp.maximum(i - 1, 0), voff)
    vcur = lambda bi, i: (bi, i, voff)
    vnxt = lambda bi, i: (bi, jnp.minimum(i + 1, nblk - 1), voff)
    return pl.pallas_call(
        _attn_lat_body,
        grid=(b, nblk),
        in_specs=[pl.BlockSpec(memory_space=pltpu.SMEM),
                  pl.BlockSpec((1, BLOCK, ATTN_WIDTH), lambda bi, i: (bi, i, U_Q // ATTN_WIDTH)),
                  pl.BlockSpec((BLOCK, LANES), lambda bi, i: (i, 0)),
                  pl.BlockSpec((BLOCK, LANES), lambda bi, i: (i, 0)),
                  pl.BlockSpec((1, BLOCK, 256), prev),
                  pl.BlockSpec((1, BLOCK, 256), cur),
                  pl.BlockSpec((1, BLOCK, 256), nxt),
                  pl.BlockSpec((1, BLOCK, 256), vprev),
                  pl.BlockSpec((1, BLOCK, 256), vcur),
                  pl.BlockSpec((1, BLOCK, 256), vnxt),
                  pl.BlockSpec((1, 1, tc, 256), lambda bi, i: (bi, layer, 0, 0)),
                  pl.BlockSpec((1, 1, tc, 256), lambda bi, i: (bi, layer, 0, 0))],
        out_specs=pl.BlockSpec((1, BLOCK, ATTN_WIDTH), lambda bi, i: (bi, i, 0)),
        out_shape=jax.ShapeDtypeStruct((b, l, ATTN_WIDTH), BF16),
        scratch_shapes=[pltpu.VMEM((BLOCK, ATTN_WIDTH), BF16)],
        compiler_params=_cparams(("arbitrary", "arbitrary")),
        name="attn_lat",
    )(sink, u, cos, sin, k2r, k2r, k2r, u, u, u, kx2, vx2)


def _rope_tables(l):
    nf = HEAD_DIM // 4
    inv = ROPE_BASE ** (-jnp.arange(nf, dtype=F32) / nf)
    tok = jnp.arange(l)
    row = (tok // GRID_W).astype(F32)
    col = (tok % GRID_W).astype(F32)
    d = np.arange(LANES) % HEAD_DIM
    use_col = (d // (HEAD_DIM // 2)) == 1
    lo = (d % (HEAD_DIM // 2)) < nf
    pos = jnp.where(use_col[None, :], col[:, None], row[:, None])
    ang = pos * inv[d % nf][None, :]
    sign = np.where(lo, -1.0, 1.0).astype(np.float32)
    return jnp.cos(ang), jnp.sin(ang) * sign[None, :]


def _dot_split(a, b):
    ah = a.astype(BF16)
    bh = b.astype(BF16)
    al = (a - ah.astype(F32)).astype(BF16)
    bl = (b - bh.astype(F32)).astype(BF16)
    dot = functools.partial(jnp.dot, preferred_element_type=F32)
    return dot(ah, bh) + (dot(ah, bl) + dot(al, bh))


def _hy_hidden_body(z_ref, w1_ref, b1_ref, w2_ref, b2_ref, o_ref):
    h = jnp.sin(jnp.dot(z_ref[...], w1_ref[...], precision=HIGHEST, preferred_element_type=F32) + b1_ref[...])
    o_ref[...] = jnp.sin(jnp.dot(h, w2_ref[...], precision=HIGHEST, preferred_element_type=F32) + b2_ref[...])


def _hy_hidden(zfeat, w1, b1, w2, b2):
    rows = zfeat.shape[0]
    return pl.pallas_call(
        _hy_hidden_body,
        out_shape=jax.ShapeDtypeStruct((rows, LANES), F32),
        compiler_params=pltpu.CompilerParams(vmem_limit_bytes=VMEM_LIMIT),
        name="hy_hidden",
    )(zfeat, w1, b1, w2, b2)


def _hy_features(l):
    pos = jnp.arange(l, dtype=F32)
    t = pos / l
    freqs = jnp.linspace(1e-4, HY_BANDS - 1, HY_BANDS, dtype=F32)
    ang = 2.0 * jnp.pi * t[:, None] * freqs[None, :]
    z = jnp.concatenate([t[:, None], jnp.cos(ang), -jnp.sin(ang)], axis=-1)
    zrev = jnp.concatenate([z[:1], z[:0:-1]], axis=0)
    zz = jnp.concatenate([z, zrev], axis=0)
    return jnp.pad(zz, ((0, 0), (0, LANES - HY_EMB)))


def _hy_taps(hid_ref, w3f_ref, w3b_ref, decf_ref, decb_ref, l):
    ct = w3f_ref.shape[1]
    hf = _dot_split(hid_ref[0:l, :], w3f_ref[...])
    hb = _dot_split(hid_ref[l:2 * l, :], w3b_ref[...])
    row = lax.broadcasted_iota(jnp.int32, (l, ct), 0)
    t = row.astype(F32) / l
    trev = (l - row).astype(F32) / l
    hf = hf * jnp.exp(-t * jnp.abs(decf_ref[0]))
    hb = jnp.where(row == 0, 0.0, hb * jnp.exp(-trev * jnp.abs(decb_ref[0])))
    nsq = jnp.sum(hf * hf, axis=0, keepdims=True) + jnp.sum(hb * hb, axis=0, keepdims=True) + EPS
    sc = lax.rsqrt(nsq)
    return hf * sc, hb * sc


def _hy_filter_small_body(hid_ref, w3f_ref, w3b_ref, decf_ref, decb_ref, f_ref, o_ref, *, l):
    hf, hb = _hy_taps(hid_ref, w3f_ref, w3b_ref, decf_ref, decb_ref, l)
    taps = jnp.concatenate([hf, hb], axis=0)
    o_ref[0] = jnp.dot(f_ref[...], taps, precision=HIGHEST, preferred_element_type=F32) * (1.0 / (2 * l))


def _hy_filter_specs(ct):
    nct = HY_WIDTH // ct
    return [pl.BlockSpec((LANES, ct), lambda n, j: (0, n * nct + j)),
            pl.BlockSpec((LANES, ct), lambda n, j: (0, (HY_ORDER + n) * nct + j)),
            pl.BlockSpec((1, 1, ct), lambda n, j: (n, 0, j)),
            pl.BlockSpec((1, 1, ct), lambda n, j: (HY_ORDER + n, 0, j))]


def _hy_filter_small(hid, w3, decay, fmat, l):
    ct = 256
    return pl.pallas_call(
        functools.partial(_hy_filter_small_body, l=l),
        grid=(HY_ORDER, HY_WIDTH // ct),
        in_specs=[_const_spec((2 * l, LANES))] + _hy_filter_specs(ct) + [_const_spec(fmat.shape)],
        out_specs=pl.BlockSpec((1, 4 * l, ct), lambda n, j: (n, 0, j)),
        out_shape=jax.ShapeDtypeStruct((HY_ORDER, 4 * l, HY_WIDTH), F32),
        compiler_params=_cparams(("arbitrary", "arbitrary")),
        name="hy_filter_small",
    )(hid, w3, w3, decay, decay, fmat)


def _hy_filter_big_body(hid_ref, w3f_ref, w3b_ref, decf_ref, decb_ref, fa_ref, f2_ref, o_ref, xs, ss, *, l):
    n1 = 2 * l // LANES
    hf, hb = _hy_taps(hid_ref, w3f_ref, w3b_ref, decf_ref, decb_ref, l)
    for s in range(n1 // 2):
        xs[pl.ds(s * X_PITCH, LANES), :] = hf[s * LANES:(s + 1) * LANES]
        xs[pl.ds((n1 // 2 + s) * X_PITCH, LANES), :] = hb[s * LANES:(s + 1) * LANES]

    def stage_a(n2, carry):
        rhs = xs[pl.ds(n2, n1, stride=X_PITCH), :]
        out = _dot_split(fa_ref[n2], rhs)
        ss[pl.ds(n2, n1, stride=S_PITCH), :] = out[:n1]
        ss[pl.ds(LANES + n2, n1, stride=S_PITCH), :] = out[n1:]
        return carry

    lax.fori_loop(0, LANES, stage_a, 0, unroll=FFT_UNROLL // 2)

    side = min(FFT_SIDE, n1)

    def stage_b(kg, carry):
        slabs = [ss[pl.ds(pl.multiple_of((side * kg + e) * S_PITCH, 8), 2 * LANES), :] for e in range(side)]
        out = _dot_split(f2_ref[...], jnp.concatenate(slabs, axis=1)) * (1.0 / (2 * l))
        for e in range(side):
            rows = pl.ds(pl.multiple_of((side * kg + e) * 2 * LANES, 2 * LANES), 2 * LANES)
            o_ref[0, rows, :] = out[:, e * LANES:(e + 1) * LANES]
        return carry

    lax.fori_loop(0, n1 // side, stage_b, 0, unroll=2)


def _hy_filter_big(hid, w3, decay, fa, f2, l):
    n1 = 2 * l // LANES
    return pl.pallas_call(
        functools.partial(_hy_filter_big_body, l=l),
        grid=(HY_ORDER, HY_WIDTH // LANES),
        in_specs=[_const_spec((2 * l, LANES))] + _hy_filter_specs(LANES)
        + [_const_spec(fa.shape), _const_spec(f2.shape)],
        out_specs=pl.BlockSpec((1, 4 * l, LANES), lambda n, j: (n, 0, j)),
        out_shape=jax.ShapeDtypeStruct((HY_ORDER, 4 * l, HY_WIDTH), F32),
        scratch_shapes=[pltpu.VMEM((n1 * X_PITCH, LANES), F32),
                        pltpu.VMEM((n1 * S_PITCH, LANES), F32)],
        compiler_params=_cparams(("arbitrary", "arbitrary")),
        name="hy_filter_big",
    )(hid, w3, w3, decay, decay, fa, f2)


def _shift_rows(x, k):
    l = x.shape[0]
    y = pltpu.roll(x, k % l, 0)
    row = lax.broadcasted_iota(jnp.int32, (8, x.shape[1]), 0)
    if k > 0:
        return jnp.concatenate([jnp.where(row < k, 0.0, y[:8]), y[8:]], axis=0)
    return jnp.concatenate([y[:l - 8], jnp.where(row >= 8 + k, 0.0, y[l - 8:])], axis=0)


def _short_conv(x, w_ref, b_ref):
    prev = _shift_rows(x, 1)
    nxt = _shift_rows(x, -1)
    return prev * w_ref[0:1, :] + x * w_ref[1:2, :] + nxt * w_ref[2:3, :] + b_ref[...]


def _hy_small_body(uz_ref, ux1_ref, ux2_ref, wz_ref, wx1_ref, wx2_ref, bz_ref, bx1_ref, bx2_ref,
                   h_ref, bias_ref, f_ref, g_ref, o_ref):
    l = uz_ref.shape[1]
    n = 2 * l
    sig = [_short_conv(uz_ref[b].astype(F32), wz_ref, bz_ref) for b in range(2)]
    gates = [[_short_conv(ux1_ref[b].astype(F32), wx1_ref, bx1_ref) for b in range(2)],
             [_short_conv(ux2_ref[b].astype(F32), wx2_ref, bx2_ref) for b in range(2)]]
    for o in range(HY_ORDER):
        x = jnp.concatenate(sig, axis=0).astype(BF16)
        a = jnp.dot(f_ref[...], x, preferred_element_type=F32)
        ar, ai = a[:n], a[n:]
        hr, hi = h_ref[o, 0:n, :], h_ref[o, n:2 * n, :]
        y = jnp.concatenate([ar * hr - ai * hi, ar * hi + ai * hr], axis=0).astype(BF16)
        conv = jnp.dot(g_ref[...], y, preferred_element_type=F32)
        bias = bias_ref[o:o + 1, :]
        sig = [gates[o][b] * (conv[b * l:(b + 1) * l] + bias * sig[b]) for b in range(2)]
    for b in range(2):
        o_ref[b] = sig[b].astype(o_ref.dtype)


def _hy_small(u, short_w, short_b, hspec, hy_bias, fmat, gmat):
    b, l, _ = u.shape
    ct = 256
    col = lambda off: (lambda j, p: (p, 0, off // ct + j))
    wcol = lambda off: (lambda j, p: (0, off // ct + j))
    return pl.pallas_call(
        _hy_small_body,
        grid=(HY_WIDTH // ct, b // 2),
        in_specs=[pl.BlockSpec((2, l, ct), col(U_Z)),
                  pl.BlockSpec((2, l, ct), col(U_X1)),
                  pl.BlockSpec((2, l, ct), col(U_X2)),
                  pl.BlockSpec((3, ct), wcol(2 * HY_WIDTH)),
                  pl.BlockSpec((3, ct), wcol(0)),
                  pl.BlockSpec((3, ct), wcol(HY_WIDTH)),
                  pl.BlockSpec((1, ct), wcol(2 * HY_WIDTH)),
                  pl.BlockSpec((1, ct), wcol(0)),
                  pl.BlockSpec((1, ct), wcol(HY_WIDTH)),
                  pl.BlockSpec((HY_ORDER, 4 * l, ct), lambda j, p: (0, 0, j)),
                  pl.BlockSpec((HY_ORDER, ct), lambda j, p: (0, j)),
                  _const_spec(fmat.shape),
                  _const_spec(gmat.shape)],
        out_specs=pl.BlockSpec((2, l, ct), lambda j, p: (p, 0, j)),
        out_shape=jax.ShapeDtypeStruct((b, l, HY_WIDTH), BF16),
        compiler_params=_cparams(("arbitrary", "arbitrary")),
        name="hy_small",
    )(u, u, u, short_w, short_w, short_w, short_b, short_b, short_b, hspec, hy_bias, fmat, gmat)


def _hy_big_body(*refs, conv_sig, l):
    if conv_sig:
        (sig_ref, ws_ref, bs_ref, gate_ref, wg_ref, bg_ref, h_ref, bias_ref,
         fa_ref, ga_ref, f2_ref, f2i_ref, o_ref, xs, ss, ys) = refs
    else:
        (sig_ref, gate_ref, wg_ref, bg_ref, h_ref, bias_ref,
         fa_ref, ga_ref, f2_ref, f2i_ref, o_ref, xs, ss, ys) = refs
    n1 = 2 * l // LANES
    hn = n1 // 2

    for b in range(2):
        x = sig_ref[b].astype(F32)
        if conv_sig:
            x = _short_conv(x, ws_ref, bs_ref)
        for s in range(hn):
            xs[pl.ds((b * hn + s) * X_PITCH, LANES), :] = x[s * LANES:(s + 1) * LANES]

    def stage_a(n2, carry):
        xr = xs[pl.ds(n2, hn, stride=X_PITCH), :]
        xi = xs[pl.ds(hn * X_PITCH + n2, hn, stride=X_PITCH), :]
        rhs = jnp.concatenate([xr, xi], axis=0).astype(BF16)
        out = jnp.dot(fa_ref[n2], rhs, preferred_element_type=F32)
        ss[pl.ds(n2, n1, stride=S_PITCH), :] = out[:n1]
        ss[pl.ds(LANES + n2, n1, stride=S_PITCH), :] = out[n1:]
        return carry

    lax.fori_loop(0, LANES, stage_a, 0, unroll=FFT_UNROLL)

    side = 2
    groups = min(FFT_GROUPS, n1 // side)

    def stage_b(kg, carry):
        k1s = [[(kg * groups + p) * side + e for e in range(side)] for p in range(groups)]
        slabs = [jnp.concatenate([ss[pl.ds(pl.multiple_of(k1 * S_PITCH, 8), 2 * LANES), :] for k1 in grp],
                                 axis=1).astype(BF16) for grp in k1s]
        backs = []
        for p in range(groups):
            x = jnp.dot(f2_ref[...], slabs[p], preferred_element_type=F32)
            hbs = [pl.multiple_of(k1 * 2 * LANES, 2 * LANES) for k1 in k1s[p]]
            hr = jnp.concatenate([h_ref[0, pl.ds(hb, LANES), :] for hb in hbs], axis=1)
            hi = jnp.concatenate([h_ref[0, pl.ds(hb + LANES, LANES), :] for hb in hbs], axis=1)
            xr, xi = x[:LANES], x[LANES:]
            y = jnp.concatenate([xr * hr - xi * hi, xr * hi + xi * hr], axis=0).astype(BF16)
            backs.append(jnp.dot(f2i_ref[...], y, preferred_element_type=F32))
        for p in range(groups):
            for e in range(side):
                ss[pl.ds(pl.multiple_of(k1s[p][e] * S_PITCH, 8), 2 * LANES), :] = backs[p][:, e * LANES:(e + 1) * LANES]
        return carry

    lax.fori_loop(0, n1 // (side * groups), stage_b, 0)

    def stage_a_inv(n2, carry):
        br = ss[pl.ds(n2, n1, stride=S_PITCH), :]
        bi = ss[pl.ds(LANES + n2, n1, stride=S_PITCH), :]
        rhs = jnp.concatenate([br, bi], axis=0).astype(BF16)
        out = jnp.dot(ga_ref[n2], rhs, preferred_element_type=F32)
        ys[pl.ds(n2, hn, stride=X_PITCH), :] = out[:hn]
        ys[pl.ds(hn * X_PITCH + n2, hn, stride=X_PITCH), :] = out[hn:]
        return carry

    lax.fori_loop(0, LANES, stage_a_inv, 0, unroll=FFT_UNROLL)

    bias = bias_ref[...]
    for b in range(2):
        g = _short_conv(gate_ref[b].astype(F32), wg_ref, bg_ref)
        for s in range(hn):
            rows = pl.ds((b * hn + s) * X_PITCH, LANES)
            val = g[s * LANES:(s + 1) * LANES] * (ys[rows, :] + bias * xs[rows, :])
            o_ref[b, s * LANES:(s + 1) * LANES, :] = val.astype(o_ref.dtype)


def _hy_big(sig, sig_off, sig_w, gate, gate_off, gate_woff, short_w, short_b, hspec, order, hy_bias,
            tables, out_dtype):
    b, l, _ = gate.shape
    n1 = 2 * l // LANES
    fa, ga, f2, f2i = tables
    conv_sig = sig_w is not None
    col = lambda off: (lambda j, p: (p, 0, off // LANES + j))
    wcol = lambda off: (lambda j, p: (0, off // LANES + j))
    in_specs = [pl.BlockSpec((2, l, LANES), col(sig_off))]
    args = [sig]
    if conv_sig:
        in_specs += [pl.BlockSpec((3, LANES), wcol(sig_w)), pl.BlockSpec((1, LANES), wcol(sig_w))]
        args += [short_w, short_b]
    in_specs += [pl.BlockSpec((2, l, LANES), col(gate_off)),
                 pl.BlockSpec((3, LANES), wcol(gate_woff)),
                 pl.BlockSpec((1, LANES), wcol(gate_woff)),
                 pl.BlockSpec((1, 4 * l, LANES), lambda j, p: (order, 0, j), pipeline_mode=pl.Buffered(1)),
                 pl.BlockSpec((1, LANES), lambda j, p: (0, j)),
                 _const_spec(fa.shape), _const_spec(ga.shape), _const_spec(f2.shape), _const_spec(f2i.shape)]
    args += [gate, short_w, short_b, hspec, hy_bias[order][None, :], fa, ga, f2, f2i]
    return pl.pallas_call(
        functools.partial(_hy_big_body, conv_sig=conv_sig, l=l),
        grid=(HY_WIDTH // LANES, b // 2),
        in_specs=in_specs,
        out_specs=pl.BlockSpec((2, l, LANES), lambda j, p: (p, 0, j)),
        out_shape=jax.ShapeDtypeStruct((b, l, HY_WIDTH), out_dtype),
        scratch_shapes=[pltpu.VMEM((n1 * X_PITCH, LANES), F32),
                        pltpu.VMEM((n1 * S_PITCH, LANES), F32),
                        pltpu.VMEM((n1 * X_PITCH, LANES), F32)],
        compiler_params=_cparams(("arbitrary", "arbitrary")),
        name="hy_big",
    )(*args)


def _cis(idx, n):
    ang = idx.astype(F32) * (2.0 * np.pi / n)
    return jnp.cos(ang), jnp.sin(ang)


def _stack_complex(cr, ci):
    return jnp.concatenate([jnp.concatenate([cr, -ci], axis=-1), jnp.concatenate([ci, cr], axis=-1)], axis=-2)


def _dense_dft_tables(l):
    n = 2 * l
    k = jnp.arange(n, dtype=jnp.int32)
    t = jnp.arange(l, dtype=jnp.int32)
    cr, ci = _cis((k[:, None] * t[None, :]) % n, n)
    fwd = _stack_complex(cr, -ci)
    inv = _stack_complex(cr.T, ci.T)
    cfr, cfi = _cis((k[:, None] * k[None, :]) % n, n)
    filt = jnp.concatenate([cfr, -cfi], axis=0)
    return fwd.astype(BF16), inv.astype(BF16), filt


def _two_stage_dft_tables(l):
    n = 2 * l
    n1 = n // LANES
    hn = n1 // 2
    n2 = jnp.arange(LANES, dtype=jnp.int32)[:, None, None]
    k1 = jnp.arange(n1, dtype=jnp.int32)[None, :, None]
    m1 = jnp.arange(n1, dtype=jnp.int32)[None, None, :]
    cr, ci = _cis((k1 * (LANES * m1 + n2)) % n, n)
    fa = _stack_complex(cr[:, :, :hn], -ci[:, :, :hn])
    ga = _stack_complex(jnp.swapaxes(cr, 1, 2)[:, :hn, :], jnp.swapaxes(ci, 1, 2)[:, :hn, :])
    fa_filt = jnp.concatenate([cr, -ci], axis=1)
    a = jnp.arange(LANES, dtype=jnp.int32)
    c2r, c2i = _cis((a[:, None] * a[None, :]) % LANES, LANES)
    f2 = _stack_complex(c2r, -c2i)
    f2i = _stack_complex(c2r, c2i)
    return (fa.astype(BF16), ga.astype(BF16), f2.astype(BF16), f2i.astype(BF16)), fa_filt, f2


def _silu(x):
    return x * jax.nn.sigmoid(x)


def _proj_out_even_body(att_ref, ga_ref, hy_ref, gh_ref, w_ref, x_ref, g_ref, o_ref):
    a = att_ref[0].astype(F32) * _silu(ga_ref[0].astype(F32))
    h = hy_ref[0].astype(F32) * _silu(gh_ref[0].astype(F32))
    mixed = jnp.concatenate([a, h], axis=-1).astype(BF16)
    res = jnp.dot(mixed, w_ref[...], preferred_element_type=F32)
    o_ref[0] = x_ref[0] + g_ref[0] * res


def _proj_out_even(att, hy, u, w_out, x, gate, tm):
    b, l, d = x.shape
    per_batch = gate.shape[0] == b
    gmap = (lambda bi, i: (bi, 0, 0)) if per_batch else (lambda bi, i: (0, 0, 0))
    return pl.pallas_call(
        _proj_out_even_body,
        grid=(b, l // tm),
        in_specs=[pl.BlockSpec((1, tm, ATTN_WIDTH), lambda bi, i: (bi, i, 0)),
                  pl.BlockSpec((1, tm, ATTN_WIDTH), lambda bi, i: (bi, i, U_GA // ATTN_WIDTH)),
                  pl.BlockSpec((1, tm, HY_WIDTH), lambda bi, i: (bi, i, 0)),
                  pl.BlockSpec((1, tm, HY_WIDTH), lambda bi, i: (bi, i, U_GH // HY_WIDTH)),
                  _const_spec(w_out.shape),
                  pl.BlockSpec((1, tm, d), lambda bi, i: (bi, i, 0)),
                  pl.BlockSpec((1, 1, d), gmap)],
        out_specs=pl.BlockSpec((1, tm, d), lambda bi, i: (bi, i, 0)),
        out_shape=jax.ShapeDtypeStruct((b, l, d), F32),
        compiler_params=_cparams(("arbitrary", "arbitrary")),
        name="proj_out_even",
    )(att, u, hy, u, w_out, x, gate)


def _proj_out_odd_body(yf_ref, yb_ref, gt_ref, w_ref, x_ref, g_ref, fg_ref, o_ref, *, final_norm):
    y = yf_ref[0].astype(F32) + yb_ref[0].astype(F32)
    mixed = (y * _silu(gt_ref[0].astype(F32))).astype(BF16)
    res = jnp.dot(mixed, w_ref[...], preferred_element_type=F32)
    xn = x_ref[0] + g_ref[0] * res
    if final_norm:
        ms = jnp.mean(xn * xn, axis=-1, keepdims=True)
        xn = xn * lax.rsqrt(ms + EPS) * fg_ref[...]
    o_ref[0] = xn


def _proj_out_odd(yf, yb, u2, w_out, x, gate, final_g, final_norm, tm):
    b, l, d = x.shape
    per_batch = gate.shape[0] == b
    gmap = (lambda bi, i: (bi, 0, 0)) if per_batch else (lambda bi, i: (0, 0, 0))
    return pl.pallas_call(
        functools.partial(_proj_out_odd_body, final_norm=final_norm),
        grid=(b, l // tm),
        in_specs=[pl.BlockSpec((1, tm, d), lambda bi, i: (bi, i, 0)),
                  pl.BlockSpec((1, tm, d), lambda bi, i: (bi, i, 0)),
                  pl.BlockSpec((1, tm, d), lambda bi, i: (bi, i, 1)),
                  _const_spec(w_out.shape),
                  pl.BlockSpec((1, tm, d), lambda bi, i: (bi, i, 0)),
                  pl.BlockSpec((1, 1, d), gmap),
                  pl.BlockSpec((1, d), lambda bi, i: (0, 0))],
        out_specs=pl.BlockSpec((1, tm, d), lambda bi, i: (bi, i, 0)),
        out_shape=jax.ShapeDtypeStruct((b, l, d), F32),
        compiler_params=_cparams(("arbitrary", "arbitrary")),
        name="proj_out_odd",
    )(yf, yb, u2, w_out, x, gate, final_g.reshape(1, d))


def _rmsnorm_body(x_ref, g_ref, o_ref):
    x = x_ref[0]
    ms = jnp.mean(x * x, axis=-1, keepdims=True)
    o_ref[0] = x * lax.rsqrt(ms + EPS) * g_ref[...]


def _rmsnorm(x, g, tm):
    b, l, d = x.shape
    return pl.pallas_call(
        _rmsnorm_body,
        grid=(b, l // tm),
        in_specs=[pl.BlockSpec((1, tm, d), lambda bi, i: (bi, i, 0)),
                  pl.BlockSpec((1, d), lambda bi, i: (0, 0))],
        out_specs=pl.BlockSpec((1, tm, d), lambda bi, i: (bi, i, 0)),
        out_shape=jax.ShapeDtypeStruct((b, l, d), F32),
        compiler_params=_cparams(("arbitrary", "arbitrary")),
        name="final_rmsnorm",
    )(x, g.reshape(1, d))


LRU_GROUP = 8


LRU_HALO = 2 * LRU_GROUP


def _lru_gates(d, x_ref, xp_ref, xn_ref, keep_prev, keep_next, cw_ref, cb_ref, wa_ref, ba_ref, wx_ref, bx_ref,
               lam_ref, x_scr, a_scr, b_scr):
    tt = x_ref.shape[1]
    g = LRU_GROUP
    nk = LRU_BLOCK // LANES
    for s in range(g):
        for k in range(nk):
            lanes = slice(k * LANES, (k + 1) * LANES)
            x_scr[d, k, pl.ds(LRU_HALO + s, tt, stride=g), :] = x_ref[s, :, lanes].astype(F32)
            x_scr[d, k, pl.ds(s, 1), :] = xp_ref[s, 6:7, lanes].astype(F32) * keep_prev
            x_scr[d, k, pl.ds(g + s, 1), :] = xp_ref[s, 7:8, lanes].astype(F32) * keep_prev
            x_scr[d, k, pl.ds(LRU_HALO + tt * g + s, 1), :] = xn_ref[s, 0:1, lanes].astype(F32) * keep_next
    xcs = []
    for k in range(nk):
        lanes = slice(k * LANES, (k + 1) * LANES)
        acc = cb_ref[:, lanes]
        for j in range(4):
            acc = acc + x_scr[d, k, pl.ds(j * g, tt * g), :] * cw_ref[j:j + 1, lanes]
        xcs.append(acc)
    xc = jnp.concatenate(xcs, axis=1)
    xcb = xc.astype(BF16)
    lam = lam_ref[d:d + 1, :]
    softplus_neg = jnp.maximum(-lam, 0.0) + jnp.log1p(jnp.exp(-jnp.abs(lam)))
    tr = jnp.tanh(jnp.dot(xcb, wa_ref[d, 0], preferred_element_type=F32) + ba_ref[d:d + 1, :])
    ti = jnp.tanh(jnp.dot(xcb, wx_ref[d, 0], preferred_element_type=F32) + bx_ref[d:d + 1, :])
    half_c = (-0.5 * LRU_C) * softplus_neg
    log_a = half_c * tr + half_c
    a = jnp.exp(log_a)
    w4 = jnp.tanh(log_a) * (-0.25 * (a * a) - 0.25)
    root = jnp.where(w4 > 0.0, w4 * lax.rsqrt(w4), 0.0)
    b = root * (ti + 1.0) * xc
    for k in range(nk):
        a_scr[d, k] = a[:, k * LANES:(k + 1) * LANES]
        b_scr[d, k] = b[:, k * LANES:(k + 1) * LANES]


def _lru_body(xf_ref, xfp_ref, xfn_ref, xb_ref, xbp_ref, xbn_ref, cw_ref, cb_ref, wa_ref, ba_ref, wx_ref, bx_ref,
              lam_ref, h0_ref, yf_ref, yb_ref, fin_ref, x_scr, a_scr, b_scr, y_scr, h_scr):
    i = pl.program_id(2)
    nt = pl.num_programs(2)
    tt = xf_ref.shape[1]
    nk = LRU_BLOCK // LANES

    @pl.when(i == 0)
    def _():
        h_scr[...] = h0_ref[...]

    first = jnp.where(i == 0, 0.0, 1.0)
    last = jnp.where(i == nt - 1, 0.0, 1.0)
    params = (cw_ref, cb_ref, wa_ref, ba_ref, wx_ref, bx_ref, lam_ref, x_scr, a_scr, b_scr)
    _lru_gates(0, xf_ref, xfp_ref, xfn_ref, first, last, *params)
    _lru_gates(1, xb_ref, xbp_ref, xbn_ref, last, first, *params)

    def step(t, carry):
        rf = pl.ds(pl.multiple_of(t * LRU_GROUP, LRU_GROUP), LRU_GROUP)
        rb = pl.ds(pl.multiple_of((tt - 1 - t) * LRU_GROUP, LRU_GROUP), LRU_GROUP)
        out = []
        for d, rows in ((0, rf), (1, rb)):
            for k in range(nk):
                h = a_scr[d, k, rows, :] * carry[d * nk + k] + b_scr[d, k, rows, :]
                y_scr[d, k, rows, :] = h
                out.append(h)
        return tuple(out)

    init = tuple(h_scr[d, :, k * LANES:(k + 1) * LANES] for d in range(2) for k in range(nk))
    fin = lax.fori_loop(0, tt, step, init, unroll=16)
    for d in range(2):
        for k in range(nk):
            h_scr[d, :, k * LANES:(k + 1) * LANES] = fin[d * nk + k]

    for s in range(LRU_GROUP):
        for k in range(nk):
            rows = pl.ds(s, tt, stride=LRU_GROUP)
            yf_ref[s, :, k * LANES:(k + 1) * LANES] = y_scr[0, k, rows, :].astype(yf_ref.dtype)
            yb_ref[s, :, k * LANES:(k + 1) * LANES] = y_scr[1, k, rows, :].astype(yb_ref.dtype)

    @pl.when(i == nt - 1)
    def _():
        fin_ref[...] = h_scr[...]


def _lru(u2, conv_w, conv_b, wa, ba, wx, bx, lam, h0):
    b, l, _ = u2.shape
    w = N_HEADS_C * LRU_BLOCK
    hb = LRU_BLOCK
    g = LRU_GROUP
    assert b % g == 0, "the scan packs LRU_GROUP sequences per vreg"
    tt = min(l, LRU_TIME)
    nt = l // tt
    hr = tt // 8
    nhalo = l // 8
    vec = lambda rows: pl.BlockSpec((rows, hb), lambda h, gi, i: (0, h))
    wspec = pl.BlockSpec((2, 1, hb, hb), lambda h, gi, i: (0, h, 0, 0))
    tile = lambda f: pl.BlockSpec((g, tt, hb), lambda h, gi, i: (gi, f(i), h))
    halo_prev = lambda f: pl.BlockSpec((g, 8, hb), lambda h, gi, i: (gi, jnp.maximum(f(i) * hr - 1, 0), h))
    halo_next = lambda f: pl.BlockSpec((g, 8, hb), lambda h, gi, i: (gi, jnp.minimum((f(i) + 1) * hr, nhalo - 1), h))
    fwd = lambda i: i
    bwd = lambda i: nt - 1 - i
    state = pl.BlockSpec((2, g, hb), lambda h, gi, i: (0, gi, h))
    return pl.pallas_call(
        _lru_body,
        grid=(N_HEADS_C, b // g, nt),
        in_specs=[tile(fwd), halo_prev(fwd), halo_next(fwd), tile(bwd), halo_prev(bwd), halo_next(bwd),
                  vec(4), vec(1), wspec, vec(2), wspec, vec(2), vec(2), state],
        out_specs=[tile(fwd), tile(bwd), state],
        out_shape=[jax.ShapeDtypeStruct((b, l, w), BF16),
                   jax.ShapeDtypeStruct((b, l, w), BF16),
                   jax.ShapeDtypeStruct((2, b, w), F32)],
        scratch_shapes=[pltpu.VMEM((2, hb // LANES, (tt + 3) * g, LANES), F32),
                        pltpu.VMEM((2, hb // LANES, tt * g, LANES), F32),
                        pltpu.VMEM((2, hb // LANES, tt * g, LANES), F32),
                        pltpu.VMEM((2, hb // LANES, tt * g, LANES), F32),
                        pltpu.VMEM((2, g, hb), F32)],
        compiler_params=_cparams(("arbitrary", "arbitrary", "arbitrary")),
        name="rg_lru",
    )(u2, u2, u2, u2, u2, u2, conv_w, conv_b.reshape(1, w), wa, ba, wx, bx, lam, h0)


def _even_w_in(w):
    q = w[:, 0:ATTN_WIDTH]
    k = w[:, ATTN_WIDTH:ATTN_WIDTH + KV_COLS]
    v = w[:, ATTN_WIDTH + KV_COLS:ATTN_WIDTH + 2 * KV_COLS]
    o = ATTN_WIDTH + 2 * KV_COLS
    ga = w[:, o:o + ATTN_WIDTH]
    hy = w[:, o + ATTN_WIDTH:o + ATTN_WIDTH + 3 * HY_WIDTH]
    gh = w[:, o + ATTN_WIDTH + 3 * HY_WIDTH:]
    dup = lambda m: jnp.concatenate([m[:, h * HEAD_DIM:(h + 1) * HEAD_DIM] for h in range(N_KV_A) for _ in range(2)],
                                    axis=1)
    return jnp.concatenate([q, ga, hy, gh, dup(k), dup(v)], axis=1).astype(BF16)


def _dup_heads(x):
    return jnp.concatenate([x[..., h, :] for h in range(N_KV_A) for _ in range(2)], axis=-1).astype(BF16)


def kernel(x_prompt, x_sample, cache_k, cache_v, state_lru, c, c_ctx, mod_w, mod_b, norm_g, final_norm_g, a_w_in, a_w_out, a_sink, hy_short_w, hy_short_b, hy_w1, hy_b1, hy_w2, hy_b2, hy_w3, hy_decay, hy_bias, c_w_in, c_w_out, c_conv_w, c_conv_b, c_wa, c_ba, c_wx, c_bx, c_lambda):
    depth, d, _ = mod_w.shape
    bp, lp, _ = x_prompt.shape
    bs, ls, _ = x_sample.shape

    cvec = jnp.concatenate([c, c_ctx[None, :], jnp.zeros((16 - bs - 1, d), F32)], axis=0)
    mod = _modulation(cvec, mod_w, mod_b)

    cos, sin = _rope_tables(ls)
    kx2 = _dup_heads(cache_k)
    vx2 = _dup_heads(cache_v)
    tabs_big, fa_filt, f2_filt = _two_stage_dft_tables(ls)
    fwd_small, inv_small, filt_small = _dense_dft_tables(lp)
    feat_p, feat_s = _hy_features(lp), _hy_features(ls)
    hid_pad = ((0, LANES - hy_w1.shape[-1]),)

    tm_s = min(ls, ROWS_NARROW_COLS)
    tm_o = min(ls, ROWS_OUT)
    xp, xs = x_prompt, x_sample
    new_k, new_v, new_s = [], [], []
    for layer in range(depth):
        j = layer // 2
        sh_s, sc_s, g_s = (mod[layer, :bs, None, i * d:(i + 1) * d] for i in range(3))
        sh_p, sc_p, g_p = (mod[layer, bs:bs + 1, None, i * d:(i + 1) * d] for i in range(3))
        last = layer == depth - 1
        if layer % 2 == 0:
            w_in = _even_w_in(a_w_in[j])
            w_out = a_w_out[j].astype(BF16)
            up = _proj_in_flat(xp, norm_g[layer], sc_p, sh_p, w_in, ROWS_WIDE_COLS, 2)
            us = _proj_in(xs, norm_g[layer], sc_s, sh_s, w_in, min(ls, ROWS_WIDE_COLS), 2)
            new_k.append(jnp.stack([up[:, :, U_K2 + 2 * h * HEAD_DIM:U_K2 + (2 * h + 1) * HEAD_DIM]
                                    for h in range(N_KV_A)], axis=2).astype(F32))
            new_v.append(jnp.stack([up[:, :, U_V2 + 2 * h * HEAD_DIM:U_V2 + (2 * h + 1) * HEAD_DIM]
                                    for h in range(N_KV_A)], axis=2).astype(F32))
            att_p = _attn_ctx(up, a_sink[j])
            k2r = _rope_k(us, cos, sin, min(ls, ROWS_ROPE))
            att_s = _attn_lat(us, k2r, kx2, vx2, j, a_sink[j], cos, sin)

            w1 = jnp.pad(hy_w1[j], ((0, LANES - HY_EMB),) + hid_pad)
            b1 = jnp.pad(hy_b1[j][None, :], ((0, 0),) + hid_pad)
            w2 = jnp.pad(hy_w2[j], hid_pad + hid_pad)
            b2 = jnp.pad(hy_b2[j][None, :], ((0, 0),) + hid_pad)
            w3 = jnp.pad(hy_w3[j], hid_pad + ((0, 0),))
            decay = hy_decay[j].reshape(2 * HY_ORDER, 1, HY_WIDTH)
            sw, sb = hy_short_w[j], hy_short_b[j][None, :]
            hspec_p = _hy_filter_small(_hy_hidden(feat_p, w1, b1, w2, b2), w3, decay, filt_small, lp)
            hspec_s = _hy_filter_big(_hy_hidden(feat_s, w1, b1, w2, b2), w3, decay, fa_filt, f2_filt, ls)
            hy_p = _hy_small(up, sw, sb, hspec_p, hy_bias[j], fwd_small, inv_small)
            z1 = _hy_big(us, U_Z, 2 * HY_WIDTH, us, U_X1, 0, sw, sb, hspec_s, 0, hy_bias[j], tabs_big, F32)
            hy_s = _hy_big(z1, 0, None, us, U_X2, HY_WIDTH, sw, sb, hspec_s, 1, hy_bias[j], tabs_big, BF16)

            xp = _proj_out_even(att_p, hy_p, up, w_out, xp, g_p, lp)
            xs = _proj_out_even(att_s, hy_s, us, w_out, xs, g_s, tm_o)
            if last:
                xp = _rmsnorm(xp, final_norm_g, lp)
                xs = _rmsnorm(xs, final_norm_g, tm_o)
        else:
            w_in = c_w_in[j].astype(BF16)
            w_out = c_w_out[j].astype(BF16)
            wa = (0.5 * c_wa[j]).astype(BF16)
            wx = (0.5 * c_wx[j]).astype(BF16)
            up = _proj_in_flat(xp, norm_g[layer], sc_p, sh_p, w_in, ROWS_NARROW_COLS, 2)
            us = _proj_in(xs, norm_g[layer], sc_s, sh_s, w_in, tm_s, 2)
            lru_args = (c_conv_w[j], c_conv_b[j], wa, 0.5 * c_ba[j], wx, 0.5 * c_bx[j], c_lambda[j])
            yf_p, yb_p, fin_p = _lru(up, *lru_args, jnp.zeros((2, bp, d), F32))
            yf_s, yb_s, _ = _lru(us, *lru_args, jnp.moveaxis(state_lru[:, j], 1, 0))
            new_s.append(jnp.moveaxis(fin_p, 0, 1))
            xp = _proj_out_odd(yf_p, yb_p, up, w_out, xp, g_p, final_norm_g, last, lp)
            xs = _proj_out_odd(yf_s, yb_s, us, w_out, xs, g_s, final_norm_g, last, tm_o)

    return (xp, xs, jnp.stack(new_k, axis=1), jnp.stack(new_v, axis=1), jnp.stack(new_s, axis=1))
```

```python
import functools

import numpy as np
import jax
import jax.numpy as jnp
from jax import lax
from jax.experimental import pallas as pl
from jax.experimental.pallas import tpu as pltpu

F32 = jnp.float32
BF16 = jnp.bfloat16
HIGHEST = lax.Precision.HIGHEST

EPS = 1e-6
HEAD_DIM = 64
N_HEADS_A = 16
N_KV_A = 2
GQA_GROUP = N_HEADS_A // N_KV_A
ATTN_WIDTH = N_HEADS_A * HEAD_DIM
KV_COLS = N_KV_A * HEAD_DIM
WINDOW = 128
BLOCK = 128
GRID_W = 64
ROPE_BASE = 10000.0
HY_WIDTH = 1024
HY_ORDER = 2
HY_BANDS = 16
HY_EMB = 2 * HY_BANDS + 1
N_HEADS_C = 8
LRU_BLOCK = 256
LRU_C = 8.0
ATTN_SCALE = HEAD_DIM ** -0.5
NEG = float(np.finfo(np.float32).min)

LANES = 128
X_PITCH = LANES + 8
S_PITCH = 2 * LANES + 8
FFT_SIDE = 4
FFT_UNROLL = 128
FFT_GROUPS = 16

U_Q, U_GA, U_X1, U_X2, U_Z, U_GH, U_K2, U_V2 = 0, 1024, 2048, 3072, 4096, 5120, 6144, 6400
U_COLS = 6656

VMEM_LIMIT = 56 * 1024 * 1024

ROWS_WIDE_COLS = 512
ROWS_NARROW_COLS = 1024
ROWS_OUT = 512
ROWS_ROPE = 2048
LRU_TIME = 256


def _cparams(sem, vmem=VMEM_LIMIT):
    return pltpu.CompilerParams(dimension_semantics=sem, vmem_limit_bytes=vmem)


def _const_spec(shape):
    nd = len(shape)
    return pl.BlockSpec(shape, lambda *_: (0,) * nd, pipeline_mode=pl.Buffered(1))


def _mod_body(c_ref, w_ref, b_ref, o_ref):
    c = c_ref[...]
    s = c * jax.nn.sigmoid(c)
    o_ref[0] = jnp.dot(s.astype(BF16), w_ref[0].astype(BF16), preferred_element_type=F32) + b_ref[0]


def _modulation(cvec, mod_w, mod_b):
    depth, d, n3 = mod_w.shape
    rows = cvec.shape[0]
    tn = 512
    return pl.pallas_call(
        _mod_body,
        grid=(depth, n3 // tn),
        in_specs=[pl.BlockSpec((rows, d), lambda l, j: (0, 0)),
                  pl.BlockSpec((1, d, tn), lambda l, j: (l, 0, j)),
                  pl.BlockSpec((1, 1, tn), lambda l, j: (l, 0, j))],
        out_specs=pl.BlockSpec((1, rows, tn), lambda l, j: (l, 0, j)),
        out_shape=jax.ShapeDtypeStruct((depth, rows, n3), F32),
        compiler_params=_cparams(("arbitrary", "arbitrary")),
        name="modulation",
    )(cvec, mod_w, mod_b.reshape(depth, 1, n3))


NORM_ROWS = 16
DOT_COLS = 512


def _col_chunks(tn):
    edges = list(range(0, tn, DOT_COLS)) + [tn]
    if len(edges) > 2 and edges[-1] - edges[-2] < 256:
        del edges[-2]
    return list(zip(edges[:-1], edges[1:]))


def _proj_in_body(x_ref, g_ref, sc_ref, sh_ref, w_ref, o_ref, h_scr):
    @pl.when(pl.program_id(2) == 0)
    def _():
        gain = g_ref[...]
        scale1 = 1.0 + sc_ref[0]
        shift = sh_ref[0]

        def rows(r, carry):
            sl = pl.ds(pl.multiple_of(r * NORM_ROWS, NORM_ROWS), NORM_ROWS)
            x = x_ref[0, sl, :]
            ms = jnp.mean(x * x, axis=-1, keepdims=True)
            y = x * lax.rsqrt(ms + EPS) * gain
            h_scr[sl, :] = (y * scale1 + shift).astype(BF16)
            return carry

        lax.fori_loop(0, x_ref.shape[1] // NORM_ROWS, rows, 0, unroll=8)

    for c0, c1 in _col_chunks(w_ref.shape[1]):
        o_ref[0, :, c0:c1] = jnp.dot(h_scr[...], w_ref[:, c0:c1], preferred_element_type=F32).astype(o_ref.dtype)


def _proj_in(x, norm_g, scale, shift, w, tm, n_col_tiles):
    b, l, d = x.shape
    n = w.shape[1]
    tn = n // n_col_tiles
    assert tn * n_col_tiles == n and tn % LANES == 0
    per_batch = scale.shape[0] == b
    mod_map = (lambda bi, i, j: (bi, 0, 0)) if per_batch else (lambda bi, i, j: (0, 0, 0))
    return pl.pallas_call(
        _proj_in_body,
        grid=(b, l // tm, n // tn),
        in_specs=[pl.BlockSpec((1, tm, d), lambda bi, i, j: (bi, i, 0)),
                  pl.BlockSpec((1, d), lambda bi, i, j: (0, 0)),
                  pl.BlockSpec((1, 1, d), mod_map),
                  pl.BlockSpec((1, 1, d), mod_map),
                  pl.BlockSpec((d, tn), lambda bi, i, j: (0, j))],
        out_specs=pl.BlockSpec((1, tm, tn), lambda bi, i, j: (bi, i, j)),
        out_shape=jax.ShapeDtypeStruct((b, l, n), BF16),
        scratch_shapes=[pltpu.VMEM((tm, d), BF16)],
        compiler_params=_cparams(("arbitrary", "arbitrary", "arbitrary")),
        name="proj_in",
    )(x, norm_g.reshape(1, d), scale, shift, w)


def _proj_in_flat(x, norm_g, scale, shift, w, rows, n_col_tiles):
    b, l, d = x.shape
    assert scale.shape[0] == 1
    tm = min(b * l, rows)
    out = _proj_in(x.reshape(1, b * l, d), norm_g, scale, shift, w, tm, n_col_tiles)
    return out.reshape(b, l, w.shape[1])


ATTN_SLABS = 4


def _attn_heads(q, k2, v2, sink_ref, masks, o_ref):
    t = q.shape[0]
    left = lax.broadcasted_iota(jnp.int32, (t, LANES), 1) < HEAD_DIM
    top = lax.broadcasted_iota(jnp.int32, (LANES, t), 0) < HEAD_DIM
    zero = jnp.zeros((t, LANES), q.dtype)
    for j0 in range(0, ATTN_WIDTH // LANES, ATTN_SLABS):
        kv = (2 * j0) // GQA_GROUP
        kk = k2[:, LANES * kv:LANES * (kv + 1)]
        vv = v2[:, LANES * kv:LANES * (kv + 1)]
        blocks = []
        for j in range(j0, j0 + ATTN_SLABS):
            qs = q[:, LANES * j:LANES * (j + 1)]
            blocks += [jnp.where(left, qs, zero), jnp.where(left, zero, qs)]
        qstack = jnp.concatenate(blocks, axis=0)
        s_t = lax.dot_general(kk, qstack, (((1,), (1,)), ((), ())), preferred_element_type=F32)
        ps, dens = [], []
        for bi in range(2 * ATTN_SLABS):
            s = s_t[:, bi * t:(bi + 1) * t]
            pieces, pos = [], 0
            for r0, r1, mk in masks:
                if r0 > pos:
                    pieces.append(s[pos:r0])
                pieces.append(jnp.where(mk, s[r0:r1], NEG))
                pos = r1
            if pieces:
                if pos < s.shape[0]:
                    pieces.append(s[pos:])
                s = jnp.concatenate(pieces, axis=0)
            sink = sink_ref[2 * j0 + bi]
            m = jnp.maximum(jnp.max(s, axis=0, keepdims=True), sink)
            p = jnp.exp(s - m)
            dens.append(jnp.sum(p, axis=0, keepdims=True) + jnp.exp(sink - m))
            ps.append(p.astype(BF16))
        p_t = jnp.concatenate(ps, axis=1)
        o_t = lax.dot_general(vv, p_t, (((0,), (0,)), ((), ())), preferred_element_type=F32)
        o_t = o_t / jnp.concatenate(dens, axis=1)
        for jj in range(ATTN_SLABS):
            comb = jnp.where(top, o_t[:, (2 * jj) * t:(2 * jj + 1) * t], o_t[:, (2 * jj + 1) * t:(2 * jj + 2) * t])
            o_ref[0, :, LANES * (j0 + jj):LANES * (j0 + jj + 1)] = comb.T.astype(o_ref.dtype)


def _attn_ctx_body(sink_ref, q_ref, k_ref, v_ref, o_ref, ko_ref, vo_ref):
    k2, v2 = k_ref[0], v_ref[0]
    _attn_heads(q_ref[0] * ATTN_SCALE, k2, v2, sink_ref, (), o_ref)
    left = lax.broadcasted_iota(jnp.int32, (k2.shape[0], LANES), 1) < HEAD_DIM
    ko_ref[0] = jnp.where(left, k2[:, :LANES], k2[:, LANES:]).astype(ko_ref.dtype)
    vo_ref[0] = jnp.where(left, v2[:, :LANES], v2[:, LANES:]).astype(vo_ref.dtype)


def _attn_ctx(u, sink):
    b, t, _ = u.shape
    kv_spec = pl.BlockSpec((1, t, KV_COLS), lambda bi: (bi, 0, 0))
    return pl.pallas_call(
        _attn_ctx_body,
        grid=(b,),
        in_specs=[pl.BlockSpec(memory_space=pltpu.SMEM),
                  pl.BlockSpec((1, t, ATTN_WIDTH), lambda bi: (bi, 0, U_Q // ATTN_WIDTH)),
                  pl.BlockSpec((1, t, 256), lambda bi: (bi, 0, U_K2 // 256)),
                  pl.BlockSpec((1, t, 256), lambda bi: (bi, 0, U_V2 // 256))],
        out_specs=[pl.BlockSpec((1, t, ATTN_WIDTH), lambda bi: (bi, 0, 0)), kv_spec, kv_spec],
        out_shape=[jax.ShapeDtypeStruct((b, t, ATTN_WIDTH), BF16),
                   jax.ShapeDtypeStruct((b, t, KV_COLS), F32),
                   jax.ShapeDtypeStruct((b, t, KV_COLS), F32)],
        compiler_params=_cparams(("arbitrary",)),
        name="attn_ctx",
    )(sink, u, u, u)


def _rope_slab(x, cos, sin):
    lane = lax.broadcasted_iota(jnp.int32, x.shape, 1)
    lo = (lane & 16) == 0
    partner = jnp.where(lo, pltpu.roll(x, LANES - 16, 1), pltpu.roll(x, 16, 1))
    return x * cos + partner * sin


def _rope_k_body(k_ref, cos_ref, sin_ref, o_ref):
    cos, sin = cos_ref[...], sin_ref[...]
    for s in range(2):
        x = k_ref[0, :, LANES * s:LANES * (s + 1)].astype(F32)
        o_ref[0, :, LANES * s:LANES * (s + 1)] = _rope_slab(x, cos, sin).astype(o_ref.dtype)


def _rope_k(u, cos, sin, tl):
    b, l, _ = u.shape
    return pl.pallas_call(
        _rope_k_body,
        grid=(b, l // tl),
        in_specs=[pl.BlockSpec((1, tl, 256), lambda bi, i: (bi, i, U_K2 // 256)),
                  pl.BlockSpec((tl, LANES), lambda bi, i: (i, 0)),
                  pl.BlockSpec((tl, LANES), lambda bi, i: (i, 0))],
        out_specs=pl.BlockSpec((1, tl, 256), lambda bi, i: (bi, i, 0)),
        out_shape=jax.ShapeDtypeStruct((b, l, 256), BF16),
        compiler_params=_cparams(("arbitrary", "arbitrary")),
        name="rope_k",
    )(u, cos, sin)


def _attn_lat_body(sink_ref, q_ref, cos_ref, sin_ref, kp_ref, kc_ref, kn_ref, vp_ref, vc_ref, vn_ref,
                   kx_ref, vx_ref, o_ref, q_scr):
    i = pl.program_id(1)
    nblk = pl.num_programs(1)
    cos, sin = cos_ref[...], sin_ref[...]
    for j in range(ATTN_WIDTH // LANES):
        x = q_ref[0, :, LANES * j:LANES * (j + 1)].astype(F32)
        q_scr[:, LANES * j:LANES * (j + 1)] = (_rope_slab(x, cos, sin) * ATTN_SCALE).astype(BF16)
    k2 = jnp.concatenate([kx_ref[0, 0], kp_ref[0], kc_ref[0], kn_ref[0]], axis=0)
    v2 = jnp.concatenate([vx_ref[0, 0], vp_ref[0], vc_ref[0], vn_ref[0]], axis=0)
    tc = kx_ref.shape[2]
    diff = (lax.broadcasted_iota(jnp.int32, (BLOCK, BLOCK), 0)
            - lax.broadcasted_iota(jnp.int32, (BLOCK, BLOCK), 1))
    prev_ok = diff >= jnp.where(i > 0, 0, 2 * BLOCK)
    next_ok = diff <= jnp.where(i < nblk - 1, 0, -2 * BLOCK)
    masks = ((tc, tc + BLOCK, prev_ok), (tc + 2 * BLOCK, tc + 3 * BLOCK, next_ok))
    _attn_heads(q_scr[...], k2, v2, sink_ref, masks, o_ref)


def _attn_lat(u, k2r, kx2, vx2, layer, sink, cos, sin):
    b, l, _ = u.shape
    nblk = l // BLOCK
    tc = kx2.shape[2]
    prev = lambda bi, i: (bi, jnp.maximum(i - 1, 0), 0)
    cur = lambda bi, i: (bi, i, 0)
    nxt = lambda bi, i: (bi, jnp.minimum(i + 1, nblk - 1), 0)
    voff = U_V2 // 256
    vprev = lambda bi, i: (bi, jnp.maximum(i - 1, 0), voff)
    vcur = lambda bi, i: (bi, i, voff)
    vnxt = lambda bi, i: (bi, jnp.minimum(i + 1, nblk - 1), voff)
    return pl.pallas_call(
        _attn_lat_body,
        grid=(b, nblk),
        in_specs=[pl.BlockSpec(memory_space=pltpu.SMEM),
                  pl.BlockSpec((1, BLOCK, ATTN_WIDTH), lambda bi, i: (bi, i, U_Q // ATTN_WIDTH)),
                  pl.BlockSpec((BLOCK, LANES), lambda bi, i: (i, 0)),
                  pl.BlockSpec((BLOCK, LANES), lambda bi, i: (i, 0)),
                  pl.BlockSpec((1, BLOCK, 256), prev),
                  pl.BlockSpec((1, BLOCK, 256), cur),
                  pl.BlockSpec((1, BLOCK, 256), nxt),
                  pl.BlockSpec((1, BLOCK, 256), vprev),
                  pl.BlockSpec((1, BLOCK, 256), vcur),
                  pl.BlockSpec((1, BLOCK, 256), vnxt),
                  pl.BlockSpec((1, 1, tc, 256), lambda bi, i: (bi, layer, 0, 0)),
                  pl.BlockSpec((1, 1, tc, 256), lambda bi, i: (bi, layer, 0, 0))],
        out_specs=pl.BlockSpec((1, BLOCK, ATTN_WIDTH), lambda bi, i: (bi, i, 0)),
        out_shape=jax.ShapeDtypeStruct((b, l, ATTN_WIDTH), BF16),
        scratch_shapes=[pltpu.VMEM((BLOCK, ATTN_WIDTH), BF16)],
        compiler_params=_cparams(("arbitrary", "arbitrary")),
        name="attn_lat",
    )(sink, u, cos, sin, k2r, k2r, k2r, u, u, u, kx2, vx2)


def _rope_tables(l):
    nf = HEAD_DIM // 4
    inv = ROPE_BASE ** (-jnp.arange(nf, dtype=F32) / nf)
    tok = jnp.arange(l)
    row = (tok // GRID_W).astype(F32)
    col = (tok % GRID_W).astype(F32)
    d = np.arange(LANES) % HEAD_DIM
    use_col = (d // (HEAD_DIM // 2)) == 1
    lo = (d % (HEAD_DIM // 2)) < nf
    pos = jnp.where(use_col[None, :], col[:, None], row[:, None])
    ang = pos * inv[d % nf][None, :]
    sign = np.where(lo, -1.0, 1.0).astype(np.float32)
    return jnp.cos(ang), jnp.sin(ang) * sign[None, :]


def _dot_split(a, b):
    ah = a.astype(BF16)
    bh = b.astype(BF16)
    al = (a - ah.astype(F32)).astype(BF16)
    bl = (b - bh.astype(F32)).astype(BF16)
    dot = functools.partial(jnp.dot, preferred_element_type=F32)
    return dot(ah, bh) + (dot(ah, bl) + dot(al, bh))


def _hy_hidden_body(z_ref, w1_ref, b1_ref, w2_ref, b2_ref, o_ref):
    h = jnp.sin(jnp.dot(z_ref[...], w1_ref[...], precision=HIGHEST, preferred_element_type=F32) + b1_ref[...])
    o_ref[...] = jnp.sin(jnp.dot(h, w2_ref[...], precision=HIGHEST, preferred_element_type=F32) + b2_ref[...])


def _hy_hidden(zfeat, w1, b1, w2, b2):
    rows = zfeat.shape[0]
    return pl.pallas_call(
        _hy_hidden_body,
        out_shape=jax.ShapeDtypeStruct((rows, LANES), F32),
        compiler_params=pltpu.CompilerParams(vmem_limit_bytes=VMEM_LIMIT),
        name="hy_hidden",
    )(zfeat, w1, b1, w2, b2)


def _hy_features(l):
    pos = jnp.arange(l, dtype=F32)
    t = pos / l
    freqs = jnp.linspace(1e-4, HY_BANDS - 1, HY_BANDS, dtype=F32)
    ang = 2.0 * jnp.pi * t[:, None] * freqs[None, :]
    z = jnp.concatenate([t[:, None], jnp.cos(ang), -jnp.sin(ang)], axis=-1)
    zrev = jnp.concatenate([z[:1], z[:0:-1]], axis=0)
    zz = jnp.concatenate([z, zrev], axis=0)
    return jnp.pad(zz, ((0, 0), (0, LANES - HY_EMB)))


def _hy_taps(hid_ref, w3f_ref, w3b_ref, decf_ref, decb_ref, l):
    ct = w3f_ref.shape[1]
    hf = _dot_split(hid_ref[0:l, :], w3f_ref[...])
    hb = _dot_split(hid_ref[l:2 * l, :], w3b_ref[...])
    row = lax.broadcasted_iota(jnp.int32, (l, ct), 0)
    t = row.astype(F32) / l
    trev = (l - row).astype(F32) / l
    hf = hf * jnp.exp(-t * jnp.abs(decf_ref[0]))
    hb = jnp.where(row == 0, 0.0, hb * jnp.exp(-trev * jnp.abs(decb_ref[0])))
    nsq = jnp.sum(hf * hf, axis=0, keepdims=True) + jnp.sum(hb * hb, axis=0, keepdims=True) + EPS
    sc = lax.rsqrt(nsq)
    return hf * sc, hb * sc


def _hy_filter_small_body(hid_ref, w3f_ref, w3b_ref, decf_ref, decb_ref, f_ref, o_ref, *, l):
    hf, hb = _hy_taps(hid_ref, w3f_ref, w3b_ref, decf_ref, decb_ref, l)
    taps = jnp.concatenate([hf, hb], axis=0)
    o_ref[0] = jnp.dot(f_ref[...], taps, precision=HIGHEST, preferred_element_type=F32) * (1.0 / (2 * l))


def _hy_filter_specs(ct):
    nct = HY_WIDTH // ct
    return [pl.BlockSpec((LANES, ct), lambda n, j: (0, n * nct + j)),
            pl.BlockSpec((LANES, ct), lambda n, j: (0, (HY_ORDER + n) * nct + j)),
            pl.BlockSpec((1, 1, ct), lambda n, j: (n, 0, j)),
            pl.BlockSpec((1, 1, ct), lambda n, j: (HY_ORDER + n, 0, j))]


def _hy_filter_small(hid, w3, decay, fmat, l):
    ct = 256
    return pl.pallas_call(
        functools.partial(_hy_filter_small_body, l=l),
        grid=(HY_ORDER, HY_WIDTH // ct),
        in_specs=[_const_spec((2 * l, LANES))] + _hy_filter_specs(ct) + [_const_spec(fmat.shape)],
        out_specs=pl.BlockSpec((1, 4 * l, ct), lambda n, j: (n, 0, j)),
        out_shape=jax.ShapeDtypeStruct((HY_ORDER, 4 * l, HY_WIDTH), F32),
        compiler_params=_cparams(("arbitrary", "arbitrary")),
        name="hy_filter_small",
    )(hid, w3, w3, decay, decay, fmat)


def _hy_filter_big_body(hid_ref, w3f_ref, w3b_ref, decf_ref, decb_ref, fa_ref, f2_ref, o_ref, xs, ss, *, l):
    n1 = 2 * l // LANES
    hf, hb = _hy_taps(hid_ref, w3f_ref, w3b_ref, decf_ref, decb_ref, l)
    for s in range(n1 // 2):
        xs[pl.ds(s * X_PITCH, LANES), :] = hf[s * LANES:(s + 1) * LANES]
        xs[pl.ds((n1 // 2 + s) * X_PITCH, LANES), :] = hb[s * LANES:(s + 1) * LANES]

    def stage_a(n2, carry):
        rhs = xs[pl.ds(n2, n1, stride=X_PITCH), :]
        out = _dot_split(fa_ref[n2], rhs)
        ss[pl.ds(n2, n1, stride=S_PITCH), :] = out[:n1]
        ss[pl.ds(LANES + n2, n1, stride=S_PITCH), :] = out[n1:]
        return carry

    lax.fori_loop(0, LANES, stage_a, 0, unroll=FFT_UNROLL // 2)

    side = min(FFT_SIDE, n1)

    def stage_b(kg, carry):
        slabs = [ss[pl.ds(pl.multiple_of((side * kg + e) * S_PITCH, 8), 2 * LANES), :] for e in range(side)]
        out = _dot_split(f2_ref[...], jnp.concatenate(slabs, axis=1)) * (1.0 / (2 * l))
        for e in range(side):
            rows = pl.ds(pl.multiple_of((side * kg + e) * 2 * LANES, 2 * LANES), 2 * LANES)
            o_ref[0, rows, :] = out[:, e * LANES:(e + 1) * LANES]
        return carry

    lax.fori_loop(0, n1 // side, stage_b, 0, unroll=2)


def _hy_filter_big(hid, w3, decay, fa, f2, l):
    n1 = 2 * l // LANES
    return pl.pallas_call(
        functools.partial(_hy_filter_big_body, l=l),
        grid=(HY_ORDER, HY_WIDTH // LANES),
        in_specs=[_const_spec((2 * l, LANES))] + _hy_filter_specs(LANES)
        + [_const_spec(fa.shape), _const_spec(f2.shape)],
        out_specs=pl.BlockSpec((1, 4 * l, LANES), lambda n, j: (n, 0, j)),
        out_shape=jax.ShapeDtypeStruct((HY_ORDER, 4 * l, HY_WIDTH), F32),
        scratch_shapes=[pltpu.VMEM((n1 * X_PITCH, LANES), F32),
                        pltpu.VMEM((n1 * S_PITCH, LANES), F32)],
        compiler_params=_cparams(("arbitrary", "arbitrary")),
        name="hy_filter_big",
    )(hid, w3, w3, decay, decay, fa, f2)


def _shift_rows(x, k):
    l = x.shape[0]
    y = pltpu.roll(x, k % l, 0)
    row = lax.broadcasted_iota(jnp.int32, (8, x.shape[1]), 0)
    if k > 0:
        return jnp.concatenate([jnp.where(row < k, 0.0, y[:8]), y[8:]], axis=0)
    return jnp.concatenate([y[:l - 8], jnp.where(row >= 8 + k, 0.0, y[l - 8:])], axis=0)


def _short_conv(x, w_ref, b_ref):
    prev = _shift_rows(x, 1)
    nxt = _shift_rows(x, -1)
    return prev * w_ref[0:1, :] + x * w_ref[1:2, :] + nxt * w_ref[2:3, :] + b_ref[...]


def _hy_small_body(uz_ref, ux1_ref, ux2_ref, wz_ref, wx1_ref, wx2_ref, bz_ref, bx1_ref, bx2_ref,
                   h_ref, bias_ref, f_ref, g_ref, o_ref):
    l = uz_ref.shape[1]
    n = 2 * l
    sig = [_short_conv(uz_ref[b].astype(F32), wz_ref, bz_ref) for b in range(2)]
    gates = [[_short_conv(ux1_ref[b].astype(F32), wx1_ref, bx1_ref) for b in range(2)],
             [_short_conv(ux2_ref[b].astype(F32), wx2_ref, bx2_ref) for b in range(2)]]
    for o in range(HY_ORDER):
        x = jnp.concatenate(sig, axis=0).astype(BF16)
        a = jnp.dot(f_ref[...], x, preferred_element_type=F32)
        ar, ai = a[:n], a[n:]
        hr, hi = h_ref[o, 0:n, :], h_ref[o, n:2 * n, :]
        y = jnp.concatenate([ar * hr - ai * hi, ar * hi + ai * hr], axis=0).astype(BF16)
        conv = jnp.dot(g_ref[...], y, preferred_element_type=F32)
        bias = bias_ref[o:o + 1, :]
        sig = [gates[o][b] * (conv[b * l:(b + 1) * l] + bias * sig[b]) for b in range(2)]
    for b in range(2):
        o_ref[b] = sig[b].astype(o_ref.dtype)


def _hy_small(u, short_w, short_b, hspec, hy_bias, fmat, gmat):
    b, l, _ = u.shape
    ct = 256
    col = lambda off: (lambda j, p: (p, 0, off // ct + j))
    wcol = lambda off: (lambda j, p: (0, off // ct + j))
    return pl.pallas_call(
        _hy_small_body,
        grid=(HY_WIDTH // ct, b // 2),
        in_specs=[pl.BlockSpec((2, l, ct), col(U_Z)),
                  pl.BlockSpec((2, l, ct), col(U_X1)),
                  pl.BlockSpec((2, l, ct), col(U_X2)),
                  pl.BlockSpec((3, ct), wcol(2 * HY_WIDTH)),
                  pl.BlockSpec((3, ct), wcol(0)),
                  pl.BlockSpec((3, ct), wcol(HY_WIDTH)),
                  pl.BlockSpec((1, ct), wcol(2 * HY_WIDTH)),
                  pl.BlockSpec((1, ct), wcol(0)),
                  pl.BlockSpec((1, ct), wcol(HY_WIDTH)),
                  pl.BlockSpec((HY_ORDER, 4 * l, ct), lambda j, p: (0, 0, j)),
                  pl.BlockSpec((HY_ORDER, ct), lambda j, p: (0, j)),
                  _const_spec(fmat.shape),
                  _const_spec(gmat.shape)],
        out_specs=pl.BlockSpec((2, l, ct), lambda j, p: (p, 0, j)),
        out_shape=jax.ShapeDtypeStruct((b, l, HY_WIDTH), BF16),
        compiler_params=_cparams(("arbitrary", "arbitrary")),
        name="hy_small",
    )(u, u, u, short_w, short_w, short_w, short_b, short_b, short_b, hspec, hy_bias, fmat, gmat)


def _hy_big_body(*refs, conv_sig, l):
    if conv_sig:
        (sig_ref, ws_ref, bs_ref, gate_ref, wg_ref, bg_ref, h_ref, bias_ref,
         fa_ref, ga_ref, f2_ref, f2i_ref, o_ref, xs, ss, ys) = refs
    else:
        (sig_ref, gate_ref, wg_ref, bg_ref, h_ref, bias_ref,
         fa_ref, ga_ref, f2_ref, f2i_ref, o_ref, xs, ss, ys) = refs
    n1 = 2 * l // LANES
    hn = n1 // 2

    for b in range(2):
        x = sig_ref[b].astype(F32)
        if conv_sig:
            x = _short_conv(x, ws_ref, bs_ref)
        for s in range(hn):
            xs[pl.ds((b * hn + s) * X_PITCH, LANES), :] = x[s * LANES:(s + 1) * LANES]

    def stage_a(n2, carry):
        xr = xs[pl.ds(n2, hn, stride=X_PITCH), :]
        xi = xs[pl.ds(hn * X_PITCH + n2, hn, stride=X_PITCH), :]
        rhs = jnp.concatenate([xr, xi], axis=0).astype(BF16)
        out = jnp.dot(fa_ref[n2], rhs, preferred_element_type=F32)
        ss[pl.ds(n2, n1, stride=S_PITCH), :] = out[:n1]
        ss[pl.ds(LANES + n2, n1, stride=S_PITCH), :] = out[n1:]
        return carry

    lax.fori_loop(0, LANES, stage_a, 0, unroll=FFT_UNROLL)

    side = 2
    groups = min(FFT_GROUPS, n1 // side)

    def stage_b(kg, carry):
        k1s = [[(kg * groups + p) * side + e for e in range(side)] for p in range(groups)]
        slabs = [jnp.concatenate([ss[pl.ds(pl.multiple_of(k1 * S_PITCH, 8), 2 * LANES), :] for k1 in grp],
                                 axis=1).astype(BF16) for grp in k1s]
        backs = []
        for p in range(groups):
            x = jnp.dot(f2_ref[...], slabs[p], preferred_element_type=F32)
            hbs = [pl.multiple_of(k1 * 2 * LANES, 2 * LANES) for k1 in k1s[p]]
            hr = jnp.concatenate([h_ref[0, pl.ds(hb, LANES), :] for hb in hbs], axis=1)
            hi = jnp.concatenate([h_ref[0, pl.ds(hb + LANES, LANES), :] for hb in hbs], axis=1)
            xr, xi = x[:LANES], x[LANES:]
            y = jnp.concatenate([xr * hr - xi * hi, xr * hi + xi * hr], axis=0).astype(BF16)
            backs.append(jnp.dot(f2i_ref[...], y, preferred_element_type=F32))
        for p in range(groups):
            for e in range(side):
                ss[pl.ds(pl.multiple_of(k1s[p][e] * S_PITCH, 8), 2 * LANES), :] = backs[p][:, e * LANES:(e + 1) * LANES]
        return carry

    lax.fori_loop(0, n1 // (side * groups), stage_b, 0)

    def stage_a_inv(n2, carry):
        br = ss[pl.ds(n2, n1, stride=S_PITCH), :]
        bi = ss[pl.ds(LANES + n2, n1, stride=S_PITCH), :]
        rhs = jnp.concatenate([br, bi], axis=0).astype(BF16)
        out = jnp.dot(ga_ref[n2], rhs, preferred_element_type=F32)
        ys[pl.ds(n2, hn, stride=X_PITCH), :] = out[:hn]
        ys[pl.ds(hn * X_PITCH + n2, hn, stride=X_PITCH), :] = out[hn:]
        return carry

    lax.fori_loop(0, LANES, stage_a_inv, 0, unroll=FFT_UNROLL)

    bias = bias_ref[...]
    for b in range(2):
        g = _short_conv(gate_ref[b].astype(F32), wg_ref, bg_ref)
        for s in range(hn):
            rows = pl.ds((b * hn + s) * X_PITCH, LANES)
            val = g[s * LANES:(s + 1) * LANES] * (ys[rows, :] + bias * xs[rows, :])
            o_ref[b, s * LANES:(s + 1) * LANES, :] = val.astype(o_ref.dtype)


def _hy_big(sig, sig_off, sig_w, gate, gate_off, gate_woff, short_w, short_b, hspec, order, hy_bias,
            tables, out_dtype):
    b, l, _ = gate.shape
    n1 = 2 * l // LANES
    fa, ga, f2, f2i = tables
    conv_sig = sig_w is not None
    col = lambda off: (lambda j, p: (p, 0, off // LANES + j))
    wcol = lambda off: (lambda j, p: (0, off // LANES + j))
    in_specs = [pl.BlockSpec((2, l, LANES), col(sig_off))]
    args = [sig]
    if conv_sig:
        in_specs += [pl.BlockSpec((3, LANES), wcol(sig_w)), pl.BlockSpec((1, LANES), wcol(sig_w))]
        args += [short_w, short_b]
    in_specs += [pl.BlockSpec((2, l, LANES), col(gate_off)),
                 pl.BlockSpec((3, LANES), wcol(gate_woff)),
                 pl.BlockSpec((1, LANES), wcol(gate_woff)),
                 pl.BlockSpec((1, 4 * l, LANES), lambda j, p: (order, 0, j), pipeline_mode=pl.Buffered(1)),
                 pl.BlockSpec((1, LANES), lambda j, p: (0, j)),
                 _const_spec(fa.shape), _const_spec(ga.shape), _const_spec(f2.shape), _const_spec(f2i.shape)]
    args += [gate, short_w, short_b, hspec, hy_bias[order][None, :], fa, ga, f2, f2i]
    return pl.pallas_call(
        functools.partial(_hy_big_body, conv_sig=conv_sig, l=l),
        grid=(HY_WIDTH // LANES, b // 2),
        in_specs=in_specs,
        out_specs=pl.BlockSpec((2, l, LANES), lambda j, p: (p, 0, j)),
        out_shape=jax.ShapeDtypeStruct((b, l, HY_WIDTH), out_dtype),
        scratch_shapes=[pltpu.VMEM((n1 * X_PITCH, LANES), F32),
                        pltpu.VMEM((n1 * S_PITCH, LANES), F32),
                        pltpu.VMEM((n1 * X_PITCH, LANES), F32)],
        compiler_params=_cparams(("arbitrary", "arbitrary")),
        name="hy_big",
    )(*args)


def _cis(idx, n):
    ang = idx.astype(F32) * (2.0 * np.pi / n)
    return jnp.cos(ang), jnp.sin(ang)


def _stack_complex(cr, ci):
    return jnp.concatenate([jnp.concatenate([cr, -ci], axis=-1), jnp.concatenate([ci, cr], axis=-1)], axis=-2)


def _dense_dft_tables(l):
    n = 2 * l
    k = jnp.arange(n, dtype=jnp.int32)
    t = jnp.arange(l, dtype=jnp.int32)
    cr, ci = _cis((k[:, None] * t[None, :]) % n, n)
    fwd = _stack_complex(cr, -ci)
    inv = _stack_complex(cr.T, ci.T)
    cfr, cfi = _cis((k[:, None] * k[None, :]) % n, n)
    filt = jnp.concatenate([cfr, -cfi], axis=0)
    return fwd.astype(BF16), inv.astype(BF16), filt


def _two_stage_dft_tables(l):
    n = 2 * l
    n1 = n // LANES
    hn = n1 // 2
    n2 = jnp.arange(LANES, dtype=jnp.int32)[:, None, None]
    k1 = jnp.arange(n1, dtype=jnp.int32)[None, :, None]
    m1 = jnp.arange(n1, dtype=jnp.int32)[None, None, :]
    cr, ci = _cis((k1 * (LANES * m1 + n2)) % n, n)
    fa = _stack_complex(cr[:, :, :hn], -ci[:, :, :hn])
    ga = _stack_complex(jnp.swapaxes(cr, 1, 2)[:, :hn, :], jnp.swapaxes(ci, 1, 2)[:, :hn, :])
    fa_filt = jnp.concatenate([cr, -ci], axis=1)
    a = jnp.arange(LANES, dtype=jnp.int32)
    c2r, c2i = _cis((a[:, None] * a[None, :]) % LANES, LANES)
    f2 = _stack_complex(c2r, -c2i)
    f2i = _stack_complex(c2r, c2i)
    return (fa.astype(BF16), ga.astype(BF16), f2.astype(BF16), f2i.astype(BF16)), fa_filt, f2


def _silu(x):
    return x * jax.nn.sigmoid(x)


def _proj_out_even_body(att_ref, ga_ref, hy_ref, gh_ref, w_ref, x_ref, g_ref, o_ref):
    a = att_ref[0].astype(F32) * _silu(ga_ref[0].astype(F32))
    h = hy_ref[0].astype(F32) * _silu(gh_ref[0].astype(F32))
    mixed = jnp.concatenate([a, h], axis=-1).astype(BF16)
    res = jnp.dot(mixed, w_ref[...], preferred_element_type=F32)
    o_ref[0] = x_ref[0] + g_ref[0] * res


def _proj_out_even(att, hy, u, w_out, x, gate, tm):
    b, l, d = x.shape
    per_batch = gate.shape[0] == b
    gmap = (lambda bi, i: (bi, 0, 0)) if per_batch else (lambda bi, i: (0, 0, 0))
    return pl.pallas_call(
        _proj_out_even_body,
        grid=(b, l // tm),
        in_specs=[pl.BlockSpec((1, tm, ATTN_WIDTH), lambda bi, i: (bi, i, 0)),
                  pl.BlockSpec((1, tm, ATTN_WIDTH), lambda bi, i: (bi, i, U_GA // ATTN_WIDTH)),
                  pl.BlockSpec((1, tm, HY_WIDTH), lambda bi, i: (bi, i, 0)),
                  pl.BlockSpec((1, tm, HY_WIDTH), lambda bi, i: (bi, i, U_GH // HY_WIDTH)),
                  _const_spec(w_out.shape),
                  pl.BlockSpec((1, tm, d), lambda bi, i: (bi, i, 0)),
                  pl.BlockSpec((1, 1, d), gmap)],
        out_specs=pl.BlockSpec((1, tm, d), lambda bi, i: (bi, i, 0)),
        out_shape=jax.ShapeDtypeStruct((b, l, d), F32),
        compiler_params=_cparams(("arbitrary", "arbitrary")),
        name="proj_out_even",
    )(att, u, hy, u, w_out, x, gate)


def _proj_out_odd_body(yf_ref, yb_ref, gt_ref, w_ref, x_ref, g_ref, fg_ref, o_ref, *, final_norm):
    y = yf_ref[0].astype(F32) + yb_ref[0].astype(F32)
    mixed = (y * _silu(gt_ref[0].astype(F32))).astype(BF16)
    res = jnp.dot(mixed, w_ref[...], preferred_element_type=F32)
    xn = x_ref[0] + g_ref[0] * res
    if final_norm:
        ms = jnp.mean(xn * xn, axis=-1, keepdims=True)
        xn = xn * lax.rsqrt(ms + EPS) * fg_ref[...]
    o_ref[0] = xn


def _proj_out_odd(yf, yb, u2, w_out, x, gate, final_g, final_norm, tm):
    b, l, d = x.shape
    per_batch = gate.shape[0] == b
    gmap = (lambda bi, i: (bi, 0, 0)) if per_batch else (lambda bi, i: (0, 0, 0))
    return pl.pallas_call(
        functools.partial(_proj_out_odd_body, final_norm=final_norm),
        grid=(b, l // tm),
        in_specs=[pl.BlockSpec((1, tm, d), lambda bi, i: (bi, i, 0)),
                  pl.BlockSpec((1, tm, d), lambda bi, i: (bi, i, 0)),
                  pl.BlockSpec((1, tm, d), lambda bi, i: (bi, i, 1)),
                  _const_spec(w_out.shape),
                  pl.BlockSpec((1, tm, d), lambda bi, i: (bi, i, 0)),
                  pl.BlockSpec((1, 1, d), gmap),
                  pl.BlockSpec((1, d), lambda bi, i: (0, 0))],
        out_specs=pl.BlockSpec((1, tm, d), lambda bi, i: (bi, i, 0)),
        out_shape=jax.ShapeDtypeStruct((b, l, d), F32),
        compiler_params=_cparams(("arbitrary", "arbitrary")),
        name="proj_out_odd",
    )(yf, yb, u2, w_out, x, gate, final_g.reshape(1, d))


def _rmsnorm_body(x_ref, g_ref, o_ref):
    x = x_ref[0]
    ms = jnp.mean(x * x, axis=-1, keepdims=True)
    o_ref[0] = x * lax.rsqrt(ms + EPS) * g_ref[...]


def _rmsnorm(x, g, tm):
    b, l, d = x.shape
    return pl.pallas_call(
        _rmsnorm_body,
        grid=(b, l // tm),
        in_specs=[pl.BlockSpec((1, tm, d), lambda bi, i: (bi, i, 0)),
                  pl.BlockSpec((1, d), lambda bi, i: (0, 0))],
        out_specs=pl.BlockSpec((1, tm, d), lambda bi, i: (bi, i, 0)),
        out_shape=jax.ShapeDtypeStruct((b, l, d), F32),
        compiler_params=_cparams(("arbitrary", "arbitrary")),
        name="final_rmsnorm",
    )(x, g.reshape(1, d))


LRU_GROUP = 8


LRU_HALO = 2 * LRU_GROUP


def _lru_gates(d, x_ref, xp_ref, xn_ref, keep_prev, keep_next, cw_ref, cb_ref, wa_ref, ba_ref, wx_ref, bx_ref,
               lam_ref, x_scr, a_scr, b_scr):
    tt = x_ref.shape[1]
    g = LRU_GROUP
    nk = LRU_BLOCK // LANES
    for s in range(g):
        for k in range(nk):
            lanes = slice(k * LANES, (k + 1) * LANES)
            x_scr[d, k, pl.ds(LRU_HALO + s, tt, stride=g), :] = x_ref[s, :, lanes].astype(F32)
            x_scr[d, k, pl.ds(s, 1), :] = xp_ref[s, 6:7, lanes].astype(F32) * keep_prev
            x_scr[d, k, pl.ds(g + s, 1), :] = xp_ref[s, 7:8, lanes].astype(F32) * keep_prev
            x_scr[d, k, pl.ds(LRU_HALO + tt * g + s, 1), :] = xn_ref[s, 0:1, lanes].astype(F32) * keep_next
    xcs = []
    for k in range(nk):
        lanes = slice(k * LANES, (k + 1) * LANES)
        acc = cb_ref[:, lanes]
        for j in range(4):
            acc = acc + x_scr[d, k, pl.ds(j * g, tt * g), :] * cw_ref[j:j + 1, lanes]
        xcs.append(acc)
    xc = jnp.concatenate(xcs, axis=1)
    xcb = xc.astype(BF16)
    lam = lam_ref[d:d + 1, :]
    softplus_neg = jnp.maximum(-lam, 0.0) + jnp.log1p(jnp.exp(-jnp.abs(lam)))
    tr = jnp.tanh(jnp.dot(xcb, wa_ref[d, 0], preferred_element_type=F32) + ba_ref[d:d + 1, :])
    ti = jnp.tanh(jnp.dot(xcb, wx_ref[d, 0], preferred_element_type=F32) + bx_ref[d:d + 1, :])
    half_c = (-0.5 * LRU_C) * softplus_neg
    log_a = half_c * tr + half_c
    a = jnp.exp(log_a)
    w4 = jnp.tanh(log_a) * (-0.25 * (a * a) - 0.25)
    root = jnp.where(w4 > 0.0, w4 * lax.rsqrt(w4), 0.0)
    b = root * (ti + 1.0) * xc
    for k in range(nk):
        a_scr[d, k] = a[:, k * LANES:(k + 1) * LANES]
        b_scr[d, k] = b[:, k * LANES:(k + 1) * LANES]


def _lru_body(xf_ref, xfp_ref, xfn_ref, xb_ref, xbp_ref, xbn_ref, cw_ref, cb_ref, wa_ref, ba_ref, wx_ref, bx_ref,
              lam_ref, h0_ref, yf_ref, yb_ref, fin_ref, x_scr, a_scr, b_scr, y_scr, h_scr):
    i = pl.program_id(2)
    nt = pl.num_programs(2)
    tt = xf_ref.shape[1]
    nk = LRU_BLOCK // LANES

    @pl.when(i == 0)
    def _():
        h_scr[...] = h0_ref[...]

    first = jnp.where(i == 0, 0.0, 1.0)
    last = jnp.where(i == nt - 1, 0.0, 1.0)
    params = (cw_ref, cb_ref, wa_ref, ba_ref, wx_ref, bx_ref, lam_ref, x_scr, a_scr, b_scr)
    _lru_gates(0, xf_ref, xfp_ref, xfn_ref, first, last, *params)
    _lru_gates(1, xb_ref, xbp_ref, xbn_ref, last, first, *params)

    def step(t, carry):
        rf = pl.ds(pl.multiple_of(t * LRU_GROUP, LRU_GROUP), LRU_GROUP)
        rb = pl.ds(pl.multiple_of((tt - 1 - t) * LRU_GROUP, LRU_GROUP), LRU_GROUP)
        out = []
        for d, rows in ((0, rf), (1, rb)):
            for k in range(nk):
                h = a_scr[d, k, rows, :] * carry[d * nk + k] + b_scr[d, k, rows, :]
                y_scr[d, k, rows, :] = h
                out.append(h)
        return tuple(out)

    init = tuple(h_scr[d, :, k * LANES:(k + 1) * LANES] for d in range(2) for k in range(nk))
    fin = lax.fori_loop(0, tt, step, init, unroll=16)
    for d in range(2):
        for k in range(nk):
            h_scr[d, :, k * LANES:(k + 1) * LANES] = fin[d * nk + k]

    for s in range(LRU_GROUP):
        for k in range(nk):
            rows = pl.ds(s, tt, stride=LRU_GROUP)
            yf_ref[s, :, k * LANES:(k + 1) * LANES] = y_scr[0, k, rows, :].astype(yf_ref.dtype)
            yb_ref[s, :, k * LANES:(k + 1) * LANES] = y_scr[1, k, rows, :].astype(yb_ref.dtype)

    @pl.when(i == nt - 1)
    def _():
        fin_ref[...] = h_scr[...]


def _lru(u2, conv_w, conv_b, wa, ba, wx, bx, lam, h0):
    b, l, _ = u2.shape
    w = N_HEADS_C * LRU_BLOCK
    hb = LRU_BLOCK
    g = LRU_GROUP
    assert b % g == 0, "the scan packs LRU_GROUP sequences per vreg"
    tt = min(l, LRU_TIME)
    nt = l // tt
    hr = tt // 8
    nhalo = l // 8
    vec = lambda rows: pl.BlockSpec((rows, hb), lambda h, gi, i: (0, h))
    wspec = pl.BlockSpec((2, 1, hb, hb), lambda h, gi, i: (0, h, 0, 0))
    tile = lambda f: pl.BlockSpec((g, tt, hb), lambda h, gi, i: (gi, f(i), h))
    halo_prev = lambda f: pl.BlockSpec((g, 8, hb), lambda h, gi, i: (gi, jnp.maximum(f(i) * hr - 1, 0), h))
    halo_next = lambda f: pl.BlockSpec((g, 8, hb), lambda h, gi, i: (gi, jnp.minimum((f(i) + 1) * hr, nhalo - 1), h))
    fwd = lambda i: i
    bwd = lambda i: nt - 1 - i
    state = pl.BlockSpec((2, g, hb), lambda h, gi, i: (0, gi, h))
    return pl.pallas_call(
        _lru_body,
        grid=(N_HEADS_C, b // g, nt),
        in_specs=[tile(fwd), halo_prev(fwd), halo_next(fwd), tile(bwd), halo_prev(bwd), halo_next(bwd),
                  vec(4), vec(1), wspec, vec(2), wspec, vec(2), vec(2), state],
        out_specs=[tile(fwd), tile(bwd), state],
        out_shape=[jax.ShapeDtypeStruct((b, l, w), BF16),
                   jax.ShapeDtypeStruct((b, l, w), BF16),
                   jax.ShapeDtypeStruct((2, b, w), F32)],
        scratch_shapes=[pltpu.VMEM((2, hb // LANES, (tt + 3) * g, LANES), F32),
                        pltpu.VMEM((2, hb // LANES, tt * g, LANES), F32),
                        pltpu.VMEM((2, hb // LANES, tt * g, LANES), F32),
                        pltpu.VMEM((2, hb // LANES, tt * g, LANES), F32),
                        pltpu.VMEM((2, g, hb), F32)],
        compiler_params=_cparams(("arbitrary", "arbitrary", "arbitrary")),
        name="rg_lru",
    )(u2, u2, u2, u2, u2, u2, conv_w, conv_b.reshape(1, w), wa, ba, wx, bx, lam, h0)


def _even_w_in(w):
    q = w[:, 0:ATTN_WIDTH]
    k = w[:, ATTN_WIDTH:ATTN_WIDTH + KV_COLS]
    v = w[:, ATTN_WIDTH + KV_COLS:ATTN_WIDTH + 2 * KV_COLS]
    o = ATTN_WIDTH + 2 * KV_COLS
    ga = w[:, o:o + ATTN_WIDTH]
    hy = w[:, o + ATTN_WIDTH:o + ATTN_WIDTH + 3 * HY_WIDTH]
    gh = w[:, o + ATTN_WIDTH + 3 * HY_WIDTH:]
    dup = lambda m: jnp.concatenate([m[:, h * HEAD_DIM:(h + 1) * HEAD_DIM] for h in range(N_KV_A) for _ in range(2)],
                                    axis=1)
    return jnp.concatenate([q, ga, hy, gh, dup(k), dup(v)], axis=1).astype(BF16)


def _flat(x):
    return x.reshape(1, -1, x.shape[-1])


def _dup_heads(x):
    return jnp.concatenate([x[..., h, :] for h in range(N_KV_A) for _ in range(2)], axis=-1).astype(BF16)


def kernel(x_prompt, x_sample, cache_k, cache_v, state_lru, c, c_ctx, mod_w, mod_b, norm_g, final_norm_g, a_w_in, a_w_out, a_sink, hy_short_w, hy_short_b, hy_w1, hy_b1, hy_w2, hy_b2, hy_w3, hy_decay, hy_bias, c_w_in, c_w_out, c_conv_w, c_conv_b, c_wa, c_ba, c_wx, c_bx, c_lambda):
    depth, d, _ = mod_w.shape
    bp, lp, _ = x_prompt.shape
    bs, ls, _ = x_sample.shape

    cvec = jnp.concatenate([c, c_ctx[None, :], jnp.zeros((16 - bs - 1, d), F32)], axis=0)
    mod = _modulation(cvec, mod_w, mod_b)

    cos, sin = _rope_tables(ls)
    kx2 = _dup_heads(cache_k)
    vx2 = _dup_heads(cache_v)
    tabs_big, fa_filt, f2_filt = _two_stage_dft_tables(ls)
    fwd_small, inv_small, filt_small = _dense_dft_tables(lp)
    feat_p, feat_s = _hy_features(lp), _hy_features(ls)
    hid_pad = ((0, LANES - hy_w1.shape[-1]),)

    tm_s = min(ls, ROWS_NARROW_COLS)
    tm_o = min(ls, ROWS_OUT)
    tm_p = min(bp * lp, ROWS_OUT)
    xp, xs = x_prompt, x_sample
    new_k, new_v, new_s = [], [], []
    for layer in range(depth):
        j = layer // 2
        sh_s, sc_s, g_s = (mod[layer, :bs, None, i * d:(i + 1) * d] for i in range(3))
        sh_p, sc_p, g_p = (mod[layer, bs:bs + 1, None, i * d:(i + 1) * d] for i in range(3))
        last = layer == depth - 1
        if layer % 2 == 0:
            w_in = _even_w_in(a_w_in[j])
            w_out = a_w_out[j].astype(BF16)
            up = _proj_in_flat(xp, norm_g[layer], sc_p, sh_p, w_in, ROWS_WIDE_COLS, 2)
            us = _proj_in(xs, norm_g[layer], sc_s, sh_s, w_in, min(ls, ROWS_WIDE_COLS), 2)
            att_p, k_p, v_p = _attn_ctx(up, a_sink[j])
            new_k.append(k_p.reshape(bp, lp, N_KV_A, HEAD_DIM))
            new_v.append(v_p.reshape(bp, lp, N_KV_A, HEAD_DIM))
            k2r = _rope_k(us, cos, sin, min(ls, ROWS_ROPE))
            att_s = _attn_lat(us, k2r, kx2, vx2, j, a_sink[j], cos, sin)

            w1 = jnp.pad(hy_w1[j], ((0, LANES - HY_EMB),) + hid_pad)
            b1 = jnp.pad(hy_b1[j][None, :], ((0, 0),) + hid_pad)
            w2 = jnp.pad(hy_w2[j], hid_pad + hid_pad)
            b2 = jnp.pad(hy_b2[j][None, :], ((0, 0),) + hid_pad)
            w3 = jnp.pad(hy_w3[j], hid_pad + ((0, 0),))
            decay = hy_decay[j].reshape(2 * HY_ORDER, 1, HY_WIDTH)
            sw, sb = hy_short_w[j], hy_short_b[j][None, :]
            hspec_p = _hy_filter_small(_hy_hidden(feat_p, w1, b1, w2, b2), w3, decay, filt_small, lp)
            hspec_s = _hy_filter_big(_hy_hidden(feat_s, w1, b1, w2, b2), w3, decay, fa_filt, f2_filt, ls)
            hy_p = _hy_small(up, sw, sb, hspec_p, hy_bias[j], fwd_small, inv_small)
            z1 = _hy_big(us, U_Z, 2 * HY_WIDTH, us, U_X1, 0, sw, sb, hspec_s, 0, hy_bias[j], tabs_big, F32)
            hy_s = _hy_big(z1, 0, None, us, U_X2, HY_WIDTH, sw, sb, hspec_s, 1, hy_bias[j], tabs_big, BF16)

            xp = _proj_out_even(_flat(att_p), _flat(hy_p), _flat(up), w_out, _flat(xp), g_p, tm_p).reshape(bp, lp, d)
            xs = _proj_out_even(att_s, hy_s, us, w_out, xs, g_s, tm_o)
            if last:
                xp = _rmsnorm(xp, final_norm_g, lp)
                xs = _rmsnorm(xs, final_norm_g, tm_o)
        else:
            w_in = c_w_in[j].astype(BF16)
            w_out = c_w_out[j].astype(BF16)
            wa = (0.5 * c_wa[j]).astype(BF16)
            wx = (0.5 * c_wx[j]).astype(BF16)
            up = _proj_in_flat(xp, norm_g[layer], sc_p, sh_p, w_in, ROWS_NARROW_COLS, 2)
            us = _proj_in(xs, norm_g[layer], sc_s, sh_s, w_in, tm_s, 2)
            lru_args = (c_conv_w[j], c_conv_b[j], wa, 0.5 * c_ba[j], wx, 0.5 * c_bx[j], c_lambda[j])
            yf_p, yb_p, fin_p = _lru(up, *lru_args, jnp.zeros((2, bp, d), F32))
            yf_s, yb_s, _ = _lru(us, *lru_args, jnp.moveaxis(state_lru[:, j], 1, 0))
            new_s.append(jnp.moveaxis(fin_p, 0, 1))
            xp = _proj_out_odd(_flat(yf_p), _flat(yb_p), _flat(up), w_out, _flat(xp), g_p, final_norm_g, last,
                               tm_p).reshape(bp, lp, d)
            xs = _proj_out_odd(yf_s, yb_s, us, w_out, xs, g_s, final_norm_g, last, tm_o)

    return (xp, xs, jnp.stack(new_k, axis=1), jnp.stack(new_v, axis=1), jnp.stack(new_s, axis=1))
```

```python
import functools

import numpy as np
import jax
import jax.numpy as jnp
from jax import lax
from jax.experimental import pallas as pl
from jax.experimental.pallas import tpu as pltpu

F32 = jnp.float32
BF16 = jnp.bfloat16
HIGHEST = lax.Precision.HIGHEST

EPS = 1e-6
HEAD_DIM = 64
N_HEADS_A = 16
N_KV_A = 2
GQA_GROUP = N_HEADS_A // N_KV_A
ATTN_WIDTH = N_HEADS_A * HEAD_DIM
KV_COLS = N_KV_A * HEAD_DIM
WINDOW = 128
BLOCK = 128
GRID_W = 64
ROPE_BASE = 10000.0
HY_WIDTH = 1024
HY_ORDER = 2
HY_BANDS = 16
HY_EMB = 2 * HY_BANDS + 1
N_HEADS_C = 8
LRU_BLOCK = 256
LRU_C = 8.0
ATTN_SCALE = HEAD_DIM ** -0.5
NEG = float(np.finfo(np.float32).min)

LANES = 128
X_PITCH = LANES + 8
S_PITCH = 2 * LANES + 8
FFT_SIDE = 4
FFT_UNROLL = 128
FFT_GROUPS = 16

U_Q, U_GA, U_X1, U_X2, U_Z, U_GH, U_K2, U_V2 = 0, 1024, 2048, 3072, 4096, 5120, 6144, 6400
U_COLS = 6656

VMEM_LIMIT = 56 * 1024 * 1024

ROWS_WIDE_COLS = 512
ROWS_NARROW_COLS = 1024
ROWS_OUT = 512
ROWS_ROPE = 2048
LRU_TIME = 512


def _cparams(sem, vmem=VMEM_LIMIT):
    return pltpu.CompilerParams(dimension_semantics=sem, vmem_limit_bytes=vmem)


def _const_spec(shape):
    nd = len(shape)
    return pl.BlockSpec(shape, lambda *_: (0,) * nd, pipeline_mode=pl.Buffered(1))


def _mod_body(c_ref, w_ref, b_ref, o_ref):
    c = c_ref[...]
    s = c * jax.nn.sigmoid(c)
    o_ref[0] = jnp.dot(s.astype(BF16), w_ref[0].astype(BF16), preferred_element_type=F32) + b_ref[0]


def _modulation(cvec, mod_w, mod_b):
    depth, d, n3 = mod_w.shape
    rows = cvec.shape[0]
    tn = 512
    return pl.pallas_call(
        _mod_body,
        grid=(depth, n3 // tn),
        in_specs=[pl.BlockSpec((rows, d), lambda l, j: (0, 0)),
                  pl.BlockSpec((1, d, tn), lambda l, j: (l, 0, j)),
                  pl.BlockSpec((1, 1, tn), lambda l, j: (l, 0, j))],
        out_specs=pl.BlockSpec((1, rows, tn), lambda l, j: (l, 0, j)),
        out_shape=jax.ShapeDtypeStruct((depth, rows, n3), F32),
        compiler_params=_cparams(("arbitrary", "arbitrary")),
        name="modulation",
    )(cvec, mod_w, mod_b.reshape(depth, 1, n3))


NORM_ROWS = 16
DOT_COLS = 512


def _col_chunks(tn):
    edges = list(range(0, tn, DOT_COLS)) + [tn]
    if len(edges) > 2 and edges[-1] - edges[-2] < 256:
        del edges[-2]
    return list(zip(edges[:-1], edges[1:]))


def _proj_in_body(x_ref, g_ref, sc_ref, sh_ref, w_ref, o_ref, h_scr):
    @pl.when(pl.program_id(2) == 0)
    def _():
        gain = g_ref[...]
        scale1 = 1.0 + sc_ref[0]
        shift = sh_ref[0]

        def rows(r, carry):
            sl = pl.ds(pl.multiple_of(r * NORM_ROWS, NORM_ROWS), NORM_ROWS)
            x = x_ref[0, sl, :]
            ms = jnp.mean(x * x, axis=-1, keepdims=True)
            y = x * lax.rsqrt(ms + EPS) * gain
            h_scr[sl, :] = (y * scale1 + shift).astype(BF16)
            return carry

        lax.fori_loop(0, x_ref.shape[1] // NORM_ROWS, rows, 0, unroll=8)

    for c0, c1 in _col_chunks(w_ref.shape[1]):
        o_ref[0, :, c0:c1] = jnp.dot(h_scr[...], w_ref[:, c0:c1], preferred_element_type=F32).astype(o_ref.dtype)


def _proj_in(x, norm_g, scale, shift, w, tm, n_col_tiles):
    b, l, d = x.shape
    n = w.shape[1]
    tn = n // n_col_tiles
    assert tn * n_col_tiles == n and tn % LANES == 0
    per_batch = scale.shape[0] == b
    mod_map = (lambda bi, i, j: (bi, 0, 0)) if per_batch else (lambda bi, i, j: (0, 0, 0))
    return pl.pallas_call(
        _proj_in_body,
        grid=(b, l // tm, n // tn),
        in_specs=[pl.BlockSpec((1, tm, d), lambda bi, i, j: (bi, i, 0)),
                  pl.BlockSpec((1, d), lambda bi, i, j: (0, 0)),
                  pl.BlockSpec((1, 1, d), mod_map),
                  pl.BlockSpec((1, 1, d), mod_map),
                  pl.BlockSpec((d, tn), lambda bi, i, j: (0, j))],
        out_specs=pl.BlockSpec((1, tm, tn), lambda bi, i, j: (bi, i, j)),
        out_shape=jax.ShapeDtypeStruct((b, l, n), BF16),
        scratch_shapes=[pltpu.VMEM((tm, d), BF16)],
        compiler_params=_cparams(("arbitrary", "arbitrary", "arbitrary")),
        name="proj_in",
    )(x, norm_g.reshape(1, d), scale, shift, w)


def _proj_in_flat(x, norm_g, scale, shift, w, rows, n_col_tiles):
    b, l, d = x.shape
    assert scale.shape[0] == 1
    tm = min(b * l, rows)
    out = _proj_in(x.reshape(1, b * l, d), norm_g, scale, shift, w, tm, n_col_tiles)
    return out.reshape(b, l, w.shape[1])


ATTN_SLABS = 4


def _attn_heads(q, k2, v2, sink_ref, masks, o_ref):
    t = q.shape[0]
    left = lax.broadcasted_iota(jnp.int32, (t, LANES), 1) < HEAD_DIM
    top = lax.broadcasted_iota(jnp.int32, (LANES, t), 0) < HEAD_DIM
    zero = jnp.zeros((t, LANES), q.dtype)
    for j0 in range(0, ATTN_WIDTH // LANES, ATTN_SLABS):
        kv = (2 * j0) // GQA_GROUP
        kk = k2[:, LANES * kv:LANES * (kv + 1)]
        vv = v2[:, LANES * kv:LANES * (kv + 1)]
        blocks = []
        for j in range(j0, j0 + ATTN_SLABS):
            qs = q[:, LANES * j:LANES * (j + 1)]
            blocks += [jnp.where(left, qs, zero), jnp.where(left, zero, qs)]
        qstack = jnp.concatenate(blocks, axis=0)
        s_t = lax.dot_general(kk, qstack, (((1,), (1,)), ((), ())), preferred_element_type=F32)
        ps, dens = [], []
        for bi in range(2 * ATTN_SLABS):
            s = s_t[:, bi * t:(bi + 1) * t]
            pieces, pos = [], 0
            for r0, r1, mk in masks:
                if r0 > pos:
                    pieces.append(s[pos:r0])
                pieces.append(jnp.where(mk, s[r0:r1], NEG))
                pos = r1
            if pieces:
                if pos < s.shape[0]:
                    pieces.append(s[pos:])
                s = jnp.concatenate(pieces, axis=0)
            sink = sink_ref[2 * j0 + bi]
            m = jnp.maximum(jnp.max(s, axis=0, keepdims=True), sink)
            p = jnp.exp(s - m)
            dens.append(jnp.sum(p, axis=0, keepdims=True) + jnp.exp(sink - m))
            ps.append(p.astype(BF16))
        p_t = jnp.concatenate(ps, axis=1)
        o_t = lax.dot_general(vv, p_t, (((0,), (0,)), ((), ())), preferred_element_type=F32)
        o_t = o_t / jnp.concatenate(dens, axis=1)
        for jj in range(ATTN_SLABS):
            comb = jnp.where(top, o_t[:, (2 * jj) * t:(2 * jj + 1) * t], o_t[:, (2 * jj + 1) * t:(2 * jj + 2) * t])
            o_ref[0, :, LANES * (j0 + jj):LANES * (j0 + jj + 1)] = comb.T.astype(o_ref.dtype)


def _attn_ctx_body(sink_ref, q_ref, k_ref, v_ref, o_ref, ko_ref, vo_ref):
    k2, v2 = k_ref[0], v_ref[0]
    _attn_heads(q_ref[0] * ATTN_SCALE, k2, v2, sink_ref, (), o_ref)
    left = lax.broadcasted_iota(jnp.int32, (k2.shape[0], LANES), 1) < HEAD_DIM
    ko_ref[0] = jnp.where(left, k2[:, :LANES], k2[:, LANES:]).astype(ko_ref.dtype)
    vo_ref[0] = jnp.where(left, v2[:, :LANES], v2[:, LANES:]).astype(vo_ref.dtype)


def _attn_ctx(u, sink):
    b, t, _ = u.shape
    kv_spec = pl.BlockSpec((1, t, KV_COLS), lambda bi: (bi, 0, 0))
    return pl.pallas_call(
        _attn_ctx_body,
        grid=(b,),
        in_specs=[pl.BlockSpec(memory_space=pltpu.SMEM),
                  pl.BlockSpec((1, t, ATTN_WIDTH), lambda bi: (bi, 0, U_Q // ATTN_WIDTH)),
                  pl.BlockSpec((1, t, 256), lambda bi: (bi, 0, U_K2 // 256)),
                  pl.BlockSpec((1, t, 256), lambda bi: (bi, 0, U_V2 // 256))],
        out_specs=[pl.BlockSpec((1, t, ATTN_WIDTH), lambda bi: (bi, 0, 0)), kv_spec, kv_spec],
        out_shape=[jax.ShapeDtypeStruct((b, t, ATTN_WIDTH), BF16),
                   jax.ShapeDtypeStruct((b, t, KV_COLS), F32),
                   jax.ShapeDtypeStruct((b, t, KV_COLS), F32)],
        compiler_params=_cparams(("arbitrary",)),
        name="attn_ctx",
    )(sink, u, u, u)


def _rope_slab(x, cos, sin):
    lane = lax.broadcasted_iota(jnp.int32, x.shape, 1)
    lo = (lane & 16) == 0
    partner = jnp.where(lo, pltpu.roll(x, LANES - 16, 1), pltpu.roll(x, 16, 1))
    return x * cos + partner * sin


def _rope_k_body(k_ref, cos_ref, sin_ref, o_ref):
    cos, sin = cos_ref[...], sin_ref[...]
    for s in range(2):
        x = k_ref[0, :, LANES * s:LANES * (s + 1)].astype(F32)
        o_ref[0, :, LANES * s:LANES * (s + 1)] = _rope_slab(x, cos, sin).astype(o_ref.dtype)


def _rope_k(u, cos, sin, tl):
    b, l, _ = u.shape
    return pl.pallas_call(
        _rope_k_body,
        grid=(b, l // tl),
        in_specs=[pl.BlockSpec((1, tl, 256), lambda bi, i: (bi, i, U_K2 // 256)),
                  pl.BlockSpec((tl, LANES), lambda bi, i: (i, 0)),
                  pl.BlockSpec((tl, LANES), lambda bi, i: (i, 0))],
        out_specs=pl.BlockSpec((1, tl, 256), lambda bi, i: (bi, i, 0)),
        out_shape=jax.ShapeDtypeStruct((b, l, 256), BF16),
        compiler_params=_cparams(("arbitrary", "arbitrary")),
        name="rope_k",
    )(u, cos, sin)


def _attn_lat_body(sink_ref, q_ref, cos_ref, sin_ref, kp_ref, kc_ref, kn_ref, vp_ref, vc_ref, vn_ref,
                   kx_ref, vx_ref, o_ref, q_scr):
    i = pl.program_id(1)
    nblk = pl.num_programs(1)
    cos, sin = cos_ref[...], sin_ref[...]
    for j in range(ATTN_WIDTH // LANES):
        x = q_ref[0, :, LANES * j:LANES * (j + 1)].astype(F32)
        q_scr[:, LANES * j:LANES * (j + 1)] = (_rope_slab(x, cos, sin) * ATTN_SCALE).astype(BF16)
    k2 = jnp.concatenate([kx_ref[0, 0], kp_ref[0], kc_ref[0], kn_ref[0]], axis=0)
    v2 = jnp.concatenate([vx_ref[0, 0], vp_ref[0], vc_ref[0], vn_ref[0]], axis=0)
    tc = kx_ref.shape[2]
    diff = (lax.broadcasted_iota(jnp.int32, (BLOCK, BLOCK), 0)
            - lax.broadcasted_iota(jnp.int32, (BLOCK, BLOCK), 1))
    prev_ok = diff >= jnp.where(i > 0, 0, 2 * BLOCK)
    next_ok = diff <= jnp.where(i < nblk - 1, 0, -2 * BLOCK)
    masks = ((tc, tc + BLOCK, prev_ok), (tc + 2 * BLOCK, tc + 3 * BLOCK, next_ok))
    _attn_heads(q_scr[...], k2, v2, sink_ref, masks, o_ref)


def _attn_lat(u, k2r, kx2, vx2, layer, sink, cos, sin):
    b, l, _ = u.shape
    nblk = l // BLOCK
    tc = kx2.shape[2]
    prev = lambda bi, i: (bi, jnp.maximum(i - 1, 0), 0)
    cur = lambda bi, i: (bi, i, 0)
    nxt = lambda bi, i: (bi, jnp.minimum(i + 1, nblk - 1), 0)
    voff = U_V2 // 256
    vprev = lambda bi, i: (bi, jnp.maximum(i - 1, 0), voff)
    vcur = lambda bi, i: (bi, i, voff)
    vnxt = lambda bi, i: (bi, jnp.minimum(i + 1, nblk - 1), voff)
    return pl.pallas_call(
        _attn_lat_body,
        grid=(b, nblk),
        in_specs=[pl.BlockSpec(memory_space=pltpu.SMEM),
                  pl.BlockSpec((1, BLOCK, ATTN_WIDTH), lambda bi, i: (bi, i, U_Q // ATTN_WIDTH)),
                  pl.BlockSpec((BLOCK, LANES), lambda bi, i: (i, 0)),
                  pl.BlockSpec((BLOCK, LANES), lambda bi, i: (i, 0)),
                  pl.BlockSpec((1, BLOCK, 256), prev),
                  pl.BlockSpec((1, BLOCK, 256), cur),
                  pl.BlockSpec((1, BLOCK, 256), nxt),
                  pl.BlockSpec((1, BLOCK, 256), vprev),
                  pl.BlockSpec((1, BLOCK, 256), vcur),
                  pl.BlockSpec((1, BLOCK, 256), vnxt),
                  pl.BlockSpec((1, 1, tc, 256), lambda bi, i: (bi, layer, 0, 0)),
                  pl.BlockSpec((1, 1, tc, 256), lambda bi, i: (bi, layer, 0, 0))],
        out_specs=pl.BlockSpec((1, BLOCK, ATTN_WIDTH), lambda bi, i: (bi, i, 0)),
        out_shape=jax.ShapeDtypeStruct((b, l, ATTN_WIDTH), BF16),
        scratch_shapes=[pltpu.VMEM((BLOCK, ATTN_WIDTH), BF16)],
        compiler_params=_cparams(("arbitrary", "arbitrary")),
        name="attn_lat",
    )(sink, u, cos, sin, k2r, k2r, k2r, u, u, u, kx2, vx2)


def _rope_tables(l):
    nf = HEAD_DIM // 4
    inv = ROPE_BASE ** (-jnp.arange(nf, dtype=F32) / nf)
    tok = jnp.arange(l)
    row = (tok // GRID_W).astype(F32)
    col = (tok % GRID_W).astype(F32)
    d = np.arange(LANES) % HEAD_DIM
    use_col = (d // (HEAD_DIM // 2)) == 1
    lo = (d % (HEAD_DIM // 2)) < nf
    pos = jnp.where(use_col[None, :], col[:, None], row[:, None])
    ang = pos * inv[d % nf][None, :]
    sign = np.where(lo, -1.0, 1.0).astype(np.float32)
    return jnp.cos(ang), jnp.sin(ang) * sign[None, :]


def _dot_split(a, b):
    ah = a.astype(BF16)
    bh = b.astype(BF16)
    al = (a - ah.astype(F32)).astype(BF16)
    bl = (b - bh.astype(F32)).astype(BF16)
    dot = functools.partial(jnp.dot, preferred_element_type=F32)
    return dot(ah, bh) + (dot(ah, bl) + dot(al, bh))


def _dot_table(t, x):
    th = t.astype(BF16)
    xh = x.astype(BF16)
    xl = (x - xh.astype(F32)).astype(BF16)
    return jnp.dot(th, xh, preferred_element_type=F32) + jnp.dot(th, xl, preferred_element_type=F32)


def _hy_hidden_body(z_ref, w1_ref, b1_ref, w2_ref, b2_ref, o_ref):
    h = jnp.sin(jnp.dot(z_ref[...], w1_ref[...], precision=HIGHEST, preferred_element_type=F32) + b1_ref[...])
    o_ref[...] = jnp.sin(jnp.dot(h, w2_ref[...], precision=HIGHEST, preferred_element_type=F32) + b2_ref[...])


def _hy_hidden(zfeat, w1, b1, w2, b2):
    rows = zfeat.shape[0]
    return pl.pallas_call(
        _hy_hidden_body,
        out_shape=jax.ShapeDtypeStruct((rows, LANES), F32),
        compiler_params=pltpu.CompilerParams(vmem_limit_bytes=VMEM_LIMIT),
        name="hy_hidden",
    )(zfeat, w1, b1, w2, b2)


def _hy_features(l):
    pos = jnp.arange(l, dtype=F32)
    t = pos / l
    freqs = jnp.linspace(1e-4, HY_BANDS - 1, HY_BANDS, dtype=F32)
    ang = 2.0 * jnp.pi * t[:, None] * freqs[None, :]
    z = jnp.concatenate([t[:, None], jnp.cos(ang), -jnp.sin(ang)], axis=-1)
    zrev = jnp.concatenate([z[:1], z[:0:-1]], axis=0)
    zz = jnp.concatenate([z, zrev], axis=0)
    return jnp.pad(zz, ((0, 0), (0, LANES - HY_EMB)))


def _hy_taps(hid_ref, w3f_ref, w3b_ref, decf_ref, decb_ref, l):
    ct = w3f_ref.shape[1]
    hf = _dot_split(hid_ref[0:l, :], w3f_ref[...])
    hb = _dot_split(hid_ref[l:2 * l, :], w3b_ref[...])
    row = lax.broadcasted_iota(jnp.int32, (l, ct), 0)
    t = row.astype(F32) / l
    trev = (l - row).astype(F32) / l
    hf = hf * jnp.exp(-t * jnp.abs(decf_ref[0]))
    hb = jnp.where(row == 0, 0.0, hb * jnp.exp(-trev * jnp.abs(decb_ref[0])))
    nsq = jnp.sum(hf * hf, axis=0, keepdims=True) + jnp.sum(hb * hb, axis=0, keepdims=True) + EPS
    sc = lax.rsqrt(nsq)
    return hf * sc, hb * sc


def _hy_filter_small_body(hid_ref, w3f_ref, w3b_ref, decf_ref, decb_ref, f_ref, o_ref, *, l):
    hf, hb = _hy_taps(hid_ref, w3f_ref, w3b_ref, decf_ref, decb_ref, l)
    taps = jnp.concatenate([hf, hb], axis=0)
    o_ref[0] = jnp.dot(f_ref[...], taps, precision=HIGHEST, preferred_element_type=F32) * (1.0 / (2 * l))


def _hy_filter_specs(ct):
    nct = HY_WIDTH // ct
    return [pl.BlockSpec((LANES, ct), lambda n, j: (0, n * nct + j)),
            pl.BlockSpec((LANES, ct), lambda n, j: (0, (HY_ORDER + n) * nct + j)),
            pl.BlockSpec((1, 1, ct), lambda n, j: (n, 0, j)),
            pl.BlockSpec((1, 1, ct), lambda n, j: (HY_ORDER + n, 0, j))]


def _hy_filter_small(hid, w3, decay, fmat, l):
    ct = 256
    return pl.pallas_call(
        functools.partial(_hy_filter_small_body, l=l),
        grid=(HY_ORDER, HY_WIDTH // ct),
        in_specs=[_const_spec((2 * l, LANES))] + _hy_filter_specs(ct) + [_const_spec(fmat.shape)],
        out_specs=pl.BlockSpec((1, 4 * l, ct), lambda n, j: (n, 0, j)),
        out_shape=jax.ShapeDtypeStruct((HY_ORDER, 4 * l, HY_WIDTH), F32),
        compiler_params=_cparams(("arbitrary", "arbitrary")),
        name="hy_filter_small",
    )(hid, w3, w3, decay, decay, fmat)


def _hy_filter_big_body(hid_ref, w3f_ref, w3b_ref, decf_ref, decb_ref, fa_ref, f2_ref, o_ref, xs, ss, *, l):
    n1 = 2 * l // LANES
    hf, hb = _hy_taps(hid_ref, w3f_ref, w3b_ref, decf_ref, decb_ref, l)
    for s in range(n1 // 2):
        xs[pl.ds(s * X_PITCH, LANES), :] = hf[s * LANES:(s + 1) * LANES]
        xs[pl.ds((n1 // 2 + s) * X_PITCH, LANES), :] = hb[s * LANES:(s + 1) * LANES]

    def stage_a(n2, carry):
        rhs = xs[pl.ds(n2, n1, stride=X_PITCH), :]
        out = _dot_table(fa_ref[n2], rhs)
        ss[pl.ds(n2, n1, stride=S_PITCH), :] = out[:n1]
        ss[pl.ds(LANES + n2, n1, stride=S_PITCH), :] = out[n1:]
        return carry

    lax.fori_loop(0, LANES, stage_a, 0, unroll=FFT_UNROLL // 2)

    side = min(FFT_SIDE, n1)

    def stage_b(kg, carry):
        slabs = [ss[pl.ds(pl.multiple_of((side * kg + e) * S_PITCH, 8), 2 * LANES), :] for e in range(side)]
        out = _dot_table(f2_ref[...], jnp.concatenate(slabs, axis=1)) * (1.0 / (2 * l))
        for e in range(side):
            rows = pl.ds(pl.multiple_of((side * kg + e) * 2 * LANES, 2 * LANES), 2 * LANES)
            o_ref[0, rows, :] = out[:, e * LANES:(e + 1) * LANES]
        return carry

    lax.fori_loop(0, n1 // side, stage_b, 0, unroll=2)


def _hy_filter_big(hid, w3, decay, fa, f2, l):
    n1 = 2 * l // LANES
    return pl.pallas_call(
        functools.partial(_hy_filter_big_body, l=l),
        grid=(HY_ORDER, HY_WIDTH // LANES),
        in_specs=[_const_spec((2 * l, LANES))] + _hy_filter_specs(LANES)
        + [_const_spec(fa.shape), _const_spec(f2.shape)],
        out_specs=pl.BlockSpec((1, 4 * l, LANES), lambda n, j: (n, 0, j)),
        out_shape=jax.ShapeDtypeStruct((HY_ORDER, 4 * l, HY_WIDTH), F32),
        scratch_shapes=[pltpu.VMEM((n1 * X_PITCH, LANES), F32),
                        pltpu.VMEM((n1 * S_PITCH, LANES), F32)],
        compiler_params=_cparams(("arbitrary", "arbitrary")),
        name="hy_filter_big",
    )(hid, w3, w3, decay, decay, fa, f2)


def _shift_rows(x, k):
    l = x.shape[0]
    y = pltpu.roll(x, k % l, 0)
    row = lax.broadcasted_iota(jnp.int32, (8, x.shape[1]), 0)
    if k > 0:
        return jnp.concatenate([jnp.where(row < k, 0.0, y[:8]), y[8:]], axis=0)
    return jnp.concatenate([y[:l - 8], jnp.where(row >= 8 + k, 0.0, y[l - 8:])], axis=0)


def _short_conv(x, w_ref, b_ref):
    prev = _shift_rows(x, 1)
    nxt = _shift_rows(x, -1)
    return prev * w_ref[0:1, :] + x * w_ref[1:2, :] + nxt * w_ref[2:3, :] + b_ref[...]


def _hy_small_body(uz_ref, ux1_ref, ux2_ref, wz_ref, wx1_ref, wx2_ref, bz_ref, bx1_ref, bx2_ref,
                   h_ref, bias_ref, f_ref, g_ref, o_ref):
    l = uz_ref.shape[1]
    n = 2 * l
    sig = [_short_conv(uz_ref[b].astype(F32), wz_ref, bz_ref) for b in range(2)]
    gates = [[_short_conv(ux1_ref[b].astype(F32), wx1_ref, bx1_ref) for b in range(2)],
             [_short_conv(ux2_ref[b].astype(F32), wx2_ref, bx2_ref) for b in range(2)]]
    for o in range(HY_ORDER):
        x = jnp.concatenate(sig, axis=0).astype(BF16)
        a = jnp.dot(f_ref[...], x, preferred_element_type=F32)
        ar, ai = a[:n], a[n:]
        hr, hi = h_ref[o, 0:n, :], h_ref[o, n:2 * n, :]
        y = jnp.concatenate([ar * hr - ai * hi, ar * hi + ai * hr], axis=0).astype(BF16)
        conv = jnp.dot(g_ref[...], y, preferred_element_type=F32)
        bias = bias_ref[o:o + 1, :]
        sig = [gates[o][b] * (conv[b * l:(b + 1) * l] + bias * sig[b]) for b in range(2)]
    for b in range(2):
        o_ref[b] = sig[b].astype(o_ref.dtype)


def _hy_small(u, short_w, short_b, hspec, hy_bias, fmat, gmat):
    b, l, _ = u.shape
    ct = 256
    col = lambda off: (lambda j, p: (p, 0, off // ct + j))
    wcol = lambda off: (lambda j, p: (0, off // ct + j))
    return pl.pallas_call(
        _hy_small_body,
        grid=(HY_WIDTH // ct, b // 2),
        in_specs=[pl.BlockSpec((2, l, ct), col(U_Z)),
                  pl.BlockSpec((2, l, ct), col(U_X1)),
                  pl.BlockSpec((2, l, ct), col(U_X2)),
                  pl.BlockSpec((3, ct), wcol(2 * HY_WIDTH)),
                  pl.BlockSpec((3, ct), wcol(0)),
                  pl.BlockSpec((3, ct), wcol(HY_WIDTH)),
                  pl.BlockSpec((1, ct), wcol(2 * HY_WIDTH)),
                  pl.BlockSpec((1, ct), wcol(0)),
                  pl.BlockSpec((1, ct), wcol(HY_WIDTH)),
                  pl.BlockSpec((HY_ORDER, 4 * l, ct), lambda j, p: (0, 0, j)),
                  pl.BlockSpec((HY_ORDER, ct), lambda j, p: (0, j)),
                  _const_spec(fmat.shape),
                  _const_spec(gmat.shape)],
        out_specs=pl.BlockSpec((2, l, ct), lambda j, p: (p, 0, j)),
        out_shape=jax.ShapeDtypeStruct((b, l, HY_WIDTH), BF16),
        compiler_params=_cparams(("arbitrary", "arbitrary")),
        name="hy_small",
    )(u, u, u, short_w, short_w, short_w, short_b, short_b, short_b, hspec, hy_bias, fmat, gmat)


def _hy_big_body(*refs, conv_sig, l):
    if conv_sig:
        (sig_ref, ws_ref, bs_ref, gate_ref, wg_ref, bg_ref, h_ref, bias_ref,
         fa_ref, ga_ref, f2_ref, f2i_ref, o_ref, xs, ss, ys) = refs
    else:
        (sig_ref, gate_ref, wg_ref, bg_ref, h_ref, bias_ref,
         fa_ref, ga_ref, f2_ref, f2i_ref, o_ref, xs, ss, ys) = refs
    n1 = 2 * l // LANES
    hn = n1 // 2

    for b in range(2):
        x = sig_ref[b].astype(F32)
        if conv_sig:
            x = _short_conv(x, ws_ref, bs_ref)
        for s in range(hn):
            xs[pl.ds((b * hn + s) * X_PITCH, LANES), :] = x[s * LANES:(s + 1) * LANES]

    def stage_a(n2, carry):
        xr = xs[pl.ds(n2, hn, stride=X_PITCH), :]
        xi = xs[pl.ds(hn * X_PITCH + n2, hn, stride=X_PITCH), :]
        rhs = jnp.concatenate([xr, xi], axis=0).astype(BF16)
        out = jnp.dot(fa_ref[n2], rhs, preferred_element_type=F32)
        ss[pl.ds(n2, n1, stride=S_PITCH), :] = out[:n1]
        ss[pl.ds(LANES + n2, n1, stride=S_PITCH), :] = out[n1:]
        return carry

    lax.fori_loop(0, LANES, stage_a, 0, unroll=FFT_UNROLL)

    side = 2
    groups = min(FFT_GROUPS, n1 // side)

    def stage_b(kg, carry):
        k1s = [[(kg * groups + p) * side + e for e in range(side)] for p in range(groups)]
        slabs = [jnp.concatenate([ss[pl.ds(pl.multiple_of(k1 * S_PITCH, 8), 2 * LANES), :] for k1 in grp],
                                 axis=1).astype(BF16) for grp in k1s]
        backs = []
        for p in range(groups):
            x = jnp.dot(f2_ref[...], slabs[p], preferred_element_type=F32)
            hbs = [pl.multiple_of(k1 * 2 * LANES, 2 * LANES) for k1 in k1s[p]]
            hr = jnp.concatenate([h_ref[0, pl.ds(hb, LANES), :] for hb in hbs], axis=1)
            hi = jnp.concatenate([h_ref[0, pl.ds(hb + LANES, LANES), :] for hb in hbs], axis=1)
            xr, xi = x[:LANES], x[LANES:]
            y = jnp.concatenate([xr * hr - xi * hi, xr * hi + xi * hr], axis=0).astype(BF16)
            backs.append(jnp.dot(f2i_ref[...], y, preferred_element_type=F32))
        for p in range(groups):
            for e in range(side):
                ss[pl.ds(pl.multiple_of(k1s[p][e] * S_PITCH, 8), 2 * LANES), :] = backs[p][:, e * LANES:(e + 1) * LANES]
        return carry

    lax.fori_loop(0, n1 // (side * groups), stage_b, 0)

    def stage_a_inv(n2, carry):
        br = ss[pl.ds(n2, n1, stride=S_PITCH), :]
        bi = ss[pl.ds(LANES + n2, n1, stride=S_PITCH), :]
        rhs = jnp.concatenate([br, bi], axis=0).astype(BF16)
        out = jnp.dot(ga_ref[n2], rhs, preferred_element_type=F32)
        ys[pl.ds(n2, hn, stride=X_PITCH), :] = out[:hn]
        ys[pl.ds(hn * X_PITCH + n2, hn, stride=X_PITCH), :] = out[hn:]
        return carry

    lax.fori_loop(0, LANES, stage_a_inv, 0, unroll=FFT_UNROLL)

    bias = bias_ref[...]
    for b in range(2):
        g = _short_conv(gate_ref[b].astype(F32), wg_ref, bg_ref)
        for s in range(hn):
            rows = pl.ds((b * hn + s) * X_PITCH, LANES)
            val = g[s * LANES:(s + 1) * LANES] * (ys[rows, :] + bias * xs[rows, :])
            o_ref[b, s * LANES:(s + 1) * LANES, :] = val.astype(o_ref.dtype)


def _hy_big(sig, sig_off, sig_w, gate, gate_off, gate_woff, short_w, short_b, hspec, order, hy_bias,
            tables, out_dtype):
    b, l, _ = gate.shape
    n1 = 2 * l // LANES
    fa, ga, f2, f2i = tables
    conv_sig = sig_w is not None
    col = lambda off: (lambda j, p: (p, 0, off // LANES + j))
    wcol = lambda off: (lambda j, p: (0, off // LANES + j))
    in_specs = [pl.BlockSpec((2, l, LANES), col(sig_off))]
    args = [sig]
    if conv_sig:
        in_specs += [pl.BlockSpec((3, LANES), wcol(sig_w)), pl.BlockSpec((1, LANES), wcol(sig_w))]
        args += [short_w, short_b]
    in_specs += [pl.BlockSpec((2, l, LANES), col(gate_off)),
                 pl.BlockSpec((3, LANES), wcol(gate_woff)),
                 pl.BlockSpec((1, LANES), wcol(gate_woff)),
                 pl.BlockSpec((1, 4 * l, LANES), lambda j, p: (order, 0, j), pipeline_mode=pl.Buffered(1)),
                 pl.BlockSpec((1, LANES), lambda j, p: (0, j)),
                 _const_spec(fa.shape), _const_spec(ga.shape), _const_spec(f2.shape), _const_spec(f2i.shape)]
    args += [gate, short_w, short_b, hspec, hy_bias[order][None, :], fa, ga, f2, f2i]
    return pl.pallas_call(
        functools.partial(_hy_big_body, conv_sig=conv_sig, l=l),
        grid=(HY_WIDTH // LANES, b // 2),
        in_specs=in_specs,
        out_specs=pl.BlockSpec((2, l, LANES), lambda j, p: (p, 0, j)),
        out_shape=jax.ShapeDtypeStruct((b, l, HY_WIDTH), out_dtype),
        scratch_shapes=[pltpu.VMEM((n1 * X_PITCH, LANES), F32),
                        pltpu.VMEM((n1 * S_PITCH, LANES), F32),
                        pltpu.VMEM((n1 * X_PITCH, LANES), F32)],
        compiler_params=_cparams(("arbitrary", "arbitrary")),
        name="hy_big",
    )(*args)


def _cis(idx, n):
    ang = idx.astype(F32) * (2.0 * np.pi / n)
    return jnp.cos(ang), jnp.sin(ang)


def _stack_complex(cr, ci):
    return jnp.concatenate([jnp.concatenate([cr, -ci], axis=-1), jnp.concatenate([ci, cr], axis=-1)], axis=-2)


def _dense_dft_tables(l):
    n = 2 * l
    k = jnp.arange(n, dtype=jnp.int32)
    t = jnp.arange(l, dtype=jnp.int32)
    cr, ci = _cis((k[:, None] * t[None, :]) % n, n)
    fwd = _stack_complex(cr, -ci)
    inv = _stack_complex(cr.T, ci.T)
    cfr, cfi = _cis((k[:, None] * k[None, :]) % n, n)
    filt = jnp.concatenate([cfr, -cfi], axis=0)
    return fwd.astype(BF16), inv.astype(BF16), filt


def _two_stage_dft_tables(l):
    n = 2 * l
    n1 = n // LANES
    hn = n1 // 2
    n2 = jnp.arange(LANES, dtype=jnp.int32)[:, None, None]
    k1 = jnp.arange(n1, dtype=jnp.int32)[None, :, None]
    m1 = jnp.arange(n1, dtype=jnp.int32)[None, None, :]
    cr, ci = _cis((k1 * (LANES * m1 + n2)) % n, n)
    fa = _stack_complex(cr[:, :, :hn], -ci[:, :, :hn])
    ga = _stack_complex(jnp.swapaxes(cr, 1, 2)[:, :hn, :], jnp.swapaxes(ci, 1, 2)[:, :hn, :])
    fa_filt = jnp.concatenate([cr, -ci], axis=1)
    a = jnp.arange(LANES, dtype=jnp.int32)
    c2r, c2i = _cis((a[:, None] * a[None, :]) % LANES, LANES)
    f2 = _stack_complex(c2r, -c2i)
    f2i = _stack_complex(c2r, c2i)
    return (fa.astype(BF16), ga.astype(BF16), f2.astype(BF16), f2i.astype(BF16)), fa_filt, f2


def _silu(x):
    return x * jax.nn.sigmoid(x)


def _proj_out_even_body(att_ref, ga_ref, hy_ref, gh_ref, w_ref, x_ref, g_ref, o_ref):
    a = att_ref[0].astype(F32) * _silu(ga_ref[0].astype(F32))
    h = hy_ref[0].astype(F32) * _silu(gh_ref[0].astype(F32))
    mixed = jnp.concatenate([a, h], axis=-1).astype(BF16)
    res = jnp.dot(mixed, w_ref[...], preferred_element_type=F32)
    o_ref[0] = x_ref[0] + g_ref[0] * res


def _proj_out_even(att, hy, u, w_out, x, gate, tm):
    b, l, d = x.shape
    per_batch = gate.shape[0] == b
    gmap = (lambda bi, i: (bi, 0, 0)) if per_batch else (lambda bi, i: (0, 0, 0))
    return pl.pallas_call(
        _proj_out_even_body,
        grid=(b, l // tm),
        in_specs=[pl.BlockSpec((1, tm, ATTN_WIDTH), lambda bi, i: (bi, i, 0)),
                  pl.BlockSpec((1, tm, ATTN_WIDTH), lambda bi, i: (bi, i, U_GA // ATTN_WIDTH)),
                  pl.BlockSpec((1, tm, HY_WIDTH), lambda bi, i: (bi, i, 0)),
                  pl.BlockSpec((1, tm, HY_WIDTH), lambda bi, i: (bi, i, U_GH // HY_WIDTH)),
                  _const_spec(w_out.shape),
                  pl.BlockSpec((1, tm, d), lambda bi, i: (bi, i, 0)),
                  pl.BlockSpec((1, 1, d), gmap)],
        out_specs=pl.BlockSpec((1, tm, d), lambda bi, i: (bi, i, 0)),
        out_shape=jax.ShapeDtypeStruct((b, l, d), F32),
        compiler_params=_cparams(("arbitrary", "arbitrary")),
        name="proj_out_even",
    )(att, u, hy, u, w_out, x, gate)


def _proj_out_odd_body(yf_ref, yb_ref, gt_ref, w_ref, x_ref, g_ref, fg_ref, o_ref, *, final_norm):
    y = yf_ref[0].astype(F32) + yb_ref[0].astype(F32)
    mixed = (y * _silu(gt_ref[0].astype(F32))).astype(BF16)
    res = jnp.dot(mixed, w_ref[...], preferred_element_type=F32)
    xn = x_ref[0] + g_ref[0] * res
    if final_norm:
        ms = jnp.mean(xn * xn, axis=-1, keepdims=True)
        xn = xn * lax.rsqrt(ms + EPS) * fg_ref[...]
    o_ref[0] = xn


def _proj_out_odd(yf, yb, u2, w_out, x, gate, final_g, final_norm, tm):
    b, l, d = x.shape
    per_batch = gate.shape[0] == b
    gmap = (lambda bi, i: (bi, 0, 0)) if per_batch else (lambda bi, i: (0, 0, 0))
    return pl.pallas_call(
        functools.partial(_proj_out_odd_body, final_norm=final_norm),
        grid=(b, l // tm),
        in_specs=[pl.BlockSpec((1, tm, d), lambda bi, i: (bi, i, 0)),
                  pl.BlockSpec((1, tm, d), lambda bi, i: (bi, i, 0)),
                  pl.BlockSpec((1, tm, d), lambda bi, i: (bi, i, 1)),
                  _const_spec(w_out.shape),
                  pl.BlockSpec((1, tm, d), lambda bi, i: (bi, i, 0)),
                  pl.BlockSpec((1, 1, d), gmap),
                  pl.BlockSpec((1, d), lambda bi, i: (0, 0))],
        out_specs=pl.BlockSpec((1, tm, d), lambda bi, i: (bi, i, 0)),
        out_shape=jax.ShapeDtypeStruct((b, l, d), F32),
        compiler_params=_cparams(("arbitrary", "arbitrary")),
        name="proj_out_odd",
    )(yf, yb, u2, w_out, x, gate, final_g.reshape(1, d))


def _rmsnorm_body(x_ref, g_ref, o_ref):
    x = x_ref[0]
    ms = jnp.mean(x * x, axis=-1, keepdims=True)
    o_ref[0] = x * lax.rsqrt(ms + EPS) * g_ref[...]


def _rmsnorm(x, g, tm):
    b, l, d = x.shape
    return pl.pallas_call(
        _rmsnorm_body,
        grid=(b, l // tm),
        in_specs=[pl.BlockSpec((1, tm, d), lambda bi, i: (bi, i, 0)),
                  pl.BlockSpec((1, d), lambda bi, i: (0, 0))],
        out_specs=pl.BlockSpec((1, tm, d), lambda bi, i: (bi, i, 0)),
        out_shape=jax.ShapeDtypeStruct((b, l, d), F32),
        compiler_params=_cparams(("arbitrary", "arbitrary")),
        name="final_rmsnorm",
    )(x, g.reshape(1, d))


LRU_GROUP = 8


LRU_HALO = 2 * LRU_GROUP


def _lru_gates(d, x_ref, xp_ref, xn_ref, keep_prev, keep_next, cw_ref, cb_ref, wa_ref, ba_ref, wx_ref, bx_ref,
               lam_ref, x_scr, a_scr, b_scr):
    tt = x_ref.shape[1]
    g = LRU_GROUP
    nk = LRU_BLOCK // LANES
    for s in range(g):
        for k in range(nk):
            lanes = slice(k * LANES, (k + 1) * LANES)
            x_scr[d, k, pl.ds(LRU_HALO + s, tt, stride=g), :] = x_ref[s, :, lanes].astype(F32)
            x_scr[d, k, pl.ds(s, 1), :] = xp_ref[s, 6:7, lanes].astype(F32) * keep_prev
            x_scr[d, k, pl.ds(g + s, 1), :] = xp_ref[s, 7:8, lanes].astype(F32) * keep_prev
            x_scr[d, k, pl.ds(LRU_HALO + tt * g + s, 1), :] = xn_ref[s, 0:1, lanes].astype(F32) * keep_next
    xcs = []
    for k in range(nk):
        lanes = slice(k * LANES, (k + 1) * LANES)
        acc = cb_ref[:, lanes]
        for j in range(4):
            acc = acc + x_scr[d, k, pl.ds(j * g, tt * g), :] * cw_ref[j:j + 1, lanes]
        xcs.append(acc)
    xc = jnp.concatenate(xcs, axis=1)
    xcb = xc.astype(BF16)
    lam = lam_ref[d:d + 1, :]
    softplus_neg = jnp.maximum(-lam, 0.0) + jnp.log1p(jnp.exp(-jnp.abs(lam)))
    tr = jnp.tanh(jnp.dot(xcb, wa_ref[d, 0], preferred_element_type=F32) + ba_ref[d:d + 1, :])
    ti = jnp.tanh(jnp.dot(xcb, wx_ref[d, 0], preferred_element_type=F32) + bx_ref[d:d + 1, :])
    half_c = (-0.5 * LRU_C) * softplus_neg
    log_a = half_c * tr + half_c
    a = jnp.exp(log_a)
    w4 = jnp.tanh(log_a) * (-0.25 * (a * a) - 0.25)
    root = jnp.where(w4 > 0.0, w4 * lax.rsqrt(w4), 0.0)
    b = root * (ti + 1.0) * xc
    for k in range(nk):
        a_scr[d, k] = a[:, k * LANES:(k + 1) * LANES]
        b_scr[d, k] = b[:, k * LANES:(k + 1) * LANES]


def _lru_body(xf_ref, xfp_ref, xfn_ref, xb_ref, xbp_ref, xbn_ref, cw_ref, cb_ref, wa_ref, ba_ref, wx_ref, bx_ref,
              lam_ref, h0_ref, yf_ref, yb_ref, fin_ref, x_scr, a_scr, b_scr, y_scr, h_scr):
    i = pl.program_id(2)
    nt = pl.num_programs(2)
    tt = xf_ref.shape[1]
    nk = LRU_BLOCK // LANES

    @pl.when(i == 0)
    def _():
        h_scr[...] = h0_ref[...]

    first = jnp.where(i == 0, 0.0, 1.0)
    last = jnp.where(i == nt - 1, 0.0, 1.0)
    params = (cw_ref, cb_ref, wa_ref, ba_ref, wx_ref, bx_ref, lam_ref, x_scr, a_scr, b_scr)
    _lru_gates(0, xf_ref, xfp_ref, xfn_ref, first, last, *params)
    _lru_gates(1, xb_ref, xbp_ref, xbn_ref, last, first, *params)

    def step(t, carry):
        rf = pl.ds(pl.multiple_of(t * LRU_GROUP, LRU_GROUP), LRU_GROUP)
        rb = pl.ds(pl.multiple_of((tt - 1 - t) * LRU_GROUP, LRU_GROUP), LRU_GROUP)
        out = []
        for d, rows in ((0, rf), (1, rb)):
            for k in range(nk):
                h = a_scr[d, k, rows, :] * carry[d * nk + k] + b_scr[d, k, rows, :]
                y_scr[d, k, rows, :] = h
                out.append(h)
        return tuple(out)

    init = tuple(h_scr[d, :, k * LANES:(k + 1) * LANES] for d in range(2) for k in range(nk))
    fin = lax.fori_loop(0, tt, step, init, unroll=16)
    for d in range(2):
        for k in range(nk):
            h_scr[d, :, k * LANES:(k + 1) * LANES] = fin[d * nk + k]

    for s in range(LRU_GROUP):
        for k in range(nk):
            rows = pl.ds(s, tt, stride=LRU_GROUP)
            yf_ref[s, :, k * LANES:(k + 1) * LANES] = y_scr[0, k, rows, :].astype(yf_ref.dtype)
            yb_ref[s, :, k * LANES:(k + 1) * LANES] = y_scr[1, k, rows, :].astype(yb_ref.dtype)

    @pl.when(i == nt - 1)
    def _():
        fin_ref[...] = h_scr[...]


def _lru(u2, conv_w, conv_b, wa, ba, wx, bx, lam, h0):
    b, l, _ = u2.shape
    w = N_HEADS_C * LRU_BLOCK
    hb = LRU_BLOCK
    g = LRU_GROUP
    assert b % g == 0, "the scan packs LRU_GROUP sequences per vreg"
    tt = min(l, LRU_TIME)
    nt = l // tt
    hr = tt // 8
    nhalo = l // 8
    vec = lambda rows: pl.BlockSpec((rows, hb), lambda h, gi, i: (0, h))
    wspec = pl.BlockSpec((2, 1, hb, hb), lambda h, gi, i: (0, h, 0, 0))
    tile = lambda f: pl.BlockSpec((g, tt, hb), lambda h, gi, i: (gi, f(i), h))
    halo_prev = lambda f: pl.BlockSpec((g, 8, hb), lambda h, gi, i: (gi, jnp.maximum(f(i) * hr - 1, 0), h))
    halo_next = lambda f: pl.BlockSpec((g, 8, hb), lambda h, gi, i: (gi, jnp.minimum((f(i) + 1) * hr, nhalo - 1), h))
    fwd = lambda i: i
    bwd = lambda i: nt - 1 - i
    state = pl.BlockSpec((2, g, hb), lambda h, gi, i: (0, gi, h))
    return pl.pallas_call(
        _lru_body,
        grid=(N_HEADS_C, b // g, nt),
        in_specs=[tile(fwd), halo_prev(fwd), halo_next(fwd), tile(bwd), halo_prev(bwd), halo_next(bwd),
                  vec(4), vec(1), wspec, vec(2), wspec, vec(2), vec(2), state],
        out_specs=[tile(fwd), tile(bwd), state],
        out_shape=[jax.ShapeDtypeStruct((b, l, w), BF16),
                   jax.ShapeDtypeStruct((b, l, w), BF16),
                   jax.ShapeDtypeStruct((2, b, w), F32)],
        scratch_shapes=[pltpu.VMEM((2, hb // LANES, (tt + 3) * g, LANES), F32),
                        pltpu.VMEM((2, hb // LANES, tt * g, LANES), F32),
                        pltpu.VMEM((2, hb // LANES, tt * g, LANES), F32),
                        pltpu.VMEM((2, hb // LANES, tt * g, LANES), F32),
                        pltpu.VMEM((2, g, hb), F32)],
        compiler_params=_cparams(("arbitrary", "arbitrary", "arbitrary")),
        name="rg_lru",
    )(u2, u2, u2, u2, u2, u2, conv_w, conv_b.reshape(1, w), wa, ba, wx, bx, lam, h0)


def _even_w_in(w):
    q = w[:, 0:ATTN_WIDTH]
    k = w[:, ATTN_WIDTH:ATTN_WIDTH + KV_COLS]
    v = w[:, ATTN_WIDTH + KV_COLS:ATTN_WIDTH + 2 * KV_COLS]
    o = ATTN_WIDTH + 2 * KV_COLS
    ga = w[:, o:o + ATTN_WIDTH]
    hy = w[:, o + ATTN_WIDTH:o + ATTN_WIDTH + 3 * HY_WIDTH]
    gh = w[:, o + ATTN_WIDTH + 3 * HY_WIDTH:]
    dup = lambda m: jnp.concatenate([m[:, h * HEAD_DIM:(h + 1) * HEAD_DIM] for h in range(N_KV_A) for _ in range(2)],
                                    axis=1)
    return jnp.concatenate([q, ga, hy, gh, dup(k), dup(v)], axis=1).astype(BF16)


def _flat(x):
    return x.reshape(1, -1, x.shape[-1])


def _dup_heads(x):
    return jnp.concatenate([x[..., h, :] for h in range(N_KV_A) for _ in range(2)], axis=-1).astype(BF16)


def kernel(x_prompt, x_sample, cache_k, cache_v, state_lru, c, c_ctx, mod_w, mod_b, norm_g, final_norm_g, a_w_in, a_w_out, a_sink, hy_short_w, hy_short_b, hy_w1, hy_b1, hy_w2, hy_b2, hy_w3, hy_decay, hy_bias, c_w_in, c_w_out, c_conv_w, c_conv_b, c_wa, c_ba, c_wx, c_bx, c_lambda):
    depth, d, _ = mod_w.shape
    bp, lp, _ = x_prompt.shape
    bs, ls, _ = x_sample.shape

    cvec = jnp.concatenate([c, c_ctx[None, :], jnp.zeros((16 - bs - 1, d), F32)], axis=0)
    mod = _modulation(cvec, mod_w, mod_b)

    cos, sin = _rope_tables(ls)
    kx2 = _dup_heads(cache_k)
    vx2 = _dup_heads(cache_v)
    tabs_big, fa_filt, f2_filt = _two_stage_dft_tables(ls)
    fwd_small, inv_small, filt_small = _dense_dft_tables(lp)
    feat_p, feat_s = _hy_features(lp), _hy_features(ls)
    hid_pad = ((0, LANES - hy_w1.shape[-1]),)

    tm_s = min(ls, ROWS_NARROW_COLS)
    tm_o = min(ls, ROWS_OUT)
    tm_p = min(bp * lp, ROWS_OUT)
    xp, xs = x_prompt, x_sample
    new_k, new_v, new_s = [], [], []
    for layer in range(depth):
        j = layer // 2
        sh_s, sc_s, g_s = (mod[layer, :bs, None, i * d:(i + 1) * d] for i in range(3))
        sh_p, sc_p, g_p = (mod[layer, bs:bs + 1, None, i * d:(i + 1) * d] for i in range(3))
        last = layer == depth - 1
        if layer % 2 == 0:
            w_in = _even_w_in(a_w_in[j])
            w_out = a_w_out[j].astype(BF16)
            up = _proj_in_flat(xp, norm_g[layer], sc_p, sh_p, w_in, ROWS_WIDE_COLS, 2)
            us = _proj_in(xs, norm_g[layer], sc_s, sh_s, w_in, min(ls, ROWS_WIDE_COLS), 2)
            att_p, k_p, v_p = _attn_ctx(up, a_sink[j])
            new_k.append(k_p.reshape(bp, lp, N_KV_A, HEAD_DIM))
            new_v.append(v_p.reshape(bp, lp, N_KV_A, HEAD_DIM))
            k2r = _rope_k(us, cos, sin, min(ls, ROWS_ROPE))
            att_s = _attn_lat(us, k2r, kx2, vx2, j, a_sink[j], cos, sin)

            w1 = jnp.pad(hy_w1[j], ((0, LANES - HY_EMB),) + hid_pad)
            b1 = jnp.pad(hy_b1[j][None, :], ((0, 0),) + hid_pad)
            w2 = jnp.pad(hy_w2[j], hid_pad + hid_pad)
            b2 = jnp.pad(hy_b2[j][None, :], ((0, 0),) + hid_pad)
            w3 = jnp.pad(hy_w3[j], hid_pad + ((0, 0),))
            decay = hy_decay[j].reshape(2 * HY_ORDER, 1, HY_WIDTH)
            sw, sb = hy_short_w[j], hy_short_b[j][None, :]
            hspec_p = _hy_filter_small(_hy_hidden(feat_p, w1, b1, w2, b2), w3, decay, filt_small, lp)
            hspec_s = _hy_filter_big(_hy_hidden(feat_s, w1, b1, w2, b2), w3, decay,
                                     fa_filt.astype(BF16), f2_filt.astype(BF16), ls)
            hy_p = _hy_small(up, sw, sb, hspec_p, hy_bias[j], fwd_small, inv_small)
            z1 = _hy_big(us, U_Z, 2 * HY_WIDTH, us, U_X1, 0, sw, sb, hspec_s, 0, hy_bias[j], tabs_big, F32)
            hy_s = _hy_big(z1, 0, None, us, U_X2, HY_WIDTH, sw, sb, hspec_s, 1, hy_bias[j], tabs_big, BF16)

            xp = _proj_out_even(_flat(att_p), _flat(hy_p), _flat(up), w_out, _flat(xp), g_p, tm_p).reshape(bp, lp, d)
            xs = _proj_out_even(att_s, hy_s, us, w_out, xs, g_s, tm_o)
            if last:
                xp = _rmsnorm(xp, final_norm_g, lp)
                xs = _rmsnorm(xs, final_norm_g, tm_o)
        else:
            w_in = c_w_in[j].astype(BF16)
            w_out = c_w_out[j].astype(BF16)
            wa = (0.5 * c_wa[j]).astype(BF16)
            wx = (0.5 * c_wx[j]).astype(BF16)
            up = _proj_in_flat(xp, norm_g[layer], sc_p, sh_p, w_in, ROWS_NARROW_COLS, 2)
            us = _proj_in(xs, norm_g[layer], sc_s, sh_s, w_in, tm_s, 2)
            lru_args = (c_conv_w[j], c_conv_b[j], wa, 0.5 * c_ba[j], wx, 0.5 * c_bx[j], c_lambda[j])
            yf_p, yb_p, fin_p = _lru(up, *lru_args, jnp.zeros((2, bp, d), F32))
            yf_s, yb_s, _ = _lru(us, *lru_args, jnp.moveaxis(state_lru[:, j], 1, 0))
            new_s.append(jnp.moveaxis(fin_p, 0, 1))
            xp = _proj_out_odd(_flat(yf_p), _flat(yb_p), _flat(up), w_out, _flat(xp), g_p, final_norm_g, last,
                               tm_p).reshape(bp, lp, d)
            xs = _proj_out_odd(yf_s, yb_s, us, w_out, xs, g_s, final_norm_g, last, tm_o)

    return (xp, xs, jnp.stack(new_k, axis=1), jnp.stack(new_v, axis=1), jnp.stack(new_s, axis=1))
```
